```python
import jax, jax.numpy as jnp
from jax import lax
import numpy as np

D_MODEL = 1024
BATCH = 8
SEQ = 8192
DEPTH = 1
DEC_BATCH = 32
DEC_SEQ = 2048
PAST_LEN = 128

MIX = D_MODEL
HEAD_DIM = 64
GA = MIX // 2
GB = MIX - GA
HA = GA // HEAD_DIM
HB = GB // HEAD_DIM
CHUNK = 128
LORA_W = 64
LORA_A = 64
LORA_G = 128
N_DIR = 2
SHIFT = 3
B_CONV = 3 * GB + N_DIR * (LORA_W + LORA_A) + LORA_G
IN_COLS = 2 * GA + B_CONV
N_EXPERTS = 32
TOP_K = 4
D_FF = D_MODEL
SWIGLU_ALPHA = 1.702
SWIGLU_LIMIT = 7.0
MOE_BLOCK = 256
PLE_DIM = 256
EPS = 1e-6
GN_EPS = 64e-5

kernel_name = "hybrid_gmlp_rwkv7_moe_encoder"


def _rmsnorm(x, g):
    xf = x.astype(jnp.float32)
    y = xf * lax.rsqrt(jnp.mean(xf * xf, axis=-1, keepdims=True) + EPS)
    return (y * g.astype(jnp.float32)).astype(x.dtype)


def _mixer_a(u, v, ln_g, ln_b, ws, bs):
    bsz, t, _ = u.shape
    f32 = jnp.float32
    u = jax.nn.gelu(u.astype(f32), approximate=False)
    v = jax.nn.gelu(v.astype(f32), approximate=False)
    mu = jnp.mean(v, axis=-1, keepdims=True)
    var = jnp.mean(jnp.square(v - mu), axis=-1, keepdims=True)
    v = (v - mu) * lax.rsqrt(var + EPS) * ln_g.astype(f32) + ln_b.astype(f32)
    v = v.reshape(bsz, t // CHUNK, CHUNK, HA, HEAD_DIM)
    s = jnp.einsum("hij,bnjhd->bnihd", ws.astype(f32), v) + bs.astype(f32).T[:, :, None]
    return u * s.reshape(bsz, t, GA)


def _wkv(r, w, k, v, a, b, reverse):
    bsz, t, h, n = r.shape
    xs = tuple(jnp.swapaxes(z, 0, 1) for z in (r, w, k, v, a, b))

    def step(S, inp):
        r_t, w_t, k_t, v_t, a_t, b_t = inp
        sa = jnp.einsum("bhvk,bhk->bhv", S, a_t)
        S = S * w_t[:, :, None, :] + sa[..., None] * b_t[:, :, None, :] + v_t[..., None] * k_t[:, :, None, :]
        return S, jnp.einsum("bhvk,bhk->bhv", S, r_t)

    S0 = jnp.zeros((bsz, h, n, n), jnp.float32)
    _, ys = lax.scan(step, S0, xs, reverse=reverse)
    return jnp.swapaxes(ys, 0, 1)


def _mixer_b(zb, conv, w0, w2, a0, a2, g2, k_k, k_a, r_k, gn_g, gn_b):
    f32 = jnp.float32
    bsz, t, _ = zb.shape
    zb = zb.astype(f32)
    conv = conv.astype(f32)
    zp = jnp.pad(zb, ((0, 0), (1, 1), (0, 0)))
    zc = zp[:, :-2] * conv[0] + zp[:, 1:-1] * conv[1] + zp[:, 2:] * conv[2]

    def heads(z):
        return z.reshape(bsz, t, HB, HEAD_DIM)

    r = heads(zc[..., :GB])
    k = heads(zc[..., GB:2 * GB])
    v = heads(zc[..., 2 * GB:3 * GB])
    o = 3 * GB
    xw = zc[..., o:o + N_DIR * LORA_W].reshape(bsz, t, N_DIR, LORA_W)
    o += N_DIR * LORA_W
    xa = zc[..., o:o + N_DIR * LORA_A].reshape(bsz, t, N_DIR, LORA_A)
    o += N_DIR * LORA_A
    xg = zc[..., o:]
    g = jax.nn.sigmoid(xg) @ g2.astype(f32)
    kk = k * k_k.astype(f32).reshape(HB, HEAD_DIM)
    kk = kk / jnp.maximum(jnp.linalg.norm(kk, axis=-1, keepdims=True), 1e-12)
    k_a = k_a.astype(f32).reshape(HB, HEAD_DIM)
    r_k = r_k.astype(f32)
    y_dirs = []
    bonus_dirs = []
    for d in range(N_DIR):
        w_log = -jax.nn.softplus(-(w0[d].astype(f32) + jnp.tanh(xw[:, :, d]) @ w2[d].astype(f32))) - 0.5
        decay = heads(jnp.exp(-jnp.exp(w_log)))
        a = heads(jax.nn.sigmoid(a0[d].astype(f32) + xa[:, :, d] @ a2[d].astype(f32)))
        k_d = k * (1.0 + (a - 1.0) * k_a)
        y_dirs.append(_wkv(r, decay, k_d, v, -kk, kk * a, reverse=(d == 1)))
        bonus_dirs.append(jnp.sum(r * k_d * r_k, axis=-1, keepdims=True) * v)
    y = y_dirs[0] + y_dirs[1]
    mu = jnp.mean(y, axis=-1, keepdims=True)
    var = jnp.mean(jnp.square(y - mu), axis=-1, keepdims=True)
    y = ((y - mu) * lax.rsqrt(var + GN_EPS)).reshape(bsz, t, GB)
    y = y * gn_g.astype(f32) + gn_b.astype(f32) + (bonus_dirs[0] + bonus_dirs[1]).reshape(bsz, t, GB)
    return y * g


def _moe(x, router_w, router_b, w1, b1, w2, b2):
    bsz, t, d = x.shape
    n = bsz * t
    xf = x.reshape(n, d)
    logits = (xf @ router_w + router_b).astype(jnp.float32)
    vals, ids = lax.top_k(logits, TOP_K)
    gates = jax.nn.softmax(vals, axis=-1).astype(x.dtype)
    m = n * TOP_K
    flat_e = ids.reshape(m)
    flat_tok = jnp.arange(m, dtype=jnp.int32) // TOP_K
    flat_g = gates.reshape(m)
    order = jnp.argsort(flat_e)
    sorted_e = flat_e[order]
    counts = jnp.bincount(flat_e, length=N_EXPERTS)
    padded = (counts + MOE_BLOCK - 1) // MOE_BLOCK * MOE_BLOCK
    starts = jnp.cumsum(counts) - counts
    pends = jnp.cumsum(padded)
    pstarts = pends - padded
    dest = pstarts[sorted_e] + jnp.arange(m, dtype=jnp.int32) - starts[sorted_e]
    n_blocks = -(-m // MOE_BLOCK) + N_EXPERTS
    rows = n_blocks * MOE_BLOCK
    row_tok = jnp.full((rows,), n, jnp.int32).at[dest].set(flat_tok[order])
    row_gate = jnp.zeros((rows,), x.dtype).at[dest].set(flat_g[order])
    block_e = jnp.minimum(
        jnp.searchsorted(pends, jnp.arange(n_blocks, dtype=jnp.int32) * MOE_BLOCK, side="right"),
        N_EXPERTS - 1)
    x_pad = jnp.concatenate([xf, jnp.zeros((1, d), x.dtype)], axis=0)

    def step(y, inp):
        tok, gt, e = inp
        hdn = x_pad[tok] @ w1[e] + b1[e]
        glu = jnp.minimum(hdn[:, :D_FF], SWIGLU_LIMIT)
        lin = jnp.clip(hdn[:, D_FF:], -SWIGLU_LIMIT, SWIGLU_LIMIT)
        act = glu * jax.nn.sigmoid(SWIGLU_ALPHA * glu) * (lin + 1.0)
        out = (act @ w2[e] + b2[e]) * gt[:, None]
        return y.at[tok].add(out), None

    y, _ = lax.scan(step, jnp.zeros((n + 1, d), x.dtype),
                    (row_tok.reshape(n_blocks, MOE_BLOCK), row_gate.reshape(n_blocks, MOE_BLOCK), block_e))
    return y[:n].reshape(bsz, t, d)


def _forward(x, p, norm_mix, w_in, a_ln_g, a_ln_b, a_ws, a_bs, b_conv, b_w0, b_w2, b_a0, b_a2,
             b_g2, b_kk, b_ka, b_rk, b_gn_g, b_gn_b, w_out, norm_ffn, router_w, router_b,
             moe_w1, moe_b1, moe_w2, moe_b2, norm_ple, ple_proj, ple_gate, norm_final):
    h = x
    for l in range(DEPTH):
        z = _rmsnorm(h, norm_mix[l]) @ w_in[l]
        ya = _mixer_a(z[..., :GA], z[..., GA:2 * GA], a_ln_g[l], a_ln_b[l], a_ws[l], a_bs[l])
        yb = _mixer_b(z[..., 2 * GA:], b_conv[l], b_w0[l], b_w2[l], b_a0[l], b_a2[l], b_g2[l],
                      b_kk[l], b_ka[l], b_rk[l], b_gn_g[l], b_gn_b[l])
        h = h + jnp.concatenate([ya, yb], axis=-1).astype(h.dtype) @ w_out[l]
        h = h + _moe(_rmsnorm(h, norm_ffn[l]), router_w[l], router_b[l],
                     moe_w1[l], moe_b1[l], moe_w2[l], moe_b2[l])
        gate = jax.nn.sigmoid(_rmsnorm(h, norm_ple[l]) @ ple_gate[l])
        h = h + (p[l] @ ple_proj[l]) * gate
    return _rmsnorm(h, norm_final)


def setup_inputs(seed: int = 0) -> dict:
    key = jax.random.key(seed)
    ks = iter(jax.random.split(key, 40))
    f32 = jnp.float32
    L = DEPTH

    def nrm(shape, scale):
        return scale * jax.random.normal(next(ks), shape, f32)

    taps = jnp.array([0.25, 0.5, 0.25], f32)[None, :, None]
    return {
        "x_prompt": nrm((BATCH, SEQ, D_MODEL), 1.0),
        "x_sample": nrm((DEC_BATCH, DEC_SEQ, D_MODEL), 1.0),
        "p_prompt": nrm((DEPTH, BATCH, SEQ, PLE_DIM), 1.0),
        "p_sample": nrm((DEPTH, DEC_BATCH, DEC_SEQ, PLE_DIM), 1.0),
        "norm_mix": 1.0 + nrm((L, D_MODEL), 0.02),
        "w_in": nrm((L, D_MODEL, IN_COLS), D_MODEL ** -0.5),
        "a_ln_g": 1.0 + nrm((L, GA), 0.02),
        "a_ln_b": nrm((L, GA), 0.02),
        "a_ws": nrm((L, HA, CHUNK, CHUNK), CHUNK ** -0.5),
        "a_bs": 1.0 + nrm((L, HA, CHUNK), 0.02),
        "b_conv": taps + nrm((L, SHIFT, B_CONV), 0.05),
        "b_w0": -1.0 + nrm((L, N_DIR, GB), 1.0),
        "b_w2": nrm((L, N_DIR, LORA_W, GB), 0.1 * LORA_W ** -0.5),
        "b_a0": nrm((L, N_DIR, GB), 0.5),
        "b_a2": nrm((L, N_DIR, LORA_A, GB), 0.1 * LORA_A ** -0.5),
        "b_g2": nrm((L, LORA_G, GB), LORA_G ** -0.5),
        "b_kk": 0.85 + nrm((L, GB), 0.02),
        "b_ka": 1.0 + nrm((L, GB), 0.02),
        "b_rk": nrm((L, HB, HEAD_DIM), 0.1),
        "b_gn_g": 1.0 + nrm((L, GB), 0.02),
        "b_gn_b": nrm((L, GB), 0.02),
        "w_out": nrm((L, MIX, D_MODEL), MIX ** -0.5),
        "norm_ffn": 1.0 + nrm((L, D_MODEL), 0.02),
        "router_w": nrm((L, D_MODEL, N_EXPERTS), D_MODEL ** -0.5),
        "router_b": nrm((L, N_EXPERTS), 0.01),
        "moe_w1": nrm((L, N_EXPERTS, D_MODEL, 2 * D_FF), D_MODEL ** -0.5),
        "moe_b1": nrm((L, N_EXPERTS, 2 * D_FF), 0.01),
        "moe_w2": nrm((L, N_EXPERTS, D_FF, D_MODEL), D_FF ** -0.5),
        "moe_b2": nrm((L, N_EXPERTS, D_MODEL), 0.01),
        "norm_ple": 1.0 + nrm((L, D_MODEL), 0.02),
        "ple_proj": nrm((L, PLE_DIM, D_MODEL), PLE_DIM ** -0.5),
        "ple_gate": nrm((L, D_MODEL, D_MODEL), D_MODEL ** -0.5),
        "norm_final": 1.0 + nrm((D_MODEL,), 0.02),
    }


def reference(x_prompt, x_sample, p_prompt, p_sample, norm_mix, w_in, a_ln_g, a_ln_b, a_ws, a_bs,
              b_conv, b_w0, b_w2, b_a0, b_a2, b_g2, b_kk, b_ka, b_rk, b_gn_g, b_gn_b, w_out,
              norm_ffn, router_w, router_b, moe_w1, moe_b1, moe_w2, moe_b2, norm_ple, ple_proj,
              ple_gate, norm_final):
    weights = (norm_mix, w_in, a_ln_g, a_ln_b, a_ws, a_bs, b_conv, b_w0, b_w2, b_a0, b_a2,
               b_g2, b_kk, b_ka, b_rk, b_gn_g, b_gn_b, w_out, norm_ffn, router_w, router_b,
               moe_w1, moe_b1, moe_w2, moe_b2, norm_ple, ple_proj, ple_gate, norm_final)
    y_prompt = _forward(x_prompt, p_prompt, *weights)
    y_sample = _forward(x_sample, p_sample, *weights)
    return (y_prompt, y_sample)
```

```python
import functools
import math

import jax
import jax.numpy as jnp
from jax import lax
from jax.experimental import pallas as pl
from jax.experimental.pallas import tpu as pltpu

F32 = jnp.float32
BF16 = jnp.bfloat16
I32 = jnp.int32

D_MODEL = 1024
HEAD_DIM = 64
GA = 512
GB = 512
N_PAIR = GB // 128
CHUNK_A = 128
LORA = 64
LORA_G = 128
B_CONV = 3 * GB + 4 * LORA + LORA_G
N_EXPERTS = 32
TOP_K = 4
D_FF = 1024
PLE_DIM = 256
SWIGLU_ALPHA = 1.702
SWIGLU_LIMIT = 7.0
EPS = 1e-6
GN_EPS = 64e-5
DECAY_SCALE = math.exp(-0.5)

LANES = 128
TILE_IN = 256
WKV_STEP = 256
WKV_CHUNK = 64
TILE_POST = 256
TILE_ROW = 256
MOE_BLOCK = 256
VMEM_LIMIT = 56 * 1024 * 1024

NT = (((1,), (1,)), ((), ()))
TN = (((0,), (0,)), ((), ()))


def _mm(a, b):
    return jnp.dot(a, b, preferred_element_type=F32)


def _split2(q):
    hi = q.astype(BF16)
    lo = (q - hi.astype(F32)).astype(BF16)
    return hi, lo


def _segsum(q, bd):
    hi, lo = _split2(q)
    return _mm(hi, bd) + _mm(lo, bd)


def _gelu(z):
    return 0.5 * z * (1.0 + lax.erf(z * (1.0 / math.sqrt(2.0))))


def _rms(xv, g):
    ms = jnp.mean(xv * xv, axis=-1, keepdims=True)
    return xv * lax.rsqrt(ms + EPS) * g


def _inproj_kernel(x_ref, xp_ref, xn_ref, nm_ref, win_ref, lng_ref, lnb_ref, ws_ref, bs_ref,
                   conv_ref, w0_ref, w2_ref, a0_ref, a2_ref, g2_ref, kk_ref, ka_ref, rk_ref, bd_ref,
                   ya_ref, r_ref, v_ref, kn_ref, lw0_ref, lw1_ref, k0_ref, k1_ref, b0_ref, b1_ref,
                   g_ref, bonus_ref, zb_scr):
    tt = x_ref.shape[0]
    i = pl.program_id(1)
    last = pl.num_programs(1) - 1
    nm = nm_ref[...]

    xn = _rms(x_ref[...], nm).astype(BF16)
    za = _mm(xn, win_ref[:, :2 * GA])
    zb_scr[8:8 + tt, :] = _mm(xn, win_ref[:, 2 * GA:])
    zp = _mm(_rms(xp_ref[...], nm).astype(BF16), win_ref[:, 2 * GA:])
    zn = _mm(_rms(xn_ref[...], nm).astype(BF16), win_ref[:, 2 * GA:])
    zb_scr[0:8, :] = jnp.where(i > 0, zp, 0.0)
    zb_scr[8 + tt:16 + tt, :] = jnp.where(i < last, zn, 0.0)

    u = _gelu(za[:, :GA])
    v = _gelu(za[:, GA:])
    mu = jnp.mean(v, axis=-1, keepdims=True)
    var = jnp.mean(jnp.square(v - mu), axis=-1, keepdims=True)
    v = ((v - mu) * lax.rsqrt(var + EPS) * lng_ref[...] + lnb_ref[...]).astype(BF16)
    lane_head = lax.broadcasted_iota(I32, (CHUNK_A, GA), 1) // HEAD_DIM
    for c in range(tt // CHUNK_A):
        rows = slice(c * CHUNK_A, (c + 1) * CHUNK_A)
        o = _mm(ws_ref[...], v[rows])
        s = bs_ref[...]
        for h in range(GA // HEAD_DIM):
            s = s + jnp.where(lane_head == h, o[h * CHUNK_A:(h + 1) * CHUNK_A], 0.0)
        ya_ref[rows, :] = (u[rows] * s).astype(ya_ref.dtype)

    conv = conv_ref[...]
    zc = (zb_scr[7:7 + tt, :] * conv[0:1] + zb_scr[8:8 + tt, :] * conv[1:2]
          + zb_scr[9:9 + tt, :] * conv[2:3])
    r = zc[:, :GB]
    k = zc[:, GB:2 * GB]
    vv = zc[:, 2 * GB:3 * GB]
    o0 = 3 * GB
    xw = jnp.tanh(zc[:, o0:o0 + 2 * LORA]).astype(BF16)
    xa = zc[:, o0 + 2 * LORA:o0 + 4 * LORA].astype(BF16)
    xg = jax.nn.sigmoid(zc[:, o0 + 4 * LORA:]).astype(BF16)
    bd = bd_ref[...]
    kk = k * kk_ref[...]
    kk = kk / jnp.maximum(jnp.sqrt(_segsum(kk * kk, bd)), 1e-12)
    ka = ka_ref[...]
    lw_refs = (lw0_ref, lw1_ref)
    k_refs = (k0_ref, k1_ref)
    b_refs = (b0_ref, b1_ref)
    ksum = None
    for d in range(2):
        yw = w0_ref[d:d + 1, :] + _mm(xw, w2_ref[d])
        lw = -DECAY_SCALE * jax.nn.sigmoid(yw)
        a = jax.nn.sigmoid(a0_ref[d:d + 1, :] + _mm(xa, a2_ref[d]))
        kd = k * (1.0 + (a - 1.0) * ka)
        bb = kk * a
        ksum = kd if ksum is None else ksum + kd
        for p in range(N_PAIR):
            ls = slice(p * LANES, (p + 1) * LANES)
            lw_refs[d][p] = lw[:, ls]
            k_refs[d][p] = kd[:, ls].astype(k0_ref.dtype)
            b_refs[d][p] = bb[:, ls].astype(b0_ref.dtype)
    for p in range(N_PAIR):
        ls = slice(p * LANES, (p + 1) * LANES)
        r_ref[p] = r[:, ls].astype(r_ref.dtype)
        v_ref[p] = vv[:, ls].astype(v_ref.dtype)
        kn_ref[p] = kk[:, ls].astype(kn_ref.dtype)
    g_ref[...] = _mm(xg, g2_ref[...])
    bonus_ref[...] = _segsum(r * ksum * rk_ref[...], bd) * vv


def _inproj(x, w):
    bsz, t, _ = x.shape
    tt = min(TILE_IN, t)
    nt = t // tt
    t8 = tt // 8

    def full(a):
        nd = a.ndim
        return pl.BlockSpec(a.shape, lambda b, i: (0,) * nd)

    consts = (w["norm_mix"], w["w_in"], w["a_ln_g"], w["a_ln_b"], w["a_ws"], w["a_bs"], w["b_conv"],
              w["b_w0"], w["b_w2"], w["b_a0"], w["b_a2"], w["b_g2"], w["b_kk"], w["b_ka"], w["b_rk"],
              w["bd"])
    in_specs = [
        pl.BlockSpec((None, tt, D_MODEL), lambda b, i: (b, i, 0)),
        pl.BlockSpec((None, 8, D_MODEL), lambda b, i: (b, jnp.maximum(i * t8 - 1, 0), 0)),
        pl.BlockSpec((None, 8, D_MODEL), lambda b, i: (b, jnp.minimum((i + 1) * t8, t // 8 - 1), 0)),
    ] + [full(a) for a in consts]
    pair = lambda dt: jax.ShapeDtypeStruct((bsz, N_PAIR, t, LANES), dt)
    flat = lambda dt: jax.ShapeDtypeStruct((bsz, t, GB), dt)
    pair_spec = pl.BlockSpec((None, N_PAIR, tt, LANES), lambda b, i: (b, 0, i, 0))
    flat_spec = pl.BlockSpec((None, tt, GB), lambda b, i: (b, i, 0))
    out_shape = (flat(BF16),
                 pair(BF16), pair(BF16), pair(BF16),
                 pair(F32), pair(F32),
                 pair(BF16), pair(BF16), pair(BF16), pair(BF16),
                 flat(F32), flat(F32))
    out_specs = (flat_spec,) + (pair_spec,) * 9 + (flat_spec, flat_spec)
    return pl.pallas_call(
        _inproj_kernel,
        grid=(bsz, nt),
        in_specs=in_specs,
        out_specs=out_specs,
        out_shape=out_shape,
        scratch_shapes=[pltpu.VMEM((tt + 16, B_CONV), F32)],
        compiler_params=pltpu.CompilerParams(
            dimension_semantics=("parallel", "parallel"), vmem_limit_bytes=VMEM_LIMIT),
        name="inproj",
    )(x, x, x, *consts)


def _wkv_kernel(r_ref, v_ref, kn_ref, k_ref, b_ref, lw_ref, y_ref, s_scr, *, reverse):
    n = WKV_STEP
    j = pl.program_id(2)

    @pl.when(j == 0)
    def _():
        s_scr[...] = jnp.zeros_like(s_scr)

    row = lax.broadcasted_iota(I32, (n, n), 0)
    col = lax.broadcasted_iota(I32, (n, n), 1)

    def same(log2):
        return (row >> log2) == (col >> log2)

    m8, m16, m32, m64 = same(3), same(4), same(5), same(6)
    strict = m64 & ((col > row) if reverse else (col < row))
    incl = m64 & ((col >= row) if reverse else (col <= row))
    eye = jnp.where(row == col, 1.0, 0.0).astype(F32)
    er = lax.broadcasted_iota(I32, (HEAD_DIM, HEAD_DIM), 0)
    ec = lax.broadcasted_iota(I32, (HEAD_DIM, HEAD_DIM), 1)
    eye_h = er == ec

    lw = lw_ref[...]
    l1 = lw.astype(BF16)
    r1 = lw - l1.astype(F32)
    l2 = r1.astype(BF16)
    l3 = (r1 - l2.astype(F32)).astype(BF16)
    tri = jnp.where(incl, 1.0, 0.0).astype(BF16)
    blk = jnp.where(m64, 1.0, 0.0).astype(BF16)
    cum = _mm(tri, l1) + _mm(tri, l2) + _mm(tri, l3)
    tot = _mm(blk, l1) + _mm(blk, l2) + _mm(blk, l3)
    wtot = jnp.exp(tot)
    rr = r_ref[...].astype(F32)
    kn = kn_ref[...].astype(F32)
    kd = k_ref[...].astype(F32)
    bb = b_ref[...].astype(F32)
    winv = jnp.exp(-cum)
    wd = jnp.exp(tot - cum)
    rt = (rr * jnp.exp(cum)).astype(BF16)
    at = (-kn * jnp.exp(cum - lw)).astype(BF16)
    bt = (bb * winv).astype(BF16)
    kt = (kd * winv).astype(BF16)
    bh = (bb * wd).astype(BF16)
    kh = (kd * wd).astype(BF16)
    vb = v_ref[...].astype(BF16)

    for hh in range(2):
        sl = slice(hh * HEAD_DIM, (hh + 1) * HEAD_DIM)
        a_h, r_h, b_h, k_h, v_h = at[:, sl], rt[:, sl], bt[:, sl], kt[:, sl], vb[:, sl]
        ab = lax.dot_general(a_h, b_h, NT, preferred_element_type=F32)
        lab = jnp.where(strict, ab, 0.0)
        lak = jnp.where(strict, lax.dot_general(a_h, k_h, NT, preferred_element_type=F32), 0.0).astype(BF16)
        mrb = jnp.where(incl, lax.dot_general(r_h, b_h, NT, preferred_element_type=F32), 0.0).astype(BF16)
        mrk = jnp.where(incl, lax.dot_general(r_h, k_h, NT, preferred_element_type=F32), 0.0).astype(BF16)

        l8 = jnp.where(m8, lab, 0.0)
        p1 = l8.astype(BF16)
        p2 = _mm(p1, p1).astype(BF16)
        p4 = _mm(p2, p2).astype(BF16)
        x = eye + l8
        x = x + _mm(x.astype(BF16), p2)
        x = x + _mm(x.astype(BF16), p4)
        prev = m8
        for cur in (m16, m32, m64):
            loff = jnp.where(cur & jnp.logical_not(prev), lab, 0.0).astype(BF16)
            xb = x.astype(BF16)
            x = x + _mm(xb, _mm(loff, xb).astype(BF16))
            prev = cur
        tb = x.astype(BF16)

        x1 = _mm(lak, v_h).astype(BF16)
        wu = _mm(tb, a_h).astype(BF16)
        uv = _mm(tb, x1).astype(BF16)
        q = (r_h.astype(F32) + _mm(mrb, wu)).astype(BF16)
        yl = _mm(mrb, uv) + _mm(mrk, v_h)

        s = s_scr[hh]
        order = range(n // WKV_CHUNK - 1, -1, -1) if reverse else range(n // WKV_CHUNK)
        for c in order:
            rs = slice(c * WKV_CHUNK, (c + 1) * WKV_CHUNK)
            b_c = bh[rs, sl]
            k_c = kh[rs, sl]
            wrow = wtot[c * WKV_CHUNK:c * WKV_CHUNK + 1, sl]
            g = (lax.dot_general(b_c, wu[rs], TN, preferred_element_type=F32)
                 + jnp.where(eye_h, jnp.broadcast_to(wrow, (HEAD_DIM, HEAD_DIM)), 0.0))
            h = (lax.dot_general(b_c, uv[rs], TN, preferred_element_type=F32)
                 + lax.dot_general(k_c, v_h[rs], TN, preferred_element_type=F32))
            sb = s.astype(BF16)
            y_ref[rs, sl] = _mm(q[rs], sb) + yl[rs]
            s = _mm(g.astype(BF16), sb) + h
        s_scr[hh] = s


def _wkv(r, v, kn, k, b, lw, reverse):
    bsz, _, t, _ = r.shape
    assert t % WKV_STEP == 0
    nj = t // WKV_STEP
    if reverse:
        idx = lambda bi, p, j: (bi, p, nj - 1 - j, 0)
    else:
        idx = lambda bi, p, j: (bi, p, j, 0)
    spec = pl.BlockSpec((None, None, WKV_STEP, LANES), idx)
    return pl.pallas_call(
        functools.partial(_wkv_kernel, reverse=reverse),
        grid=(bsz, N_PAIR, nj),
        in_specs=[spec] * 6,
        out_specs=spec,
        out_shape=jax.ShapeDtypeStruct((bsz, N_PAIR, t, LANES), F32),
        scratch_shapes=[pltpu.VMEM((2, HEAD_DIM, HEAD_DIM), F32)],
        compiler_params=pltpu.CompilerParams(
            dimension_semantics=("parallel", "parallel", "arbitrary"), vmem_limit_bytes=VMEM_LIMIT),
        name="wkv_bwd" if reverse else "wkv_fwd",
    )(r, v, kn, k, b, lw)


def _post_kernel(yf_ref, yb_ref, bonus_ref, g_ref, ya_ref, x_ref, gng_ref, gnb_ref, bd_ref, wout_ref,
                 nf_ref, rw_ref, rb_ref,
                 h_ref, xn_ref, ids_ref, gate_ref, rank_ref, cnt_ref, run_scr):
    tt = x_ref.shape[0]
    step = pl.program_id(0)

    @pl.when(step == 0)
    def _():
        run_scr[...] = jnp.zeros_like(run_scr)

    y = jnp.concatenate([yf_ref[p] + yb_ref[p] for p in range(N_PAIR)], axis=1)
    bd = bd_ref[...]
    inv = 1.0 / HEAD_DIM
    mu = _segsum(y, bd) * inv
    yc = y - mu
    var = _segsum(yc * yc, bd) * inv
    yn = yc * lax.rsqrt(var + GN_EPS) * gng_ref[...] + gnb_ref[...] + bonus_ref[...]
    ybm = (yn * g_ref[...]).astype(BF16)
    h = x_ref[...] + _mm(ya_ref[...], wout_ref[:GA, :]) + _mm(ybm, wout_ref[GA:, :])
    h_ref[...] = h
    xn = _rms(h, nf_ref[...])
    xn_ref[...] = xn

    lane = lax.broadcasted_iota(I32, (tt, LANES), 1)
    logits = jnp.dot(xn, rw_ref[...], preferred_element_type=F32, precision=lax.Precision.HIGHEST)
    logits = jnp.where(lane < N_EXPERTS, logits + rb_ref[...], -jnp.inf)
    vals, ids, sel = [], [], jnp.zeros((tt, LANES), F32)
    cur = logits
    for _ in range(TOP_K):
        m = jnp.max(cur, axis=-1, keepdims=True)
        idx = jnp.min(jnp.where(cur == m, lane, LANES), axis=-1, keepdims=True)
        hit = lane == idx
        vals.append(m)
        ids.append(idx)
        sel = sel + jnp.where(hit, 1.0, 0.0)
        cur = jnp.where(hit, -jnp.inf, cur)
    es = [jnp.exp(vv - vals[0]) for vv in vals]
    den = es[0] + es[1] + es[2] + es[3]

    rt = lax.broadcasted_iota(I32, (tt, tt), 0)
    ct = lax.broadcasted_iota(I32, (tt, tt), 1)
    before = jnp.where(ct < rt, 1.0, 0.0).astype(BF16)
    cnt = _mm(before, sel.astype(BF16)) + run_scr[...]
    ids_o = jnp.zeros((tt, LANES), I32)
    gate_o = jnp.zeros((tt, LANES), F32)
    rank_o = jnp.zeros((tt, LANES), F32)
    for kx in range(TOP_K):
        rk = jnp.sum(jnp.where(lane == ids[kx], cnt, 0.0), axis=-1, keepdims=True)
        ids_o = jnp.where(lane == kx, ids[kx], ids_o)
        gate_o = jnp.where(lane == kx, es[kx] / den, gate_o)
        rank_o = jnp.where(lane == kx, rk, rank_o)
    ids_ref[...] = ids_o
    gate_ref[...] = gate_o
    rank_ref[...] = rank_o.astype(I32)
    run = run_scr[...] + jnp.sum(sel, axis=0, keepdims=True)
    run_scr[...] = run
    cnt_ref[...] = run


def _post(yf, yb, bonus, g, ya, x, w):
    bsz, t, _ = x.shape
    n = bsz * t
    tt = min(TILE_POST, t)
    nt = t // tt

    def full(a):
        nd = a.ndim
        return pl.BlockSpec(a.shape, lambda s: (0,) * nd)

    consts = (w["b_gn_g"], w["b_gn_b"], w["bd"], w["w_out"], w["norm_ffn"], w["router_w"], w["router_b"])
    pair_spec = pl.BlockSpec((None, N_PAIR, tt, LANES), lambda s: (s // nt, 0, s % nt, 0))
    tok3 = lambda width: pl.BlockSpec((None, tt, width), lambda s: (s // nt, s % nt, 0))
    tok2 = lambda width: pl.BlockSpec((tt, width), lambda s: (s, 0))
    in_specs = [pair_spec, pair_spec, tok3(GB), tok3(GB), tok3(GA), tok3(D_MODEL)] + [full(a) for a in consts]
    out_shape = (jax.ShapeDtypeStruct((n, D_MODEL), F32),
                 jax.ShapeDtypeStruct((n, D_MODEL), F32),
                 jax.ShapeDtypeStruct((n, LANES), I32),
                 jax.ShapeDtypeStruct((n, LANES), F32),
                 jax.ShapeDtypeStruct((n, LANES), I32),
                 jax.ShapeDtypeStruct((1, LANES), F32))
    out_specs = (tok2(D_MODEL), tok2(D_MODEL), tok2(LANES), tok2(LANES), tok2(LANES),
                 pl.BlockSpec((1, LANES), lambda s: (0, 0)))
    return pl.pallas_call(
        _post_kernel,
        grid=(bsz * nt,),
        in_specs=in_specs,
        out_specs=out_specs,
        out_shape=out_shape,
        scratch_shapes=[pltpu.VMEM((1, LANES), F32)],
        compiler_params=pltpu.CompilerParams(
            dimension_semantics=("arbitrary",), vmem_limit_bytes=VMEM_LIMIT),
        name="post_router",
    )(yf, yb, bonus, g, ya, x, *consts)


def _scatter_kernel(dest_ref, xn_ref, zeros_ref, xs_ref, sem):
    del zeros_ref
    tt = xn_ref.shape[0]

    def copy(tk):
        t = tk // TOP_K
        return pltpu.make_async_copy(xn_ref.at[pl.ds(t, 1)], xs_ref.at[pl.ds(dest_ref[0, tk], 1)], sem)

    def issue(tk, c):
        copy(tk).start()
        return c

    def drain(tk, c):
        copy(tk).wait()
        return c

    lax.fori_loop(0, tt * TOP_K, issue, 0)
    lax.fori_loop(0, tt * TOP_K, drain, 0)


def _scatter_rows(xn, dest, rows):
    n = xn.shape[0]
    tt = min(TILE_ROW, n)
    zeros = jnp.zeros((rows, D_MODEL), F32)
    return pl.pallas_call(
        _scatter_kernel,
        grid=(n // tt,),
        in_specs=[pl.BlockSpec((None, 1, tt * TOP_K), lambda s: (s, 0, 0), memory_space=pltpu.SMEM),
                  pl.BlockSpec((tt, D_MODEL), lambda s: (s, 0)),
                  pl.BlockSpec(memory_space=pl.ANY)],
        out_specs=pl.BlockSpec(memory_space=pl.ANY),
        out_shape=jax.ShapeDtypeStruct((rows, D_MODEL), F32),
        scratch_shapes=[pltpu.SemaphoreType.DMA(())],
        input_output_aliases={2: 0},
        compiler_params=pltpu.CompilerParams(
            dimension_semantics=("arbitrary",), vmem_limit_bytes=VMEM_LIMIT),
        name="moe_scatter",
    )(dest.reshape(n // tt, 1, tt * TOP_K), xn, zeros)


def _expert_kernel(be_ref, na_ref, xs_ref, w1_ref, b1_ref, w2_ref, b2_ref, o_ref):
    del be_ref
    s = pl.program_id(0)

    @pl.when(s < na_ref[0])
    def _():
        xb = xs_ref[...].astype(BF16)
        hdn = _mm(xb, w1_ref[...]) + b1_ref[...]
        glu = jnp.minimum(hdn[:, :D_FF], SWIGLU_LIMIT)
        lin = jnp.clip(hdn[:, D_FF:], -SWIGLU_LIMIT, SWIGLU_LIMIT)
        act = glu * jax.nn.sigmoid(SWIGLU_ALPHA * glu) * (lin + 1.0)
        o_ref[...] = _mm(act.astype(BF16), w2_ref[...]) + b2_ref[...]

    @pl.when(s >= na_ref[0])
    def _():
        o_ref[...] = jnp.zeros_like(o_ref)


def _experts(xs, block_e, n_active, w):
    rows = xs.shape[0]
    nb = rows // MOE_BLOCK
    grid_spec = pltpu.PrefetchScalarGridSpec(
        num_scalar_prefetch=2,
        grid=(nb,),
        in_specs=[
            pl.BlockSpec((MOE_BLOCK, D_MODEL), lambda s, be, na: (s, 0)),
            pl.BlockSpec((None, D_MODEL, 2 * D_FF), lambda s, be, na: (be[s], 0, 0)),
            pl.BlockSpec((None, 1, 2 * D_FF), lambda s, be, na: (be[s], 0, 0)),
            pl.BlockSpec((None, D_FF, D_MODEL), lambda s, be, na: (be[s], 0, 0)),
            pl.BlockSpec((None, 1, D_MODEL), lambda s, be, na: (be[s], 0, 0)),
        ],
        out_specs=pl.BlockSpec((MOE_BLOCK, D_MODEL), lambda s, be, na: (s, 0)),
    )
    return pl.pallas_call(
        _expert_kernel,
        grid_spec=grid_spec,
        out_shape=jax.ShapeDtypeStruct((rows, D_MODEL), F32),
        compiler_params=pltpu.CompilerParams(
            dimension_semantics=("arbitrary",), vmem_limit_bytes=VMEM_LIMIT),
        name="moe_experts",
    )(block_e, n_active, xs, w["moe_w1"], w["moe_b1"], w["moe_w2"], w["moe_b2"])


def _combine_kernel(dcur_ref, dnxt_ref, h_ref, gate_ref, p_ref, npl_ref, pg_ref, pp_ref, nfin_ref, os_ref,
                    y_ref, buf, sem):
    tt = h_ref.shape[0]
    s = pl.program_id(0)
    ns = pl.num_programs(0)
    slot = s % 2

    def copy(d_ref, tk, sl):
        t = tk // TOP_K
        kx = tk % TOP_K
        return pltpu.make_async_copy(os_ref.at[pl.ds(d_ref[0, tk], 1)], buf.at[sl, kx, pl.ds(t, 1)], sem.at[sl])

    def issue_all(d_ref, sl):
        def body(tk, c):
            copy(d_ref, tk, sl).start()
            return c
        lax.fori_loop(0, tt * TOP_K, body, 0)

    @pl.when(s == 0)
    def _():
        issue_all(dcur_ref, 0)

    @pl.when(s + 1 < ns)
    def _():
        issue_all(dnxt_ref, 1 - slot)

    def drain(tk, c):
        copy(dcur_ref, tk, slot).wait()
        return c
    lax.fori_loop(0, tt * TOP_K, drain, 0)

    gate = gate_ref[...]
    h = h_ref[...]
    for kx in range(TOP_K):
        h = h + buf[slot, kx] * gate[:, kx:kx + 1]
    gt = jax.nn.sigmoid(_mm(_rms(h, npl_ref[...]).astype(BF16), pg_ref[...]))
    h = h + _mm(p_ref[...].astype(BF16), pp_ref[...]) * gt
    y_ref[...] = _rms(h, nfin_ref[...])


def _combine(h, gates, dest, p, os_rows, w):
    n = h.shape[0]
    tt = min(TILE_ROW, n)
    ns = n // tt

    def full(a):
        nd = a.ndim
        return pl.BlockSpec(a.shape, lambda s: (0,) * nd)

    consts = (w["norm_ple"], w["ple_gate"], w["ple_proj"], w["norm_final"])
    d2 = dest.reshape(ns, 1, tt * TOP_K)
    smem = lambda fn: pl.BlockSpec((None, 1, tt * TOP_K), fn, memory_space=pltpu.SMEM)
    return pl.pallas_call(
        _combine_kernel,
        grid=(ns,),
        in_specs=[smem(lambda s: (s, 0, 0)),
                  smem(lambda s: (jnp.minimum(s + 1, ns - 1), 0, 0)),
                  pl.BlockSpec((tt, D_MODEL), lambda s: (s, 0)),
                  pl.BlockSpec((tt, LANES), lambda s: (s, 0)),
                  pl.BlockSpec((tt, PLE_DIM), lambda s: (s, 0))]
        + [full(a) for a in consts]
        + [pl.BlockSpec(memory_space=pl.ANY)],
        out_specs=pl.BlockSpec((tt, D_MODEL), lambda s: (s, 0)),
        out_shape=jax.ShapeDtypeStruct((n, D_MODEL), F32),
        scratch_shapes=[pltpu.VMEM((2, TOP_K, tt, D_MODEL), F32), pltpu.SemaphoreType.DMA((2,))],
        compiler_params=pltpu.CompilerParams(
            dimension_semantics=("arbitrary",), vmem_limit_bytes=VMEM_LIMIT),
        name="moe_combine",
    )(d2, d2, h, gates, p, *consts, os_rows)


def _prep_weights(norm_mix, w_in, a_ln_g, a_ln_b, a_ws, a_bs, b_conv, b_w0, b_w2, b_a0, b_a2, b_g2,
                  b_kk, b_ka, b_rk, b_gn_g, b_gn_b, w_out, norm_ffn, router_w, router_b, moe_w1,
                  moe_b1, moe_w2, moe_b2, norm_ple, ple_proj, ple_gate, norm_final):
    row = lambda a: a.reshape(1, -1).astype(F32)

    def lora_pad(m):
        z = jnp.zeros((2, 2 * LORA, GB), F32)
        z = z.at[0, :LORA].set(m[0]).at[1, LORA:].set(m[1])
        return z.astype(BF16)

    seg = jnp.arange(GB, dtype=I32) // HEAD_DIM
    return {
        "norm_mix": row(norm_mix[0]),
        "w_in": w_in[0].astype(BF16),
        "a_ln_g": row(a_ln_g[0]),
        "a_ln_b": row(a_ln_b[0]),
        "a_ws": a_ws[0].reshape(-1, CHUNK_A).astype(BF16),
        "a_bs": jnp.repeat(a_bs[0].T.astype(F32), HEAD_DIM, axis=1),
        "b_conv": b_conv[0].astype(F32),
        "b_w0": b_w0[0].astype(F32),
        "b_w2": lora_pad(b_w2[0]),
        "b_a0": b_a0[0].astype(F32),
        "b_a2": lora_pad(b_a2[0]),
        "b_g2": b_g2[0].astype(BF16),
        "b_kk": row(b_kk[0]),
        "b_ka": row(b_ka[0]),
        "b_rk": row(b_rk[0]),
        "bd": (seg[:, None] == seg[None, :]).astype(BF16),
        "b_gn_g": row(b_gn_g[0]),
        "b_gn_b": row(b_gn_b[0]),
        "w_out": w_out[0].astype(BF16),
        "norm_ffn": row(norm_ffn[0]),
        "router_w": jnp.pad(router_w[0].astype(F32), ((0, 0), (0, LANES - N_EXPERTS))),
        "router_b": jnp.pad(router_b[0].astype(F32), (0, LANES - N_EXPERTS)).reshape(1, LANES),
        "moe_w1": moe_w1[0].astype(BF16),
        "moe_b1": moe_b1[0].astype(F32).reshape(N_EXPERTS, 1, 2 * D_FF),
        "moe_w2": moe_w2[0].astype(BF16),
        "moe_b2": moe_b2[0].astype(F32).reshape(N_EXPERTS, 1, D_MODEL),
        "norm_ple": row(norm_ple[0]),
        "ple_gate": ple_gate[0].astype(BF16),
        "ple_proj": ple_proj[0].astype(BF16),
        "norm_final": row(norm_final),
    }


def _forward(x, p, w):
    bsz, t, _ = x.shape
    n = bsz * t
    ya, r, v, kn, lw0, lw1, k0, k1, b0, b1, g, bonus = _inproj(x, w)
    yf = _wkv(r, v, kn, k0, b0, lw0, reverse=False)
    yb = _wkv(r, v, kn, k1, b1, lw1, reverse=True)
    h, xn, ids, gates, rank, counts = _post(yf, yb, bonus, g, ya, x, w)

    counts = counts[0, :N_EXPERTS].astype(I32)
    padded = (counts + MOE_BLOCK - 1) // MOE_BLOCK * MOE_BLOCK
    pends = jnp.cumsum(padded)
    pstarts = pends - padded
    n_blocks = -(-(n * TOP_K) // MOE_BLOCK) + N_EXPERTS
    dest = (pstarts[ids[:, :TOP_K]] + rank[:, :TOP_K]).astype(I32)
    block_e = jnp.minimum(
        jnp.searchsorted(pends, jnp.arange(n_blocks, dtype=I32) * MOE_BLOCK, side="right"),
        N_EXPERTS - 1).astype(I32)
    n_active = (pends[-1:] // MOE_BLOCK).astype(I32)

    xs = _scatter_rows(xn, dest, n_blocks * MOE_BLOCK)
    os_rows = _experts(xs, block_e, n_active, w)
    y = _combine(h, gates, dest, p.reshape(n, PLE_DIM), os_rows, w)
    return y.reshape(bsz, t, D_MODEL)


def kernel(x_prompt, x_sample, p_prompt, p_sample, norm_mix, w_in, a_ln_g, a_ln_b, a_ws, a_bs, b_conv, b_w0, b_w2, b_a0, b_a2, b_g2, b_kk, b_ka, b_rk, b_gn_g, b_gn_b, w_out, norm_ffn, router_w, router_b, moe_w1, moe_b1, moe_w2, moe_b2, norm_ple, ple_proj, ple_gate, norm_final):
    assert norm_mix.shape[0] == 1, "single-layer trunk"
    w = _prep_weights(norm_mix, w_in, a_ln_g, a_ln_b, a_ws, a_bs, b_conv, b_w0, b_w2, b_a0, b_a2, b_g2,
                      b_kk, b_ka, b_rk, b_gn_g, b_gn_b, w_out, norm_ffn, router_w, router_b, moe_w1,
                      moe_b1, moe_w2, moe_b2, norm_ple, ple_proj, ple_gate, norm_final)
    y_prompt = _forward(x_prompt, p_prompt[0], w)
    y_sample = _forward(x_sample, p_sample[0], w)
    return (y_prompt, y_sample)
```

```python
import math

import jax
import jax.numpy as jnp
from jax import lax
from jax.experimental import pallas as pl
from jax.experimental.pallas import tpu as pltpu

F32 = jnp.float32
BF16 = jnp.bfloat16
I32 = jnp.int32

D_MODEL = 1024
HEAD_DIM = 64
GA = 512
GB = 512
N_PAIR = GB // 128
CHUNK_A = 128
LORA = 64
LORA_G = 128
B_CONV = 3 * GB + 4 * LORA + LORA_G
N_EXPERTS = 32
TOP_K = 4
D_FF = 1024
PLE_DIM = 256
SWIGLU_ALPHA = 1.702
SWIGLU_LIMIT = 7.0
EPS = 1e-6
GN_EPS = 64e-5
DECAY_SCALE = math.exp(-0.5)

LANES = 128
TILE_IN = 256
WKV_STEP = 256
WKV_CHUNK = 64
TILE_POST = 256
TILE_ROW = 256
MOE_BLOCK = 256
DMA_UNROLL = 8
VMEM_LIMIT = 56 * 1024 * 1024

NT = (((1,), (1,)), ((), ()))
TN = (((0,), (0,)), ((), ()))


def _mm(a, b):
    return jnp.dot(a, b, preferred_element_type=F32)


def _split2(q):
    hi = q.astype(BF16)
    lo = (q - hi.astype(F32)).astype(BF16)
    return hi, lo


def _segsum(q, bd):
    hi, lo = _split2(q)
    return _mm(hi, bd) + _mm(lo, bd)


def _gelu(z):
    return 0.5 * z * (1.0 + lax.erf(z * (1.0 / math.sqrt(2.0))))


def _rms(xv, g):
    ms = jnp.mean(xv * xv, axis=-1, keepdims=True)
    return xv * lax.rsqrt(ms + EPS) * g


def _inproj_kernel(x_ref, xp_ref, xn_ref, nm_ref, win_ref, lng_ref, lnb_ref, ws_ref, bs_ref,
                   conv_ref, w0_ref, w2_ref, a0_ref, a2_ref, g2_ref, kk_ref, ka_ref, rk_ref, bd_ref,
                   ya_ref, r_ref, v_ref, kn_ref, lw0_ref, lw1_ref, k0_ref, k1_ref, b0_ref, b1_ref,
                   g_ref, bonus_ref, zb_scr):
    tt = x_ref.shape[0]
    i = pl.program_id(1)
    last = pl.num_programs(1) - 1
    nm = nm_ref[...]

    xn = _rms(x_ref[...], nm).astype(BF16)
    za = _mm(xn, win_ref[:, :2 * GA])
    zb_scr[8:8 + tt, :] = _mm(xn, win_ref[:, 2 * GA:])
    zp = _mm(_rms(xp_ref[...], nm).astype(BF16), win_ref[:, 2 * GA:])
    zn = _mm(_rms(xn_ref[...], nm).astype(BF16), win_ref[:, 2 * GA:])
    zb_scr[0:8, :] = jnp.where(i > 0, zp, 0.0)
    zb_scr[8 + tt:16 + tt, :] = jnp.where(i < last, zn, 0.0)

    u = _gelu(za[:, :GA])
    v = _gelu(za[:, GA:])
    mu = jnp.mean(v, axis=-1, keepdims=True)
    var = jnp.mean(jnp.square(v - mu), axis=-1, keepdims=True)
    v = ((v - mu) * lax.rsqrt(var + EPS) * lng_ref[...] + lnb_ref[...]).astype(BF16)
    lane_head = lax.broadcasted_iota(I32, (CHUNK_A, GA), 1) // HEAD_DIM
    for c in range(tt // CHUNK_A):
        rows = slice(c * CHUNK_A, (c + 1) * CHUNK_A)
        o = _mm(ws_ref[...], v[rows])
        s = bs_ref[...]
        for h in range(GA // HEAD_DIM):
            s = s + jnp.where(lane_head == h, o[h * CHUNK_A:(h + 1) * CHUNK_A], 0.0)
        ya_ref[rows, :] = (u[rows] * s).astype(ya_ref.dtype)

    conv = conv_ref[...]
    zc = (zb_scr[7:7 + tt, :] * conv[0:1] + zb_scr[8:8 + tt, :] * conv[1:2]
          + zb_scr[9:9 + tt, :] * conv[2:3])
    r = zc[:, :GB]
    k = zc[:, GB:2 * GB]
    vv = zc[:, 2 * GB:3 * GB]
    o0 = 3 * GB
    xw = jnp.tanh(zc[:, o0:o0 + 2 * LORA]).astype(BF16)
    xa = zc[:, o0 + 2 * LORA:o0 + 4 * LORA].astype(BF16)
    xg = jax.nn.sigmoid(zc[:, o0 + 4 * LORA:]).astype(BF16)
    bd = bd_ref[...]
    kk = k * kk_ref[...]
    kk = kk / jnp.maximum(jnp.sqrt(_segsum(kk * kk, bd)), 1e-12)
    ka = ka_ref[...]
    lw_refs = (lw0_ref, lw1_ref)
    k_refs = (k0_ref, k1_ref)
    b_refs = (b0_ref, b1_ref)
    ksum = None
    for d in range(2):
        yw = w0_ref[d:d + 1, :] + _mm(xw, w2_ref[d])
        lw = -DECAY_SCALE * jax.nn.sigmoid(yw)
        a = jax.nn.sigmoid(a0_ref[d:d + 1, :] + _mm(xa, a2_ref[d]))
        kd = k * (1.0 + (a - 1.0) * ka)
        bb = kk * a
        ksum = kd if ksum is None else ksum + kd
        for p in range(N_PAIR):
            ls = slice(p * LANES, (p + 1) * LANES)
            lw_refs[d][p] = lw[:, ls]
            k_refs[d][p] = kd[:, ls].astype(k0_ref.dtype)
            b_refs[d][p] = bb[:, ls].astype(b0_ref.dtype)
    for p in range(N_PAIR):
        ls = slice(p * LANES, (p + 1) * LANES)
        r_ref[p] = r[:, ls].astype(r_ref.dtype)
        v_ref[p] = vv[:, ls].astype(v_ref.dtype)
        kn_ref[p] = kk[:, ls].astype(kn_ref.dtype)
    g_ref[...] = _mm(xg, g2_ref[...])
    bonus_ref[...] = _segsum(r * ksum * rk_ref[...], bd) * vv


def _inproj(x, w):
    bsz, t, _ = x.shape
    tt = min(TILE_IN, t)
    nt = t // tt
    t8 = tt // 8

    def full(a):
        nd = a.ndim
        return pl.BlockSpec(a.shape, lambda b, i: (0,) * nd)

    consts = (w["norm_mix"], w["w_in"], w["a_ln_g"], w["a_ln_b"], w["a_ws"], w["a_bs"], w["b_conv"],
              w["b_w0"], w["b_w2"], w["b_a0"], w["b_a2"], w["b_g2"], w["b_kk"], w["b_ka"], w["b_rk"],
              w["bd"])
    in_specs = [
        pl.BlockSpec((None, tt, D_MODEL), lambda b, i: (b, i, 0)),
        pl.BlockSpec((None, 8, D_MODEL), lambda b, i: (b, jnp.maximum(i * t8 - 1, 0), 0)),
        pl.BlockSpec((None, 8, D_MODEL), lambda b, i: (b, jnp.minimum((i + 1) * t8, t // 8 - 1), 0)),
    ] + [full(a) for a in consts]
    pair = lambda dt: jax.ShapeDtypeStruct((bsz, N_PAIR, t, LANES), dt)
    flat = lambda dt: jax.ShapeDtypeStruct((bsz, t, GB), dt)
    pair_spec = pl.BlockSpec((None, N_PAIR, tt, LANES), lambda b, i: (b, 0, i, 0))
    flat_spec = pl.BlockSpec((None, tt, GB), lambda b, i: (b, i, 0))
    out_shape = (flat(BF16),
                 pair(BF16), pair(BF16), pair(BF16),
                 pair(F32), pair(F32),
                 pair(BF16), pair(BF16), pair(BF16), pair(BF16),
                 flat(F32), flat(F32))
    out_specs = (flat_spec,) + (pair_spec,) * 9 + (flat_spec, flat_spec)
    return pl.pallas_call(
        _inproj_kernel,
        grid=(bsz, nt),
        in_specs=in_specs,
        out_specs=out_specs,
        out_shape=out_shape,
        scratch_shapes=[pltpu.VMEM((tt + 16, B_CONV), F32)],
        compiler_params=pltpu.CompilerParams(
            dimension_semantics=("parallel", "parallel"), vmem_limit_bytes=VMEM_LIMIT),
        name="inproj",
    )(x, x, x, *consts)


def _wkv_prep(p, refs, reverse, shared):
    r_ref, v_ref, kn_ref, k_ref, b_ref, lw_ref = refs
    row, col, m64, blk = shared
    strict = m64 & ((col > row) if reverse else (col < row))
    incl = m64 & ((col >= row) if reverse else (col <= row))

    lw = lw_ref[p]
    l1 = lw.astype(BF16)
    r1 = lw - l1.astype(F32)
    l2 = r1.astype(BF16)
    l3 = (r1 - l2.astype(F32)).astype(BF16)
    tri = jnp.where(incl, 1.0, 0.0).astype(BF16)
    cum = _mm(tri, l1) + _mm(tri, l2) + _mm(tri, l3)
    tot = _mm(blk, l1) + _mm(blk, l2) + _mm(blk, l3)
    rr = r_ref[p].astype(F32)
    kn = kn_ref[p].astype(F32)
    kd = k_ref[p].astype(F32)
    bb = b_ref[p].astype(F32)
    winv = jnp.exp(-cum)
    wd = jnp.exp(tot - cum)
    return dict(
        strict=strict, incl=incl, wtot=jnp.exp(tot),
        rt=(rr * jnp.exp(cum)).astype(BF16), at=(-kn * jnp.exp(cum - lw)).astype(BF16),
        bt=(bb * winv).astype(BF16), kt=(kd * winv).astype(BF16),
        bh=(bb * wd).astype(BF16), kh=(kd * wd).astype(BF16), vb=v_ref[p].astype(BF16))


def _wkv_chains(chains, s_scr, masks):
    m8, m16, m32, m64, eye, eye_h = masks
    n = WKV_STEP
    nc = n // WKV_CHUNK
    dot_nt = lambda a, b: lax.dot_general(a, b, NT, preferred_element_type=F32)
    dot_tn = lambda a, b: lax.dot_general(a, b, TN, preferred_element_type=F32)
    each = lambda fn: [fn(c) for c in chains]

    for c in chains:
        sl = slice(c["hh"] * HEAD_DIM, (c["hh"] + 1) * HEAD_DIM)
        pr = c["prep"]
        c.update(sl=sl, a=pr["at"][:, sl], r=pr["rt"][:, sl], b=pr["bt"][:, sl], k=pr["kt"][:, sl],
                 v=pr["vb"][:, sl], strict=pr["strict"], incl=pr["incl"])
    lab = each(lambda c: jnp.where(c["strict"], dot_nt(c["a"], c["b"]), 0.0))
    lak = each(lambda c: jnp.where(c["strict"], dot_nt(c["a"], c["k"]), 0.0).astype(BF16))
    mrb = each(lambda c: jnp.where(c["incl"], dot_nt(c["r"], c["b"]), 0.0).astype(BF16))
    mrk = each(lambda c: jnp.where(c["incl"], dot_nt(c["r"], c["k"]), 0.0).astype(BF16))

    l8 = [jnp.where(m8, l, 0.0) for l in lab]
    p1 = [l.astype(BF16) for l in l8]
    p2 = [_mm(q, q).astype(BF16) for q in p1]
    x = [eye + l for l in l8]
    p4 = [_mm(q, q).astype(BF16) for q in p2]
    x = [xi + _mm(xi.astype(BF16), q) for xi, q in zip(x, p2)]
    x = [xi + _mm(xi.astype(BF16), q) for xi, q in zip(x, p4)]
    prev = m8
    for cur in (m16, m32, m64):
        lvl = cur & jnp.logical_not(prev)
        xb = [xi.astype(BF16) for xi in x]
        t = [_mm(jnp.where(lvl, l, 0.0).astype(BF16), xi).astype(BF16) for l, xi in zip(lab, xb)]
        x = [xi + _mm(xbi, ti) for xi, xbi, ti in zip(x, xb, t)]
        prev = cur
    tb = [xi.astype(BF16) for xi in x]

    x1 = [_mm(l, c["v"]).astype(BF16) for l, c in zip(lak, chains)]
    wu = [_mm(t, c["a"]).astype(BF16) for t, c in zip(tb, chains)]
    uv = [_mm(t, xi).astype(BF16) for t, xi in zip(tb, x1)]
    q = [(c["r"].astype(F32) + _mm(m, w)).astype(BF16) for c, m, w in zip(chains, mrb, wu)]
    yl = [_mm(m, u) + _mm(mk, c["v"]) for m, u, mk, c in zip(mrb, uv, mrk, chains)]

    g, h = [], []
    for i, c in enumerate(chains):
        gi, hi = [], []
        for ck in range(nc):
            rs = slice(ck * WKV_CHUNK, (ck + 1) * WKV_CHUNK)
            b_c = c["prep"]["bh"][rs, c["sl"]]
            k_c = c["prep"]["kh"][rs, c["sl"]]
            wrow = c["prep"]["wtot"][ck * WKV_CHUNK:ck * WKV_CHUNK + 1, c["sl"]]
            gi.append((dot_tn(b_c, wu[i][rs])
                       + jnp.where(eye_h, jnp.broadcast_to(wrow, (HEAD_DIM, HEAD_DIM)), 0.0)).astype(BF16))
            hi.append(dot_tn(b_c, uv[i][rs]) + dot_tn(k_c, c["v"][rs]))
        g.append(gi)
        h.append(hi)

    s = [s_scr[c["d"], 2 * c["p"] + c["hh"]] for c in chains]
    for step in range(nc):
        for i, c in enumerate(chains):
            ck = nc - 1 - step if c["reverse"] else step
            rs = slice(ck * WKV_CHUNK, (ck + 1) * WKV_CHUNK)
            sb = s[i].astype(BF16)
            c["y_ref"][c["p"], rs, c["sl"]] = _mm(q[i][rs], sb) + yl[i][rs]
            s[i] = _mm(g[i][ck], sb) + h[i][ck]
    for i, c in enumerate(chains):
        s_scr[c["d"], 2 * c["p"] + c["hh"]] = s[i]


def _wkv_kernel(rf_ref, vf_ref, nf_ref, kf_ref, bf_ref, lf_ref,
                rb_ref, vb_ref, nb_ref, kb_ref, bb_ref, lb_ref, yf_ref, yb_ref, s_scr):
    n = WKV_STEP

    @pl.when(pl.program_id(1) == 0)
    def _():
        s_scr[...] = jnp.zeros_like(s_scr)

    def pair(p, carry):
        row = lax.broadcasted_iota(I32, (n, n), 0)
        col = lax.broadcasted_iota(I32, (n, n), 1)
        same = lambda log2: (row >> log2) == (col >> log2)
        m64 = same(6)
        er = lax.broadcasted_iota(I32, (HEAD_DIM, HEAD_DIM), 0)
        ec = lax.broadcasted_iota(I32, (HEAD_DIM, HEAD_DIM), 1)
        shared = (row, col, m64, jnp.where(m64, 1.0, 0.0).astype(BF16))
        masks = (same(3), same(4), same(5), m64, jnp.where(row == col, 1.0, 0.0).astype(F32), er == ec)
        fwd = _wkv_prep(p, (rf_ref, vf_ref, nf_ref, kf_ref, bf_ref, lf_ref), False, shared)
        bwd = _wkv_prep(p, (rb_ref, vb_ref, nb_ref, kb_ref, bb_ref, lb_ref), True, shared)
        chains = [dict(prep=pr, hh=hh, d=d, p=p, reverse=rev, y_ref=y_ref)
                  for pr, d, rev, y_ref in ((fwd, 0, False, yf_ref), (bwd, 1, True, yb_ref))
                  for hh in range(2)]
        _wkv_chains(chains, s_scr, masks)
        return carry

    lax.fori_loop(0, N_PAIR, pair, 0)


def _wkv(r, v, kn, k0, b0, lw0, k1, b1, lw1):
    bsz, _, t, _ = r.shape
    assert t % WKV_STEP == 0
    nj = t // WKV_STEP
    fwd = pl.BlockSpec((None, N_PAIR, WKV_STEP, LANES), lambda bi, j: (bi, 0, j, 0))
    bwd = pl.BlockSpec((None, N_PAIR, WKV_STEP, LANES), lambda bi, j: (bi, 0, nj - 1 - j, 0))
    out = jax.ShapeDtypeStruct((bsz, N_PAIR, t, LANES), F32)
    return pl.pallas_call(
        _wkv_kernel,
        grid=(bsz, nj),
        in_specs=[fwd] * 6 + [bwd] * 6,
        out_specs=(fwd, bwd),
        out_shape=(out, out),
        scratch_shapes=[pltpu.VMEM((2, 2 * N_PAIR, HEAD_DIM, HEAD_DIM), F32)],
        compiler_params=pltpu.CompilerParams(
            dimension_semantics=("parallel", "arbitrary"), vmem_limit_bytes=VMEM_LIMIT),
        name="wkv",
    )(r, v, kn, k0, b0, lw0, r, v, kn, k1, b1, lw1)


def _post_kernel(yf_ref, yb_ref, bonus_ref, g_ref, ya_ref, x_ref, gng_ref, gnb_ref, bd_ref, wout_ref,
                 nf_ref, rw_ref, rb_ref,
                 h_ref, xn_ref, ids_ref, gate_ref, rank_ref, cnt_ref, run_scr):
    tt = x_ref.shape[0]
    step = pl.program_id(0)

    @pl.when(step == 0)
    def _():
        run_scr[...] = jnp.zeros_like(run_scr)

    y = jnp.concatenate([yf_ref[p] + yb_ref[p] for p in range(N_PAIR)], axis=1)
    bd = bd_ref[...]
    inv = 1.0 / HEAD_DIM
    mu = _segsum(y, bd) * inv
    yc = y - mu
    var = _segsum(yc * yc, bd) * inv
    yn = yc * lax.rsqrt(var + GN_EPS) * gng_ref[...] + gnb_ref[...] + bonus_ref[...]
    ybm = (yn * g_ref[...]).astype(BF16)
    h = x_ref[...] + _mm(ya_ref[...], wout_ref[:GA, :]) + _mm(ybm, wout_ref[GA:, :])
    h_ref[...] = h
    xn = _rms(h, nf_ref[...])
    xn_ref[...] = xn

    lane = lax.broadcasted_iota(I32, (tt, LANES), 1)
    logits = jnp.dot(xn, rw_ref[...], preferred_element_type=F32, precision=lax.Precision.HIGHEST)
    logits = jnp.where(lane < N_EXPERTS, logits + rb_ref[...], -jnp.inf)
    vals, ids, sel = [], [], jnp.zeros((tt, LANES), F32)
    cur = logits
    for _ in range(TOP_K):
        m = jnp.max(cur, axis=-1, keepdims=True)
        idx = jnp.min(jnp.where(cur == m, lane, LANES), axis=-1, keepdims=True)
        hit = lane == idx
        vals.append(m)
        ids.append(idx)
        sel = sel + jnp.where(hit, 1.0, 0.0)
        cur = jnp.where(hit, -jnp.inf, cur)
    es = [jnp.exp(vv - vals[0]) for vv in vals]
    den = es[0] + es[1] + es[2] + es[3]

    rt = lax.broadcasted_iota(I32, (tt, tt), 0)
    ct = lax.broadcasted_iota(I32, (tt, tt), 1)
    before = jnp.where(ct < rt, 1.0, 0.0).astype(BF16)
    cnt = _mm(before, sel.astype(BF16)) + run_scr[...]
    ids_o = jnp.zeros((tt, LANES), I32)
    gate_o = jnp.zeros((tt, LANES), F32)
    rank_o = jnp.zeros((tt, LANES), F32)
    for kx in range(TOP_K):
        rk = jnp.sum(jnp.where(lane == ids[kx], cnt, 0.0), axis=-1, keepdims=True)
        ids_o = jnp.where(lane == kx, ids[kx], ids_o)
        gate_o = jnp.where(lane == kx, es[kx] / den, gate_o)
        rank_o = jnp.where(lane == kx, rk, rank_o)
    ids_ref[...] = ids_o
    gate_ref[...] = gate_o
    rank_ref[...] = rank_o.astype(I32)
    run = run_scr[...] + jnp.sum(sel, axis=0, keepdims=True)
    run_scr[...] = run
    cnt_ref[...] = run


def _post(yf, yb, bonus, g, ya, x, w):
    bsz, t, _ = x.shape
    n = bsz * t
    tt = min(TILE_POST, t)
    nt = t // tt

    def full(a):
        nd = a.ndim
        return pl.BlockSpec(a.shape, lambda s: (0,) * nd)

    consts = (w["b_gn_g"], w["b_gn_b"], w["bd"], w["w_out"], w["norm_ffn"], w["router_w"], w["router_b"])
    pair_spec = pl.BlockSpec((None, N_PAIR, tt, LANES), lambda s: (s // nt, 0, s % nt, 0))
    tok3 = lambda width: pl.BlockSpec((None, tt, width), lambda s: (s // nt, s % nt, 0))
    tok2 = lambda width: pl.BlockSpec((tt, width), lambda s: (s, 0))
    in_specs = [pair_spec, pair_spec, tok3(GB), tok3(GB), tok3(GA), tok3(D_MODEL)] + [full(a) for a in consts]
    out_shape = (jax.ShapeDtypeStruct((n, D_MODEL), F32),
                 jax.ShapeDtypeStruct((n, D_MODEL), F32),
                 jax.ShapeDtypeStruct((n, LANES), I32),
                 jax.ShapeDtypeStruct((n, LANES), F32),
                 jax.ShapeDtypeStruct((n, LANES), I32),
                 jax.ShapeDtypeStruct((1, LANES), F32))
    out_specs = (tok2(D_MODEL), tok2(D_MODEL), tok2(LANES), tok2(LANES), tok2(LANES),
                 pl.BlockSpec((1, LANES), lambda s: (0, 0)))
    return pl.pallas_call(
        _post_kernel,
        grid=(bsz * nt,),
        in_specs=in_specs,
        out_specs=out_specs,
        out_shape=out_shape,
        scratch_shapes=[pltpu.VMEM((1, LANES), F32)],
        compiler_params=pltpu.CompilerParams(
            dimension_semantics=("arbitrary",), vmem_limit_bytes=VMEM_LIMIT),
        name="post_router",
    )(yf, yb, bonus, g, ya, x, *consts)


def _scatter_kernel(dest_ref, xn_ref, zeros_ref, xs_ref, sem):
    del zeros_ref
    tt = xn_ref.shape[0]

    def issue(t, c):
        for kx in range(TOP_K):
            pltpu.make_async_copy(xn_ref.at[pl.ds(t, 1)],
                                  xs_ref.at[pl.ds(dest_ref[0, t * TOP_K + kx], 1)], sem).start()
        return c

    lax.fori_loop(0, tt, issue, 0, unroll=DMA_UNROLL)
    for _ in range(TOP_K):
        pltpu.make_async_copy(xn_ref, xs_ref.at[pl.ds(0, tt)], sem).wait()


def _scatter_rows(xn, dest, rows):
    n = xn.shape[0]
    tt = min(TILE_ROW, n)
    zeros = jnp.zeros((rows, D_MODEL), F32)
    return pl.pallas_call(
        _scatter_kernel,
        grid=(n // tt,),
        in_specs=[pl.BlockSpec((None, 1, tt * TOP_K), lambda s: (s, 0, 0), memory_space=pltpu.SMEM),
                  pl.BlockSpec((tt, D_MODEL), lambda s: (s, 0)),
                  pl.BlockSpec(memory_space=pl.ANY)],
        out_specs=pl.BlockSpec(memory_space=pl.ANY),
        out_shape=jax.ShapeDtypeStruct((rows, D_MODEL), F32),
        scratch_shapes=[pltpu.SemaphoreType.DMA(())],
        input_output_aliases={2: 0},
        compiler_params=pltpu.CompilerParams(
            dimension_semantics=("arbitrary",), vmem_limit_bytes=VMEM_LIMIT),
        name="moe_scatter",
    )(dest.reshape(n // tt, 1, tt * TOP_K), xn, zeros)


def _expert_kernel(be_ref, na_ref, xs_ref, w1_ref, b1_ref, w2_ref, b2_ref, o_ref):
    del be_ref
    s = pl.program_id(0)

    @pl.when(s < na_ref[0])
    def _():
        xb = xs_ref[...].astype(BF16)
        hdn = _mm(xb, w1_ref[...]) + b1_ref[...]
        glu = jnp.minimum(hdn[:, :D_FF], SWIGLU_LIMIT)
        lin = jnp.clip(hdn[:, D_FF:], -SWIGLU_LIMIT, SWIGLU_LIMIT)
        act = glu * jax.nn.sigmoid(SWIGLU_ALPHA * glu) * (lin + 1.0)
        o_ref[...] = _mm(act.astype(BF16), w2_ref[...]) + b2_ref[...]

    @pl.when(s >= na_ref[0])
    def _():
        o_ref[...] = jnp.zeros_like(o_ref)


def _experts(xs, block_e, n_active, w):
    rows = xs.shape[0]
    nb = rows // MOE_BLOCK
    grid_spec = pltpu.PrefetchScalarGridSpec(
        num_scalar_prefetch=2,
        grid=(nb,),
        in_specs=[
            pl.BlockSpec((MOE_BLOCK, D_MODEL), lambda s, be, na: (s, 0)),
            pl.BlockSpec((None, D_MODEL, 2 * D_FF), lambda s, be, na: (be[s], 0, 0)),
            pl.BlockSpec((None, 1, 2 * D_FF), lambda s, be, na: (be[s], 0, 0)),
            pl.BlockSpec((None, D_FF, D_MODEL), lambda s, be, na: (be[s], 0, 0)),
            pl.BlockSpec((None, 1, D_MODEL), lambda s, be, na: (be[s], 0, 0)),
        ],
        out_specs=pl.BlockSpec((MOE_BLOCK, D_MODEL), lambda s, be, na: (s, 0)),
    )
    return pl.pallas_call(
        _expert_kernel,
        grid_spec=grid_spec,
        out_shape=jax.ShapeDtypeStruct((rows, D_MODEL), F32),
        compiler_params=pltpu.CompilerParams(
            dimension_semantics=("arbitrary",), vmem_limit_bytes=VMEM_LIMIT),
        name="moe_experts",
    )(block_e, n_active, xs, w["moe_w1"], w["moe_b1"], w["moe_w2"], w["moe_b2"])


def _combine_kernel(dcur_ref, dnxt_ref, h_ref, gate_ref, p_ref, npl_ref, pg_ref, pp_ref, nfin_ref, os_ref,
                    y_ref, buf, sem):
    tt = h_ref.shape[0]
    s = pl.program_id(0)
    ns = pl.num_programs(0)
    slot = s % 2

    def issue_all(d_ref, sl):
        def body(t, c):
            for kx in range(TOP_K):
                pltpu.make_async_copy(os_ref.at[pl.ds(d_ref[0, t * TOP_K + kx], 1)],
                                      buf.at[sl, kx, pl.ds(t, 1)], sem.at[sl]).start()
            return c
        lax.fori_loop(0, tt, body, 0, unroll=DMA_UNROLL)

    @pl.when(s == 0)
    def _():
        issue_all(dcur_ref, 0)

    @pl.when(s + 1 < ns)
    def _():
        issue_all(dnxt_ref, 1 - slot)

    for kx in range(TOP_K):
        pltpu.make_async_copy(os_ref.at[pl.ds(0, tt)], buf.at[slot, kx], sem.at[slot]).wait()

    gate = gate_ref[...]
    h = h_ref[...]
    for kx in range(TOP_K):
        h = h + buf[slot, kx] * gate[:, kx:kx + 1]
    gt = jax.nn.sigmoid(_mm(_rms(h, npl_ref[...]).astype(BF16), pg_ref[...]))
    h = h + _mm(p_ref[...].astype(BF16), pp_ref[...]) * gt
    y_ref[...] = _rms(h, nfin_ref[...])


def _combine(h, gates, dest, p, os_rows, w):
    n = h.shape[0]
    tt = min(TILE_ROW, n)
    ns = n // tt

    def full(a):
        nd = a.ndim
        return pl.BlockSpec(a.shape, lambda s: (0,) * nd)

    consts = (w["norm_ple"], w["ple_gate"], w["ple_proj"], w["norm_final"])
    d2 = dest.reshape(ns, 1, tt * TOP_K)
    smem = lambda fn: pl.BlockSpec((None, 1, tt * TOP_K), fn, memory_space=pltpu.SMEM)
    return pl.pallas_call(
        _combine_kernel,
        grid=(ns,),
        in_specs=[smem(lambda s: (s, 0, 0)),
                  smem(lambda s: (jnp.minimum(s + 1, ns - 1), 0, 0)),
                  pl.BlockSpec((tt, D_MODEL), lambda s: (s, 0)),
                  pl.BlockSpec((tt, LANES), lambda s: (s, 0)),
                  pl.BlockSpec((tt, PLE_DIM), lambda s: (s, 0))]
        + [full(a) for a in consts]
        + [pl.BlockSpec(memory_space=pl.ANY)],
        out_specs=pl.BlockSpec((tt, D_MODEL), lambda s: (s, 0)),
        out_shape=jax.ShapeDtypeStruct((n, D_MODEL), F32),
        scratch_shapes=[pltpu.VMEM((2, TOP_K, tt, D_MODEL), F32), pltpu.SemaphoreType.DMA((2,))],
        compiler_params=pltpu.CompilerParams(
            dimension_semantics=("arbitrary",), vmem_limit_bytes=VMEM_LIMIT),
        name="moe_combine",
    )(d2, d2, h, gates, p, *consts, os_rows)


def _prep_weights(norm_mix, w_in, a_ln_g, a_ln_b, a_ws, a_bs, b_conv, b_w0, b_w2, b_a0, b_a2, b_g2,
                  b_kk, b_ka, b_rk, b_gn_g, b_gn_b, w_out, norm_ffn, router_w, router_b, moe_w1,
                  moe_b1, moe_w2, moe_b2, norm_ple, ple_proj, ple_gate, norm_final):
    row = lambda a: a.reshape(1, -1).astype(F32)

    def lora_pad(m):
        z = jnp.zeros((2, 2 * LORA, GB), F32)
        z = z.at[0, :LORA].set(m[0]).at[1, LORA:].set(m[1])
        return z.astype(BF16)

    seg = jnp.arange(GB, dtype=I32) // HEAD_DIM
    return {
        "norm_mix": row(norm_mix[0]),
        "w_in": w_in[0].astype(BF16),
        "a_ln_g": row(a_ln_g[0]),
        "a_ln_b": row(a_ln_b[0]),
        "a_ws": a_ws[0].reshape(-1, CHUNK_A).astype(BF16),
        "a_bs": jnp.repeat(a_bs[0].T.astype(F32), HEAD_DIM, axis=1),
        "b_conv": b_conv[0].astype(F32),
        "b_w0": b_w0[0].astype(F32),
        "b_w2": lora_pad(b_w2[0]),
        "b_a0": b_a0[0].astype(F32),
        "b_a2": lora_pad(b_a2[0]),
        "b_g2": b_g2[0].astype(BF16),
        "b_kk": row(b_kk[0]),
        "b_ka": row(b_ka[0]),
        "b_rk": row(b_rk[0]),
        "bd": (seg[:, None] == seg[None, :]).astype(BF16),
        "b_gn_g": row(b_gn_g[0]),
        "b_gn_b": row(b_gn_b[0]),
        "w_out": w_out[0].astype(BF16),
        "norm_ffn": row(norm_ffn[0]),
        "router_w": jnp.pad(router_w[0].astype(F32), ((0, 0), (0, LANES - N_EXPERTS))),
        "router_b": jnp.pad(router_b[0].astype(F32), (0, LANES - N_EXPERTS)).reshape(1, LANES),
        "moe_w1": moe_w1[0].astype(BF16),
        "moe_b1": moe_b1[0].astype(F32).reshape(N_EXPERTS, 1, 2 * D_FF),
        "moe_w2": moe_w2[0].astype(BF16),
        "moe_b2": moe_b2[0].astype(F32).reshape(N_EXPERTS, 1, D_MODEL),
        "norm_ple": row(norm_ple[0]),
        "ple_gate": ple_gate[0].astype(BF16),
        "ple_proj": ple_proj[0].astype(BF16),
        "norm_final": row(norm_final),
    }


def _forward(x, p, w):
    bsz, t, _ = x.shape
    n = bsz * t
    ya, r, v, kn, lw0, lw1, k0, k1, b0, b1, g, bonus = _inproj(x, w)
    yf, yb = _wkv(r, v, kn, k0, b0, lw0, k1, b1, lw1)
    h, xn, ids, gates, rank, counts = _post(yf, yb, bonus, g, ya, x, w)

    counts = counts[0, :N_EXPERTS].astype(I32)
    padded = (counts + MOE_BLOCK - 1) // MOE_BLOCK * MOE_BLOCK
    pends = jnp.cumsum(padded)
    pstarts = pends - padded
    n_blocks = -(-(n * TOP_K) // MOE_BLOCK) + N_EXPERTS
    dest = (pstarts[ids[:, :TOP_K]] + rank[:, :TOP_K]).astype(I32)
    block_e = jnp.minimum(
        jnp.searchsorted(pends, jnp.arange(n_blocks, dtype=I32) * MOE_BLOCK, side="right"),
        N_EXPERTS - 1).astype(I32)
    n_active = (pends[-1:] // MOE_BLOCK).astype(I32)

    xs = _scatter_rows(xn, dest, n_blocks * MOE_BLOCK)
    os_rows = _experts(xs, block_e, n_active, w)
    y = _combine(h, gates, dest, p.reshape(n, PLE_DIM), os_rows, w)
    return y.reshape(bsz, t, D_MODEL)


def kernel(x_prompt, x_sample, p_prompt, p_sample, norm_mix, w_in, a_ln_g, a_ln_b, a_ws, a_bs, b_conv, b_w0, b_w2, b_a0, b_a2, b_g2, b_kk, b_ka, b_rk, b_gn_g, b_gn_b, w_out, norm_ffn, router_w, router_b, moe_w1, moe_b1, moe_w2, moe_b2, norm_ple, ple_proj, ple_gate, norm_final):
    assert norm_mix.shape[0] == 1, "single-layer trunk"
    w = _prep_weights(norm_mix, w_in, a_ln_g, a_ln_b, a_ws, a_bs, b_conv, b_w0, b_w2, b_a0, b_a2, b_g2,
                      b_kk, b_ka, b_rk, b_gn_g, b_gn_b, w_out, norm_ffn, router_w, router_b, moe_w1,
                      moe_b1, moe_w2, moe_b2, norm_ple, ple_proj, ple_gate, norm_final)
    y_prompt = _forward(x_prompt, p_prompt[0], w)
    y_sample = _forward(x_sample, p_sample[0], w)
    return (y_prompt, y_sample)
```

```python
import math

import jax
import jax.numpy as jnp
from jax import lax
from jax.experimental import pallas as pl
from jax.experimental.pallas import tpu as pltpu

F32 = jnp.float32
BF16 = jnp.bfloat16
I32 = jnp.int32

D_MODEL = 1024
HEAD_DIM = 64
GA = 512
GB = 512
N_PAIR = GB // 128
CHUNK_A = 128
LORA = 64
LORA_G = 128
B_CONV = 3 * GB + 4 * LORA + LORA_G
N_EXPERTS = 32
TOP_K = 4
D_FF = 1024
PLE_DIM = 256
SWIGLU_ALPHA = 1.702
SWIGLU_LIMIT = 7.0
EPS = 1e-6
GN_EPS = 64e-5
DECAY_SCALE = math.exp(-0.5)

LANES = 128
TILE_IN = 256
WKV_STEP = 256
WKV_CHUNK = 64
TILE_POST = 256
TILE_ROW = 256
MOE_BLOCK = 512
DMA_UNROLL = 8
VMEM_LIMIT = 56 * 1024 * 1024

NT = (((1,), (1,)), ((), ()))
TN = (((0,), (0,)), ((), ()))


def _mm(a, b):
    return jnp.dot(a, b, preferred_element_type=F32)


def _split2(q):
    hi = q.astype(BF16)
    lo = (q - hi.astype(F32)).astype(BF16)
    return hi, lo


def _segsum(q, bd):
    hi, lo = _split2(q)
    return _mm(hi, bd) + _mm(lo, bd)


def _gelu(z):
    return 0.5 * z * (1.0 + lax.erf(z * (1.0 / math.sqrt(2.0))))


def _rms(xv, g):
    ms = jnp.mean(xv * xv, axis=-1, keepdims=True)
    return xv * lax.rsqrt(ms + EPS) * g


def _inproj_kernel(x_ref, xp_ref, xn_ref, nm_ref, win_ref, lng_ref, lnb_ref, ws_ref, bs_ref,
                   conv_ref, w0_ref, w2_ref, a0_ref, a2_ref, g2_ref, kk_ref, ka_ref, rk_ref, bd_ref,
                   ya_ref, r_ref, v_ref, kn_ref, lw0_ref, lw1_ref, k0_ref, k1_ref, b0_ref, b1_ref,
                   g_ref, bonus_ref):
    tt = x_ref.shape[0]
    i = pl.program_id(1)
    last = pl.num_programs(1) - 1
    nm = nm_ref[...]

    xe = jnp.concatenate([x_ref[...], xp_ref[...], xn_ref[...]], axis=0)
    xe = _rms(xe, nm).astype(BF16)
    za = _mm(xe[:tt], win_ref[:, :2 * GA])
    ze = _mm(xe, win_ref[:, 2 * GA:])
    zb = ze[:tt]
    row_prev = jnp.where(i > 0, ze[tt + 7:tt + 8], 0.0)
    row_next = jnp.where(i < last, ze[tt + 8:tt + 9], 0.0)

    u = _gelu(za[:, :GA])
    v = _gelu(za[:, GA:])
    mu = jnp.mean(v, axis=-1, keepdims=True)
    var = jnp.mean(jnp.square(v - mu), axis=-1, keepdims=True)
    v = ((v - mu) * lax.rsqrt(var + EPS) * lng_ref[...] + lnb_ref[...]).astype(BF16)
    lane_head = lax.broadcasted_iota(I32, (CHUNK_A, GA), 1) // HEAD_DIM
    for c in range(tt // CHUNK_A):
        rows = slice(c * CHUNK_A, (c + 1) * CHUNK_A)
        o = _mm(ws_ref[...], v[rows])
        s = bs_ref[...]
        for h in range(GA // HEAD_DIM):
            s = s + jnp.where(lane_head == h, o[h * CHUNK_A:(h + 1) * CHUNK_A], 0.0)
        ya_ref[rows, :] = (u[rows] * s).astype(ya_ref.dtype)

    conv = conv_ref[...]
    trow = lax.broadcasted_iota(I32, (tt, 1), 0)
    z_prev = jnp.where(trow == 0, row_prev, pltpu.roll(zb, 1, axis=0))
    z_next = jnp.where(trow == tt - 1, row_next, pltpu.roll(zb, tt - 1, axis=0))
    zc = z_prev * conv[0:1] + zb * conv[1:2] + z_next * conv[2:3]
    r = zc[:, :GB]
    k = zc[:, GB:2 * GB]
    vv = zc[:, 2 * GB:3 * GB]
    o0 = 3 * GB
    xw = jnp.tanh(zc[:, o0:o0 + 2 * LORA]).astype(BF16)
    xa = zc[:, o0 + 2 * LORA:o0 + 4 * LORA].astype(BF16)
    xg = jax.nn.sigmoid(zc[:, o0 + 4 * LORA:]).astype(BF16)
    bd = bd_ref[...]
    kk = k * kk_ref[...]
    kk = kk / jnp.maximum(jnp.sqrt(_segsum(kk * kk, bd)), 1e-12)
    ka = ka_ref[...]
    lw_refs = (lw0_ref, lw1_ref)
    k_refs = (k0_ref, k1_ref)
    b_refs = (b0_ref, b1_ref)
    ksum = None
    for d in range(2):
        yw = w0_ref[d:d + 1, :] + _mm(xw, w2_ref[d])
        lw = -DECAY_SCALE * jax.nn.sigmoid(yw)
        a = jax.nn.sigmoid(a0_ref[d:d + 1, :] + _mm(xa, a2_ref[d]))
        kd = k * (1.0 + (a - 1.0) * ka)
        bb = kk * a
        ksum = kd if ksum is None else ksum + kd
        for p in range(N_PAIR):
            ls = slice(p * LANES, (p + 1) * LANES)
            lw_refs[d][p] = lw[:, ls]
            k_refs[d][p] = kd[:, ls].astype(k0_ref.dtype)
            b_refs[d][p] = bb[:, ls].astype(b0_ref.dtype)
    for p in range(N_PAIR):
        ls = slice(p * LANES, (p + 1) * LANES)
        r_ref[p] = r[:, ls].astype(r_ref.dtype)
        v_ref[p] = vv[:, ls].astype(v_ref.dtype)
        kn_ref[p] = kk[:, ls].astype(kn_ref.dtype)
    g_ref[...] = _mm(xg, g2_ref[...])
    bonus_ref[...] = _segsum(r * ksum * rk_ref[...], bd) * vv


def _inproj(x, w):
    bsz, t, _ = x.shape
    tt = min(TILE_IN, t)
    nt = t // tt
    t8 = tt // 8

    def full(a):
        nd = a.ndim
        return pl.BlockSpec(a.shape, lambda b, i: (0,) * nd)

    consts = (w["norm_mix"], w["w_in"], w["a_ln_g"], w["a_ln_b"], w["a_ws"], w["a_bs"], w["b_conv"],
              w["b_w0"], w["b_w2"], w["b_a0"], w["b_a2"], w["b_g2"], w["b_kk"], w["b_ka"], w["b_rk"],
              w["bd"])
    in_specs = [
        pl.BlockSpec((None, tt, D_MODEL), lambda b, i: (b, i, 0)),
        pl.BlockSpec((None, 8, D_MODEL), lambda b, i: (b, jnp.maximum(i * t8 - 1, 0), 0)),
        pl.BlockSpec((None, 8, D_MODEL), lambda b, i: (b, jnp.minimum((i + 1) * t8, t // 8 - 1), 0)),
    ] + [full(a) for a in consts]
    pair = lambda dt: jax.ShapeDtypeStruct((bsz, N_PAIR, t, LANES), dt)
    flat = lambda dt: jax.ShapeDtypeStruct((bsz, t, GB), dt)
    pair_spec = pl.BlockSpec((None, N_PAIR, tt, LANES), lambda b, i: (b, 0, i, 0))
    flat_spec = pl.BlockSpec((None, tt, GB), lambda b, i: (b, i, 0))
    out_shape = (flat(BF16),
                 pair(BF16), pair(BF16), pair(BF16),
                 pair(F32), pair(F32),
                 pair(BF16), pair(BF16), pair(BF16), pair(BF16),
                 flat(F32), flat(F32))
    out_specs = (flat_spec,) + (pair_spec,) * 9 + (flat_spec, flat_spec)
    return pl.pallas_call(
        _inproj_kernel,
        grid=(bsz, nt),
        in_specs=in_specs,
        out_specs=out_specs,
        out_shape=out_shape,
        compiler_params=pltpu.CompilerParams(
            dimension_semantics=("parallel", "parallel"), vmem_limit_bytes=VMEM_LIMIT),
        name="inproj",
    )(x, x, x, *consts)


def _wkv_prep(p, refs, reverse, shared):
    r_ref, v_ref, kn_ref, k_ref, b_ref, lw_ref = refs
    row, col, m64 = shared
    strict = m64 & ((col > row) if reverse else (col < row))
    incl = m64 & ((col >= row) if reverse else (col <= row))

    lw = lw_ref[p]
    l1, l2 = _split2(lw)
    tri = jnp.where(incl, 1.0, 0.0).astype(BF16)
    cum = _mm(tri, l1) + _mm(tri, l2)
    nc = WKV_STEP // WKV_CHUNK
    end = 0 if reverse else WKV_CHUNK - 1
    tot = jnp.concatenate(
        [jnp.broadcast_to(cum[ck * WKV_CHUNK + end:ck * WKV_CHUNK + end + 1], (WKV_CHUNK, LANES))
         for ck in range(nc)], axis=0)
    rr = r_ref[p].astype(F32)
    kn = kn_ref[p].astype(F32)
    kd = k_ref[p].astype(F32)
    bb = b_ref[p].astype(F32)
    winv = jnp.exp(-cum)
    wd = jnp.exp(tot - cum)
    return dict(
        strict=strict, incl=incl, wtot=jnp.exp(tot),
        rt=(rr * jnp.exp(cum)).astype(BF16), at=(-kn * jnp.exp(cum - lw)).astype(BF16),
        bt=(bb * winv).astype(BF16), kt=(kd * winv).astype(BF16),
        bh=(bb * wd).astype(BF16), kh=(kd * wd).astype(BF16), vb=v_ref[p].astype(BF16))


def _wkv_chains(chains, s_scr, masks):
    c8, c16, c32, cb, eye_c, eye_h, blk = masks
    n = WKV_STEP
    nc = n // WKV_CHUNK
    dot_nt = lambda a, b: lax.dot_general(a, b, NT, preferred_element_type=F32)
    dot_tn = lambda a, b: lax.dot_general(a, b, TN, preferred_element_type=F32)
    each = lambda fn: [fn(c) for c in chains]

    for c in chains:
        sl = slice(c["hh"] * HEAD_DIM, (c["hh"] + 1) * HEAD_DIM)
        pr = c["prep"]
        c.update(sl=sl, a=pr["at"][:, sl], r=pr["rt"][:, sl], b=pr["bt"][:, sl], k=pr["kt"][:, sl],
                 v=pr["vb"][:, sl], strict=pr["strict"], incl=pr["incl"])
    lab = each(lambda c: jnp.where(c["strict"], dot_nt(c["a"], c["b"]), 0.0))
    lak = each(lambda c: jnp.where(c["strict"], dot_nt(c["a"], c["k"]), 0.0).astype(BF16))
    mrb = each(lambda c: jnp.where(c["incl"], dot_nt(c["r"], c["b"]), 0.0).astype(BF16))
    mrk = each(lambda c: jnp.where(c["incl"], dot_nt(c["r"], c["k"]), 0.0).astype(BF16))

    lab_c = [sum(jnp.where(cb == ck, l[ck * WKV_CHUNK:(ck + 1) * WKV_CHUNK], 0.0) for ck in range(nc))
             for l in lab]
    bdiag = lambda zb: jnp.concatenate([zb] * nc, axis=0) * blk
    l8 = [jnp.where(c8, l, 0.0) for l in lab_c]
    p1 = [l.astype(BF16) for l in l8]
    p2 = [_mm(q, bdiag(q)).astype(BF16) for q in p1]
    p2d = [bdiag(q) for q in p2]
    x = [eye_c + l for l in l8]
    p4d = [bdiag(_mm(q, qd).astype(BF16)) for q, qd in zip(p2, p2d)]
    x = [xi + _mm(xi.astype(BF16), qd) for xi, qd in zip(x, p2d)]
    x = [xi + _mm(xi.astype(BF16), qd) for xi, qd in zip(x, p4d)]
    prev = c8
    for cur in (c16, c32, None):
        lvl = jnp.logical_not(prev) if cur is None else cur & jnp.logical_not(prev)
        xb = [xi.astype(BF16) for xi in x]
        t = [_mm(jnp.where(lvl, l, 0.0).astype(BF16), bdiag(xi)).astype(BF16) for l, xi in zip(lab_c, xb)]
        x = [xi + _mm(xbi, bdiag(ti)) for xi, xbi, ti in zip(x, xb, t)]
        prev = cur
    tb = [bdiag(xi.astype(BF16)) for xi in x]

    x1 = [_mm(l, c["v"]).astype(BF16) for l, c in zip(lak, chains)]
    wu = [_mm(t, c["a"]).astype(BF16) for t, c in zip(tb, chains)]
    uv = [_mm(t, xi).astype(BF16) for t, xi in zip(tb, x1)]
    q = [(c["r"].astype(F32) + _mm(m, w)).astype(BF16) for c, m, w in zip(chains, mrb, wu)]
    yl = [_mm(m, u) + _mm(mk, c["v"]) for m, u, mk, c in zip(mrb, uv, mrk, chains)]

    g, h = [], []
    for i, c in enumerate(chains):
        gi, hi = [], []
        for ck in range(nc):
            rs = slice(ck * WKV_CHUNK, (ck + 1) * WKV_CHUNK)
            b_c = c["prep"]["bh"][rs, c["sl"]]
            k_c = c["prep"]["kh"][rs, c["sl"]]
            wrow = c["prep"]["wtot"][ck * WKV_CHUNK:ck * WKV_CHUNK + 1, c["sl"]]
            gi.append((dot_tn(b_c, wu[i][rs])
                       + jnp.where(eye_h, jnp.broadcast_to(wrow, (HEAD_DIM, HEAD_DIM)), 0.0)).astype(BF16))
            hi.append(dot_tn(b_c, uv[i][rs]) + dot_tn(k_c, c["v"][rs]))
        g.append(gi)
        h.append(hi)

    s = [s_scr[c["d"], 2 * c["p"] + c["hh"]] for c in chains]
    for step in range(nc):
        for i, c in enumerate(chains):
            ck = nc - 1 - step if c["reverse"] else step
            rs = slice(ck * WKV_CHUNK, (ck + 1) * WKV_CHUNK)
            sb = s[i].astype(BF16)
            c["y_ref"][c["p"], rs, c["sl"]] = _mm(q[i][rs], sb) + yl[i][rs]
            s[i] = _mm(g[i][ck], sb) + h[i][ck]
    for i, c in enumerate(chains):
        s_scr[c["d"], 2 * c["p"] + c["hh"]] = s[i]


def _wkv_kernel(rf_ref, vf_ref, nf_ref, kf_ref, bf_ref, lf_ref,
                rb_ref, vb_ref, nb_ref, kb_ref, bb_ref, lb_ref, yf_ref, yb_ref, s_scr):
    n = WKV_STEP

    @pl.when(pl.program_id(1) == 0)
    def _():
        s_scr[...] = jnp.zeros_like(s_scr)

    def pair(p, carry):
        row = lax.broadcasted_iota(I32, (n, n), 0)
        col = lax.broadcasted_iota(I32, (n, n), 1)
        m64 = (row >> 6) == (col >> 6)
        er = lax.broadcasted_iota(I32, (HEAD_DIM, HEAD_DIM), 0)
        ec = lax.broadcasted_iota(I32, (HEAD_DIM, HEAD_DIM), 1)
        shared = (row, col, m64)
        rc = lax.broadcasted_iota(I32, (WKV_CHUNK, n), 0)
        lc = lax.broadcasted_iota(I32, (WKV_CHUNK, n), 1)
        li = lc & (WKV_CHUNK - 1)
        samec = lambda log2: (rc >> log2) == (li >> log2)
        masks = (samec(3), samec(4), samec(5), lc >> 6, jnp.where(rc == li, 1.0, 0.0).astype(F32), er == ec,
                 jnp.where(m64, 1.0, 0.0).astype(BF16))
        chains = []
        for pp in (2 * p, 2 * p + 1):
            fwd = _wkv_prep(pp, (rf_ref, vf_ref, nf_ref, kf_ref, bf_ref, lf_ref), False, shared)
            bwd = _wkv_prep(pp, (rb_ref, vb_ref, nb_ref, kb_ref, bb_ref, lb_ref), True, shared)
            chains += [dict(prep=pr, hh=hh, d=d, p=pp, reverse=rev, y_ref=y_ref)
                       for pr, d, rev, y_ref in ((fwd, 0, False, yf_ref), (bwd, 1, True, yb_ref))
                       for hh in range(2)]
        _wkv_chains(chains, s_scr, masks)
        return carry

    lax.fori_loop(0, N_PAIR // 2, pair, 0)


def _wkv(r, v, kn, k0, b0, lw0, k1, b1, lw1):
    bsz, _, t, _ = r.shape
    assert t % WKV_STEP == 0
    nj = t // WKV_STEP
    fwd = pl.BlockSpec((None, N_PAIR, WKV_STEP, LANES), lambda bi, j: (bi, 0, j, 0))
    bwd = pl.BlockSpec((None, N_PAIR, WKV_STEP, LANES), lambda bi, j: (bi, 0, nj - 1 - j, 0))
    out = jax.ShapeDtypeStruct((bsz, N_PAIR, t, LANES), F32)
    return pl.pallas_call(
        _wkv_kernel,
        grid=(bsz, nj),
        in_specs=[fwd] * 6 + [bwd] * 6,
        out_specs=(fwd, bwd),
        out_shape=(out, out),
        scratch_shapes=[pltpu.VMEM((2, 2 * N_PAIR, HEAD_DIM, HEAD_DIM), F32)],
        compiler_params=pltpu.CompilerParams(
            dimension_semantics=("parallel", "arbitrary"), vmem_limit_bytes=VMEM_LIMIT),
        name="wkv",
    )(r, v, kn, k0, b0, lw0, r, v, kn, k1, b1, lw1)


def _post_kernel(yf_ref, yb_ref, bonus_ref, g_ref, ya_ref, x_ref, gng_ref, gnb_ref, bd_ref, wout_ref,
                 nf_ref, rwh_ref, rwl_ref, rb_ref,
                 h_ref, xn_ref, ids_ref, gate_ref, rank_ref, cnt_ref, run_scr):
    tt = x_ref.shape[0]
    step = pl.program_id(0)

    @pl.when(step == 0)
    def _():
        run_scr[...] = jnp.zeros_like(run_scr)

    y = jnp.concatenate([yf_ref[p] + yb_ref[p] for p in range(N_PAIR)], axis=1)
    bd = bd_ref[...]
    inv = 1.0 / HEAD_DIM
    mu = _segsum(y, bd) * inv
    yc = y - mu
    var = _segsum(yc * yc, bd) * inv
    yn = yc * lax.rsqrt(var + GN_EPS) * gng_ref[...] + gnb_ref[...] + bonus_ref[...]
    ybm = (yn * g_ref[...]).astype(BF16)
    h = x_ref[...] + _mm(ya_ref[...], wout_ref[:GA, :]) + _mm(ybm, wout_ref[GA:, :])
    h_ref[...] = h
    xn = _rms(h, nf_ref[...])
    xn_ref[...] = xn

    lane = lax.broadcasted_iota(I32, (tt, LANES), 1)
    xh, xl = _split2(xn)
    logits = _mm(xh, rwh_ref[...]) + _mm(xl, rwh_ref[...]) + _mm(xh, rwl_ref[...])
    logits = jnp.where(lane < N_EXPERTS, logits + rb_ref[...], -jnp.inf)
    vals, ids, sel = [], [], jnp.zeros((tt, LANES), F32)
    cur = logits
    for _ in range(TOP_K):
        m = jnp.max(cur, axis=-1, keepdims=True)
        idx = jnp.min(jnp.where(cur == m, lane, LANES), axis=-1, keepdims=True)
        hit = lane == idx
        vals.append(m)
        ids.append(idx)
        sel = sel + jnp.where(hit, 1.0, 0.0)
        cur = jnp.where(hit, -jnp.inf, cur)
    es = [jnp.exp(vv - vals[0]) for vv in vals]
    den = es[0] + es[1] + es[2] + es[3]

    rt = lax.broadcasted_iota(I32, (tt, tt), 0)
    ct = lax.broadcasted_iota(I32, (tt, tt), 1)
    before = jnp.where(ct < rt, 1.0, 0.0).astype(BF16)
    cnt = _mm(before, sel.astype(BF16)) + run_scr[...]
    ids_o = jnp.zeros((tt, LANES), I32)
    gate_o = jnp.zeros((tt, LANES), F32)
    rank_o = jnp.zeros((tt, LANES), F32)
    for kx in range(TOP_K):
        rk = jnp.sum(jnp.where(lane == ids[kx], cnt, 0.0), axis=-1, keepdims=True)
        ids_o = jnp.where(lane == kx, ids[kx], ids_o)
        gate_o = jnp.where(lane == kx, es[kx] / den, gate_o)
        rank_o = jnp.where(lane == kx, rk, rank_o)
    ids_ref[...] = ids_o
    gate_ref[...] = gate_o
    rank_ref[...] = rank_o.astype(I32)
    run = run_scr[...] + jnp.sum(sel, axis=0, keepdims=True)
    run_scr[...] = run
    cnt_ref[...] = run


def _post(yf, yb, bonus, g, ya, x, w):
    bsz, t, _ = x.shape
    n = bsz * t
    tt = min(TILE_POST, t)
    nt = t // tt

    def full(a):
        nd = a.ndim
        return pl.BlockSpec(a.shape, lambda s: (0,) * nd)

    consts = (w["b_gn_g"], w["b_gn_b"], w["bd"], w["w_out"], w["norm_ffn"], w["router_wh"], w["router_wl"],
              w["router_b"])
    pair_spec = pl.BlockSpec((None, N_PAIR, tt, LANES), lambda s: (s // nt, 0, s % nt, 0))
    tok3 = lambda width: pl.BlockSpec((None, tt, width), lambda s: (s // nt, s % nt, 0))
    tok2 = lambda width: pl.BlockSpec((tt, width), lambda s: (s, 0))
    in_specs = [pair_spec, pair_spec, tok3(GB), tok3(GB), tok3(GA), tok3(D_MODEL)] + [full(a) for a in consts]
    out_shape = (jax.ShapeDtypeStruct((n, D_MODEL), F32),
                 jax.ShapeDtypeStruct((n, D_MODEL), F32),
                 jax.ShapeDtypeStruct((n, LANES), I32),
                 jax.ShapeDtypeStruct((n, LANES), F32),
                 jax.ShapeDtypeStruct((n, LANES), I32),
                 jax.ShapeDtypeStruct((1, LANES), F32))
    out_specs = (tok2(D_MODEL), tok2(D_MODEL), tok2(LANES), tok2(LANES), tok2(LANES),
                 pl.BlockSpec((1, LANES), lambda s: (0, 0)))
    return pl.pallas_call(
        _post_kernel,
        grid=(bsz * nt,),
        in_specs=in_specs,
        out_specs=out_specs,
        out_shape=out_shape,
        scratch_shapes=[pltpu.VMEM((1, LANES), F32)],
        compiler_params=pltpu.CompilerParams(
            dimension_semantics=("arbitrary",), vmem_limit_bytes=VMEM_LIMIT),
        name="post_router",
    )(yf, yb, bonus, g, ya, x, *consts)


def _scatter_kernel(dest_ref, xn_ref, zeros_ref, xs_ref, sem):
    del zeros_ref
    tt = xn_ref.shape[0]

    def issue(t, c):
        for kx in range(TOP_K):
            pltpu.make_async_copy(xn_ref.at[pl.ds(t, 1)],
                                  xs_ref.at[pl.ds(dest_ref[0, t * TOP_K + kx], 1)], sem).start()
        return c

    lax.fori_loop(0, tt, issue, 0, unroll=DMA_UNROLL)
    for _ in range(TOP_K):
        pltpu.make_async_copy(xn_ref, xs_ref.at[pl.ds(0, tt)], sem).wait()


def _scatter_rows(xn, dest, rows):
    n = xn.shape[0]
    tt = min(TILE_ROW, n)
    zeros = jnp.zeros((rows, D_MODEL), F32)
    return pl.pallas_call(
        _scatter_kernel,
        grid=(n // tt,),
        in_specs=[pl.BlockSpec((None, 1, tt * TOP_K), lambda s: (s, 0, 0), memory_space=pltpu.SMEM),
                  pl.BlockSpec((tt, D_MODEL), lambda s: (s, 0)),
                  pl.BlockSpec(memory_space=pl.ANY)],
        out_specs=pl.BlockSpec(memory_space=pl.ANY),
        out_shape=jax.ShapeDtypeStruct((rows, D_MODEL), F32),
        scratch_shapes=[pltpu.SemaphoreType.DMA(())],
        input_output_aliases={2: 0},
        compiler_params=pltpu.CompilerParams(
            dimension_semantics=("arbitrary",), vmem_limit_bytes=VMEM_LIMIT),
        name="moe_scatter",
    )(dest.reshape(n // tt, 1, tt * TOP_K), xn, zeros)


def _expert_kernel(be_ref, na_ref, xs_ref, w1_ref, b1_ref, w2_ref, b2_ref, o_ref):
    del be_ref
    s = pl.program_id(0)

    @pl.when(s < na_ref[0])
    def _():
        xb = xs_ref[...].astype(BF16)
        hdn = _mm(xb, w1_ref[...]) + b1_ref[...]
        glu = jnp.minimum(hdn[:, :D_FF], SWIGLU_LIMIT)
        lin = jnp.clip(hdn[:, D_FF:], -SWIGLU_LIMIT, SWIGLU_LIMIT)
        act = glu * jax.nn.sigmoid(SWIGLU_ALPHA * glu) * (lin + 1.0)
        o_ref[...] = _mm(act.astype(BF16), w2_ref[...]) + b2_ref[...]

    @pl.when(s >= na_ref[0])
    def _():
        o_ref[...] = jnp.zeros_like(o_ref)


def _experts(xs, block_e, n_active, w):
    rows = xs.shape[0]
    nb = rows // MOE_BLOCK
    grid_spec = pltpu.PrefetchScalarGridSpec(
        num_scalar_prefetch=2,
        grid=(nb,),
        in_specs=[
            pl.BlockSpec((MOE_BLOCK, D_MODEL), lambda s, be, na: (s, 0)),
            pl.BlockSpec((None, D_MODEL, 2 * D_FF), lambda s, be, na: (be[s], 0, 0)),
            pl.BlockSpec((None, 1, 2 * D_FF), lambda s, be, na: (be[s], 0, 0)),
            pl.BlockSpec((None, D_FF, D_MODEL), lambda s, be, na: (be[s], 0, 0)),
            pl.BlockSpec((None, 1, D_MODEL), lambda s, be, na: (be[s], 0, 0)),
        ],
        out_specs=pl.BlockSpec((MOE_BLOCK, D_MODEL), lambda s, be, na: (s, 0)),
    )
    return pl.pallas_call(
        _expert_kernel,
        grid_spec=grid_spec,
        out_shape=jax.ShapeDtypeStruct((rows, D_MODEL), F32),
        compiler_params=pltpu.CompilerParams(
            dimension_semantics=("arbitrary",), vmem_limit_bytes=VMEM_LIMIT),
        name="moe_experts",
    )(block_e, n_active, xs, w["moe_w1"], w["moe_b1"], w["moe_w2"], w["moe_b2"])


def _combine_kernel(dcur_ref, dnxt_ref, h_ref, gate_ref, p_ref, npl_ref, pg_ref, pp_ref, nfin_ref, os_ref,
                    y_ref, buf, sem):
    tt = h_ref.shape[0]
    s = pl.program_id(0)
    ns = pl.num_programs(0)
    slot = s % 2

    def issue_all(d_ref, sl):
        def body(t, c):
            for kx in range(TOP_K):
                pltpu.make_async_copy(os_ref.at[pl.ds(d_ref[0, t * TOP_K + kx], 1)],
                                      buf.at[sl, kx, pl.ds(t, 1)], sem.at[sl]).start()
            return c
        lax.fori_loop(0, tt, body, 0, unroll=DMA_UNROLL)

    @pl.when(s == 0)
    def _():
        issue_all(dcur_ref, 0)

    @pl.when(s + 1 < ns)
    def _():
        issue_all(dnxt_ref, 1 - slot)

    for kx in range(TOP_K):
        pltpu.make_async_copy(os_ref.at[pl.ds(0, tt)], buf.at[slot, kx], sem.at[slot]).wait()

    gate = gate_ref[...]
    h = h_ref[...]
    for kx in range(TOP_K):
        h = h + buf[slot, kx] * gate[:, kx:kx + 1]
    gt = jax.nn.sigmoid(_mm(_rms(h, npl_ref[...]).astype(BF16), pg_ref[...]))
    h = h + _mm(p_ref[...].astype(BF16), pp_ref[...]) * gt
    y_ref[...] = _rms(h, nfin_ref[...])


def _combine(h, gates, dest, p, os_rows, w):
    n = h.shape[0]
    tt = min(TILE_ROW, n)
    ns = n // tt

    def full(a):
        nd = a.ndim
        return pl.BlockSpec(a.shape, lambda s: (0,) * nd)

    consts = (w["norm_ple"], w["ple_gate"], w["ple_proj"], w["norm_final"])
    d2 = dest.reshape(ns, 1, tt * TOP_K)
    smem = lambda fn: pl.BlockSpec((None, 1, tt * TOP_K), fn, memory_space=pltpu.SMEM)
    return pl.pallas_call(
        _combine_kernel,
        grid=(ns,),
        in_specs=[smem(lambda s: (s, 0, 0)),
                  smem(lambda s: (jnp.minimum(s + 1, ns - 1), 0, 0)),
                  pl.BlockSpec((tt, D_MODEL), lambda s: (s, 0)),
                  pl.BlockSpec((tt, LANES), lambda s: (s, 0)),
                  pl.BlockSpec((tt, PLE_DIM), lambda s: (s, 0))]
        + [full(a) for a in consts]
        + [pl.BlockSpec(memory_space=pl.ANY)],
        out_specs=pl.BlockSpec((tt, D_MODEL), lambda s: (s, 0)),
        out_shape=jax.ShapeDtypeStruct((n, D_MODEL), F32),
        scratch_shapes=[pltpu.VMEM((2, TOP_K, tt, D_MODEL), F32), pltpu.SemaphoreType.DMA((2,))],
        compiler_params=pltpu.CompilerParams(
            dimension_semantics=("arbitrary",), vmem_limit_bytes=VMEM_LIMIT),
        name="moe_combine",
    )(d2, d2, h, gates, p, *consts, os_rows)


def _prep_weights(norm_mix, w_in, a_ln_g, a_ln_b, a_ws, a_bs, b_conv, b_w0, b_w2, b_a0, b_a2, b_g2,
                  b_kk, b_ka, b_rk, b_gn_g, b_gn_b, w_out, norm_ffn, router_w, router_b, moe_w1,
                  moe_b1, moe_w2, moe_b2, norm_ple, ple_proj, ple_gate, norm_final):
    row = lambda a: a.reshape(1, -1).astype(F32)

    def lora_pad(m):
        z = jnp.zeros((2, 2 * LORA, GB), F32)
        z = z.at[0, :LORA].set(m[0]).at[1, LORA:].set(m[1])
        return z.astype(BF16)

    seg = jnp.arange(GB, dtype=I32) // HEAD_DIM
    rw = jnp.pad(router_w[0].astype(F32), ((0, 0), (0, LANES - N_EXPERTS)))
    rw_hi = rw.astype(BF16)
    return {
        "norm_mix": row(norm_mix[0]),
        "w_in": w_in[0].astype(BF16),
        "a_ln_g": row(a_ln_g[0]),
        "a_ln_b": row(a_ln_b[0]),
        "a_ws": a_ws[0].reshape(-1, CHUNK_A).astype(BF16),
        "a_bs": jnp.repeat(a_bs[0].T.astype(F32), HEAD_DIM, axis=1),
        "b_conv": b_conv[0].astype(F32),
        "b_w0": b_w0[0].astype(F32),
        "b_w2": lora_pad(b_w2[0]),
        "b_a0": b_a0[0].astype(F32),
        "b_a2": lora_pad(b_a2[0]),
        "b_g2": b_g2[0].astype(BF16),
        "b_kk": row(b_kk[0]),
        "b_ka": row(b_ka[0]),
        "b_rk": row(b_rk[0]),
        "bd": (seg[:, None] == seg[None, :]).astype(BF16),
        "b_gn_g": row(b_gn_g[0]),
        "b_gn_b": row(b_gn_b[0]),
        "w_out": w_out[0].astype(BF16),
        "norm_ffn": row(norm_ffn[0]),
        "router_wh": rw_hi,
        "router_wl": (rw - rw_hi.astype(F32)).astype(BF16),
        "router_b": jnp.pad(router_b[0].astype(F32), (0, LANES - N_EXPERTS)).reshape(1, LANES),
        "moe_w1": moe_w1[0].astype(BF16),
        "moe_b1": moe_b1[0].astype(F32).reshape(N_EXPERTS, 1, 2 * D_FF),
        "moe_w2": moe_w2[0].astype(BF16),
        "moe_b2": moe_b2[0].astype(F32).reshape(N_EXPERTS, 1, D_MODEL),
        "norm_ple": row(norm_ple[0]),
        "ple_gate": ple_gate[0].astype(BF16),
        "ple_proj": ple_proj[0].astype(BF16),
        "norm_final": row(norm_final),
    }


def _forward(x, p, w):
    bsz, t, _ = x.shape
    n = bsz * t
    ya, r, v, kn, lw0, lw1, k0, k1, b0, b1, g, bonus = _inproj(x, w)
    yf, yb = _wkv(r, v, kn, k0, b0, lw0, k1, b1, lw1)
    h, xn, ids, gates, rank, counts = _post(yf, yb, bonus, g, ya, x, w)

    counts = counts[0, :N_EXPERTS].astype(I32)
    padded = (counts + MOE_BLOCK - 1) // MOE_BLOCK * MOE_BLOCK
    pends = jnp.cumsum(padded)
    pstarts = pends - padded
    n_blocks = -(-(n * TOP_K) // MOE_BLOCK) + N_EXPERTS
    dest = (pstarts[ids[:, :TOP_K]] + rank[:, :TOP_K]).astype(I32)
    block_start = jnp.arange(n_blocks, dtype=I32) * MOE_BLOCK
    block_e = jnp.minimum(jnp.sum(pends[None, :] <= block_start[:, None], axis=1), N_EXPERTS - 1).astype(I32)
    n_active = (pends[-1:] // MOE_BLOCK).astype(I32)

    xs = _scatter_rows(xn, dest, n_blocks * MOE_BLOCK)
    os_rows = _experts(xs, block_e, n_active, w)
    y = _combine(h, gates, dest, p.reshape(n, PLE_DIM), os_rows, w)
    return y.reshape(bsz, t, D_MODEL)


def kernel(x_prompt, x_sample, p_prompt, p_sample, norm_mix, w_in, a_ln_g, a_ln_b, a_ws, a_bs, b_conv, b_w0, b_w2, b_a0, b_a2, b_g2, b_kk, b_ka, b_rk, b_gn_g, b_gn_b, w_out, norm_ffn, router_w, router_b, moe_w1, moe_b1, moe_w2, moe_b2, norm_ple, ple_proj, ple_gate, norm_final):
    assert norm_mix.shape[0] == 1, "single-layer trunk"
    w = _prep_weights(norm_mix, w_in, a_ln_g, a_ln_b, a_ws, a_bs, b_conv, b_w0, b_w2, b_a0, b_a2, b_g2,
                      b_kk, b_ka, b_rk, b_gn_g, b_gn_b, w_out, norm_ffn, router_w, router_b, moe_w1,
                      moe_b1, moe_w2, moe_b2, norm_ple, ple_proj, ple_gate, norm_final)
    y_prompt = _forward(x_prompt, p_prompt[0], w)
    y_sample = _forward(x_sample, p_sample[0], w)
    return (y_prompt, y_sample)
```

```python
import math

import jax
import jax.numpy as jnp
from jax import lax
from jax.experimental import pallas as pl
from jax.experimental.pallas import tpu as pltpu

F32 = jnp.float32
BF16 = jnp.bfloat16
I32 = jnp.int32

D_MODEL = 1024
HEAD_DIM = 64
GA = 512
GB = 512
N_PAIR = GB // 128
CHUNK_A = 128
LORA = 64
LORA_G = 128
B_CONV = 3 * GB + 4 * LORA + LORA_G
N_EXPERTS = 32
TOP_K = 4
D_FF = 1024
PLE_DIM = 256
SWIGLU_ALPHA = 1.702
SWIGLU_LIMIT = 7.0
EPS = 1e-6
GN_EPS = 64e-5
DECAY_SCALE = math.exp(-0.5)

LANES = 128
TILE_IN = 256
WKV_STEP = 256
WKV_CHUNK = 64
TILE_POST = 256
TILE_ROW = 256
MOE_BLOCK = 512
VMEM_LIMIT = 56 * 1024 * 1024

NT = (((1,), (1,)), ((), ()))
TN = (((0,), (0,)), ((), ()))


def _mm(a, b):
    return jnp.dot(a, b, preferred_element_type=F32)


def _split2(q):
    hi = q.astype(BF16)
    lo = (q - hi.astype(F32)).astype(BF16)
    return hi, lo


def _segsum(q, bd):
    hi, lo = _split2(q)
    return _mm(hi, bd) + _mm(lo, bd)


def _gelu(z):
    return 0.5 * z * (1.0 + lax.erf(z * (1.0 / math.sqrt(2.0))))


def _rms(xv, g):
    ms = jnp.mean(xv * xv, axis=-1, keepdims=True)
    return xv * lax.rsqrt(ms + EPS) * g


def _inproj_kernel(x_ref, xp_ref, xn_ref, nm_ref, win_ref, lng_ref, lnb_ref, ws_ref, bs_ref,
                   conv_ref, w0_ref, w2_ref, a0_ref, a2_ref, g2_ref, kk_ref, ka_ref, rk_ref, bd_ref,
                   ya_ref, r_ref, v_ref, kn_ref, lw0_ref, lw1_ref, k0_ref, k1_ref, b0_ref, b1_ref,
                   g_ref, bonus_ref):
    tt = x_ref.shape[0]
    i = pl.program_id(1)
    last = pl.num_programs(1) - 1
    nm = nm_ref[...]

    xe = jnp.concatenate([x_ref[...], xp_ref[...], xn_ref[...]], axis=0)
    xe = _rms(xe, nm).astype(BF16)
    za = _mm(xe[:tt], win_ref[:, :2 * GA])
    ze = _mm(xe, win_ref[:, 2 * GA:])
    zb = ze[:tt]
    row_prev = jnp.where(i > 0, ze[tt + 7:tt + 8], 0.0)
    row_next = jnp.where(i < last, ze[tt + 8:tt + 9], 0.0)

    u = _gelu(za[:, :GA])
    v = _gelu(za[:, GA:])
    mu = jnp.mean(v, axis=-1, keepdims=True)
    var = jnp.mean(jnp.square(v - mu), axis=-1, keepdims=True)
    v = ((v - mu) * lax.rsqrt(var + EPS) * lng_ref[...] + lnb_ref[...]).astype(BF16)
    lane_head = lax.broadcasted_iota(I32, (CHUNK_A, GA), 1) // HEAD_DIM
    for c in range(tt // CHUNK_A):
        rows = slice(c * CHUNK_A, (c + 1) * CHUNK_A)
        o = _mm(ws_ref[...], v[rows])
        s = bs_ref[...]
        for h in range(GA // HEAD_DIM):
            s = s + jnp.where(lane_head == h, o[h * CHUNK_A:(h + 1) * CHUNK_A], 0.0)
        ya_ref[rows, :] = (u[rows] * s).astype(ya_ref.dtype)

    conv = conv_ref[...]
    trow = lax.broadcasted_iota(I32, (tt, 1), 0)
    z_prev = jnp.where(trow == 0, row_prev, pltpu.roll(zb, 1, axis=0))
    z_next = jnp.where(trow == tt - 1, row_next, pltpu.roll(zb, tt - 1, axis=0))
    zc = z_prev * conv[0:1] + zb * conv[1:2] + z_next * conv[2:3]
    r = zc[:, :GB]
    k = zc[:, GB:2 * GB]
    vv = zc[:, 2 * GB:3 * GB]
    o0 = 3 * GB
    xw = jnp.tanh(zc[:, o0:o0 + 2 * LORA]).astype(BF16)
    xa = zc[:, o0 + 2 * LORA:o0 + 4 * LORA].astype(BF16)
    xg = jax.nn.sigmoid(zc[:, o0 + 4 * LORA:]).astype(BF16)
    bd = bd_ref[...]
    kk = k * kk_ref[...]
    kk = kk / jnp.maximum(jnp.sqrt(_segsum(kk * kk, bd)), 1e-12)
    ka = ka_ref[...]
    lw_refs = (lw0_ref, lw1_ref)
    k_refs = (k0_ref, k1_ref)
    b_refs = (b0_ref, b1_ref)
    ksum = None
    for d in range(2):
        yw = w0_ref[d:d + 1, :] + _mm(xw, w2_ref[d])
        lw = -DECAY_SCALE * jax.nn.sigmoid(yw)
        a = jax.nn.sigmoid(a0_ref[d:d + 1, :] + _mm(xa, a2_ref[d]))
        kd = k * (1.0 + (a - 1.0) * ka)
        bb = kk * a
        ksum = kd if ksum is None else ksum + kd
        for p in range(N_PAIR):
            ls = slice(p * LANES, (p + 1) * LANES)
            lw_refs[d][p] = lw[:, ls]
            k_refs[d][p] = kd[:, ls].astype(k0_ref.dtype)
            b_refs[d][p] = bb[:, ls].astype(b0_ref.dtype)
    for p in range(N_PAIR):
        ls = slice(p * LANES, (p + 1) * LANES)
        r_ref[p] = r[:, ls].astype(r_ref.dtype)
        v_ref[p] = vv[:, ls].astype(v_ref.dtype)
        kn_ref[p] = kk[:, ls].astype(kn_ref.dtype)
    g_ref[...] = _mm(xg, g2_ref[...])
    bonus_ref[...] = _segsum(r * ksum * rk_ref[...], bd) * vv


def _inproj(x, w):
    bsz, t, _ = x.shape
    tt = min(TILE_IN, t)
    nt = t // tt
    t8 = tt // 8

    def full(a):
        nd = a.ndim
        return pl.BlockSpec(a.shape, lambda b, i: (0,) * nd)

    consts = (w["norm_mix"], w["w_in"], w["a_ln_g"], w["a_ln_b"], w["a_ws"], w["a_bs"], w["b_conv"],
              w["b_w0"], w["b_w2"], w["b_a0"], w["b_a2"], w["b_g2"], w["b_kk"], w["b_ka"], w["b_rk"],
              w["bd"])
    in_specs = [
        pl.BlockSpec((None, tt, D_MODEL), lambda b, i: (b, i, 0)),
        pl.BlockSpec((None, 8, D_MODEL), lambda b, i: (b, jnp.maximum(i * t8 - 1, 0), 0)),
        pl.BlockSpec((None, 8, D_MODEL), lambda b, i: (b, jnp.minimum((i + 1) * t8, t // 8 - 1), 0)),
    ] + [full(a) for a in consts]
    pair = lambda dt: jax.ShapeDtypeStruct((bsz, N_PAIR, t, LANES), dt)
    flat = lambda dt: jax.ShapeDtypeStruct((bsz, t, GB), dt)
    pair_spec = pl.BlockSpec((None, N_PAIR, tt, LANES), lambda b, i: (b, 0, i, 0))
    flat_spec = pl.BlockSpec((None, tt, GB), lambda b, i: (b, i, 0))
    out_shape = (flat(BF16),
                 pair(BF16), pair(BF16), pair(BF16),
                 pair(F32), pair(F32),
                 pair(BF16), pair(BF16), pair(BF16), pair(BF16),
                 flat(F32), flat(F32))
    out_specs = (flat_spec,) + (pair_spec,) * 9 + (flat_spec, flat_spec)
    return pl.pallas_call(
        _inproj_kernel,
        grid=(bsz, nt),
        in_specs=in_specs,
        out_specs=out_specs,
        out_shape=out_shape,
        compiler_params=pltpu.CompilerParams(
            dimension_semantics=("parallel", "parallel"), vmem_limit_bytes=VMEM_LIMIT),
        name="inproj",
    )(x, x, x, *consts)


def _wkv_prep(p, refs, reverse, shared):
    r_ref, v_ref, kn_ref, k_ref, b_ref, lw_ref = refs
    row, col, m64 = shared
    strict = m64 & ((col > row) if reverse else (col < row))
    incl = m64 & ((col >= row) if reverse else (col <= row))

    lw = lw_ref[p]
    l1, l2 = _split2(lw)
    tri = jnp.where(incl, 1.0, 0.0).astype(BF16)
    cum = _mm(tri, l1) + _mm(tri, l2)
    nc = WKV_STEP // WKV_CHUNK
    end = 0 if reverse else WKV_CHUNK - 1
    tot = jnp.concatenate(
        [jnp.broadcast_to(cum[ck * WKV_CHUNK + end:ck * WKV_CHUNK + end + 1], (WKV_CHUNK, LANES))
         for ck in range(nc)], axis=0)
    rr = r_ref[p].astype(F32)
    kn = kn_ref[p].astype(F32)
    kd = k_ref[p].astype(F32)
    bb = b_ref[p].astype(F32)
    winv = jnp.exp(-cum)
    wd = jnp.exp(tot - cum)
    return dict(
        strict=strict, incl=incl, wtot=jnp.exp(tot),
        rt=(rr * jnp.exp(cum)).astype(BF16), at=(-kn * jnp.exp(cum - lw)).astype(BF16),
        bt=(bb * winv).astype(BF16), kt=(kd * winv).astype(BF16),
        bh=(bb * wd).astype(BF16), kh=(kd * wd).astype(BF16), vb=v_ref[p].astype(BF16))


def _wkv_chains(chains, s_scr, masks):
    c8, c16, c32, cb, eye_c, eye_h, blk = masks
    n = WKV_STEP
    nc = n // WKV_CHUNK
    dot_nt = lambda a, b: lax.dot_general(a, b, NT, preferred_element_type=F32)
    dot_tn = lambda a, b: lax.dot_general(a, b, TN, preferred_element_type=F32)
    each = lambda fn: [fn(c) for c in chains]

    for c in chains:
        sl = slice(c["hh"] * HEAD_DIM, (c["hh"] + 1) * HEAD_DIM)
        pr = c["prep"]
        c.update(sl=sl, a=pr["at"][:, sl], r=pr["rt"][:, sl], b=pr["bt"][:, sl], k=pr["kt"][:, sl],
                 v=pr["vb"][:, sl], strict=pr["strict"], incl=pr["incl"])
    lab = each(lambda c: jnp.where(c["strict"], dot_nt(c["a"], c["b"]), 0.0))
    lak = each(lambda c: jnp.where(c["strict"], dot_nt(c["a"], c["k"]), 0.0).astype(BF16))
    mrb = each(lambda c: jnp.where(c["incl"], dot_nt(c["r"], c["b"]), 0.0).astype(BF16))
    mrk = each(lambda c: jnp.where(c["incl"], dot_nt(c["r"], c["k"]), 0.0).astype(BF16))

    lab_c = [sum(jnp.where(cb == ck, l[ck * WKV_CHUNK:(ck + 1) * WKV_CHUNK], 0.0) for ck in range(nc))
             for l in lab]
    bdiag = lambda zb: jnp.concatenate([zb] * nc, axis=0) * blk
    l8 = [jnp.where(c8, l, 0.0) for l in lab_c]
    p1 = [l.astype(BF16) for l in l8]
    p2 = [_mm(q, bdiag(q)).astype(BF16) for q in p1]
    p2d = [bdiag(q) for q in p2]
    x = [eye_c + l for l in l8]
    p4d = [bdiag(_mm(q, qd).astype(BF16)) for q, qd in zip(p2, p2d)]
    x = [xi + _mm(xi.astype(BF16), qd) for xi, qd in zip(x, p2d)]
    x = [xi + _mm(xi.astype(BF16), qd) for xi, qd in zip(x, p4d)]
    prev = c8
    for cur in (c16, c32, None):
        lvl = jnp.logical_not(prev) if cur is None else cur & jnp.logical_not(prev)
        xb = [xi.astype(BF16) for xi in x]
        t = [_mm(jnp.where(lvl, l, 0.0).astype(BF16), bdiag(xi)).astype(BF16) for l, xi in zip(lab_c, xb)]
        x = [xi + _mm(xbi, bdiag(ti)) for xi, xbi, ti in zip(x, xb, t)]
        prev = cur
    tb = [bdiag(xi.astype(BF16)) for xi in x]

    x1 = [_mm(l, c["v"]).astype(BF16) for l, c in zip(lak, chains)]
    wu = [_mm(t, c["a"]).astype(BF16) for t, c in zip(tb, chains)]
    uv = [_mm(t, xi).astype(BF16) for t, xi in zip(tb, x1)]
    q = [(c["r"].astype(F32) + _mm(m, w)).astype(BF16) for c, m, w in zip(chains, mrb, wu)]
    yl = [_mm(m, u) + _mm(mk, c["v"]) for m, u, mk, c in zip(mrb, uv, mrk, chains)]

    g, h = [], []
    for i, c in enumerate(chains):
        gi, hi = [], []
        for ck in range(nc):
            rs = slice(ck * WKV_CHUNK, (ck + 1) * WKV_CHUNK)
            b_c = c["prep"]["bh"][rs, c["sl"]]
            k_c = c["prep"]["kh"][rs, c["sl"]]
            wrow = c["prep"]["wtot"][ck * WKV_CHUNK:ck * WKV_CHUNK + 1, c["sl"]]
            gi.append((dot_tn(b_c, wu[i][rs])
                       + jnp.where(eye_h, jnp.broadcast_to(wrow, (HEAD_DIM, HEAD_DIM)), 0.0)).astype(BF16))
            hi.append(dot_tn(b_c, uv[i][rs]) + dot_tn(k_c, c["v"][rs]))
        g.append(gi)
        h.append(hi)

    s = [s_scr[c["d"], 2 * c["p"] + c["hh"]] for c in chains]
    for step in range(nc):
        for i, c in enumerate(chains):
            ck = nc - 1 - step if c["reverse"] else step
            rs = slice(ck * WKV_CHUNK, (ck + 1) * WKV_CHUNK)
            sb = s[i].astype(BF16)
            c["y_ref"][c["p"], rs, c["sl"]] = _mm(q[i][rs], sb) + yl[i][rs]
            s[i] = _mm(g[i][ck], sb) + h[i][ck]
    for i, c in enumerate(chains):
        s_scr[c["d"], 2 * c["p"] + c["hh"]] = s[i]


def _wkv_kernel(rf_ref, vf_ref, nf_ref, kf_ref, bf_ref, lf_ref,
                rb_ref, vb_ref, nb_ref, kb_ref, bb_ref, lb_ref, yf_ref, yb_ref, s_scr):
    n = WKV_STEP

    @pl.when(pl.program_id(1) == 0)
    def _():
        s_scr[...] = jnp.zeros_like(s_scr)

    def pair(p, carry):
        row = lax.broadcasted_iota(I32, (n, n), 0)
        col = lax.broadcasted_iota(I32, (n, n), 1)
        m64 = (row >> 6) == (col >> 6)
        er = lax.broadcasted_iota(I32, (HEAD_DIM, HEAD_DIM), 0)
        ec = lax.broadcasted_iota(I32, (HEAD_DIM, HEAD_DIM), 1)
        shared = (row, col, m64)
        rc = lax.broadcasted_iota(I32, (WKV_CHUNK, n), 0)
        lc = lax.broadcasted_iota(I32, (WKV_CHUNK, n), 1)
        li = lc & (WKV_CHUNK - 1)
        samec = lambda log2: (rc >> log2) == (li >> log2)
        masks = (samec(3), samec(4), samec(5), lc >> 6, jnp.where(rc == li, 1.0, 0.0).astype(F32), er == ec,
                 jnp.where(m64, 1.0, 0.0).astype(BF16))
        chains = []
        for pp in (2 * p, 2 * p + 1):
            fwd = _wkv_prep(pp, (rf_ref, vf_ref, nf_ref, kf_ref, bf_ref, lf_ref), False, shared)
            bwd = _wkv_prep(pp, (rb_ref, vb_ref, nb_ref, kb_ref, bb_ref, lb_ref), True, shared)
            chains += [dict(prep=pr, hh=hh, d=d, p=pp, reverse=rev, y_ref=y_ref)
                       for pr, d, rev, y_ref in ((fwd, 0, False, yf_ref), (bwd, 1, True, yb_ref))
                       for hh in range(2)]
        _wkv_chains(chains, s_scr, masks)
        return carry

    lax.fori_loop(0, N_PAIR // 2, pair, 0)


def _wkv(r, v, kn, k0, b0, lw0, k1, b1, lw1):
    bsz, _, t, _ = r.shape
    assert t % WKV_STEP == 0
    nj = t // WKV_STEP
    fwd = pl.BlockSpec((None, N_PAIR, WKV_STEP, LANES), lambda bi, j: (bi, 0, j, 0))
    bwd = pl.BlockSpec((None, N_PAIR, WKV_STEP, LANES), lambda bi, j: (bi, 0, nj - 1 - j, 0))
    out = jax.ShapeDtypeStruct((bsz, N_PAIR, t, LANES), F32)
    return pl.pallas_call(
        _wkv_kernel,
        grid=(bsz, nj),
        in_specs=[fwd] * 6 + [bwd] * 6,
        out_specs=(fwd, bwd),
        out_shape=(out, out),
        scratch_shapes=[pltpu.VMEM((2, 2 * N_PAIR, HEAD_DIM, HEAD_DIM), F32)],
        compiler_params=pltpu.CompilerParams(
            dimension_semantics=("parallel", "arbitrary"), vmem_limit_bytes=VMEM_LIMIT),
        name="wkv",
    )(r, v, kn, k0, b0, lw0, r, v, kn, k1, b1, lw1)


def _post_kernel(yf_ref, yb_ref, bonus_ref, g_ref, ya_ref, x_ref, gng_ref, gnb_ref, bd_ref, wout_ref,
                 nf_ref, rwh_ref, rwl_ref, rb_ref,
                 h_ref, xn_ref, ids_ref, gate_ref, rank_ref, cnt_ref, before_ref, run_scr):
    tt = x_ref.shape[0]
    step = pl.program_id(0)

    @pl.when(step == 0)
    def _():
        run_scr[...] = jnp.zeros_like(run_scr)

    y = jnp.concatenate([yf_ref[p] + yb_ref[p] for p in range(N_PAIR)], axis=1)
    bd = bd_ref[...]
    inv = 1.0 / HEAD_DIM
    mu = _segsum(y, bd) * inv
    yc = y - mu
    var = _segsum(yc * yc, bd) * inv
    yn = yc * lax.rsqrt(var + GN_EPS) * gng_ref[...] + gnb_ref[...] + bonus_ref[...]
    ybm = (yn * g_ref[...]).astype(BF16)
    h = x_ref[...] + _mm(ya_ref[...], wout_ref[:GA, :]) + _mm(ybm, wout_ref[GA:, :])
    h_ref[...] = h
    xn = _rms(h, nf_ref[...])
    xn_ref[...] = xn

    lane = lax.broadcasted_iota(I32, (tt, LANES), 1)
    xh, xl = _split2(xn)
    logits = _mm(xh, rwh_ref[...]) + _mm(xl, rwh_ref[...]) + _mm(xh, rwl_ref[...])
    logits = jnp.where(lane < N_EXPERTS, logits + rb_ref[...], -jnp.inf)
    vals, ids, sel = [], [], jnp.zeros((tt, LANES), F32)
    cur = logits
    for _ in range(TOP_K):
        m = jnp.max(cur, axis=-1, keepdims=True)
        idx = jnp.min(jnp.where(cur == m, lane, LANES), axis=-1, keepdims=True)
        hit = lane == idx
        vals.append(m)
        ids.append(idx)
        sel = sel + jnp.where(hit, 1.0, 0.0)
        cur = jnp.where(hit, -jnp.inf, cur)
    es = [jnp.exp(vv - vals[0]) for vv in vals]
    den = es[0] + es[1] + es[2] + es[3]

    rt = lax.broadcasted_iota(I32, (tt, tt), 0)
    ct = lax.broadcasted_iota(I32, (tt, tt), 1)
    before = jnp.where(ct < rt, 1.0, 0.0).astype(BF16)
    before_ref[...] = run_scr[...]
    cnt = _mm(before, sel.astype(BF16)) + run_scr[...]
    ids_o = jnp.zeros((tt, LANES), I32)
    gate_o = jnp.zeros((tt, LANES), F32)
    rank_o = jnp.zeros((tt, LANES), F32)
    for kx in range(TOP_K):
        rk = jnp.sum(jnp.where(lane == ids[kx], cnt, 0.0), axis=-1, keepdims=True)
        ids_o = jnp.where(lane == kx, ids[kx], ids_o)
        gate_o = jnp.where(lane == kx, es[kx] / den, gate_o)
        rank_o = jnp.where(lane == kx, rk, rank_o)
    ids_ref[...] = ids_o
    gate_ref[...] = gate_o
    rank_ref[...] = rank_o.astype(I32)
    run = run_scr[...] + jnp.sum(sel, axis=0, keepdims=True)
    run_scr[...] = run
    cnt_ref[...] = run


def _post(yf, yb, bonus, g, ya, x, w):
    bsz, t, _ = x.shape
    n = bsz * t
    tt = min(TILE_POST, t)
    nt = t // tt

    def full(a):
        nd = a.ndim
        return pl.BlockSpec(a.shape, lambda s: (0,) * nd)

    consts = (w["b_gn_g"], w["b_gn_b"], w["bd"], w["w_out"], w["norm_ffn"], w["router_wh"], w["router_wl"],
              w["router_b"])
    pair_spec = pl.BlockSpec((None, N_PAIR, tt, LANES), lambda s: (s // nt, 0, s % nt, 0))
    tok3 = lambda width: pl.BlockSpec((None, tt, width), lambda s: (s // nt, s % nt, 0))
    tok2 = lambda width: pl.BlockSpec((tt, width), lambda s: (s, 0))
    in_specs = [pair_spec, pair_spec, tok3(GB), tok3(GB), tok3(GA), tok3(D_MODEL)] + [full(a) for a in consts]
    out_shape = (jax.ShapeDtypeStruct((n, D_MODEL), F32),
                 jax.ShapeDtypeStruct((n, D_MODEL), F32),
                 jax.ShapeDtypeStruct((n, LANES), I32),
                 jax.ShapeDtypeStruct((n, LANES), F32),
                 jax.ShapeDtypeStruct((n, LANES), I32),
                 jax.ShapeDtypeStruct((1, LANES), F32),
                 jax.ShapeDtypeStruct((bsz * nt, 1, LANES), F32))
    out_specs = (tok2(D_MODEL), tok2(D_MODEL), tok2(LANES), tok2(LANES), tok2(LANES),
                 pl.BlockSpec((1, LANES), lambda s: (0, 0)),
                 pl.BlockSpec((None, 1, LANES), lambda s: (s, 0, 0)))
    return pl.pallas_call(
        _post_kernel,
        grid=(bsz * nt,),
        in_specs=in_specs,
        out_specs=out_specs,
        out_shape=out_shape,
        scratch_shapes=[pltpu.VMEM((1, LANES), F32)],
        compiler_params=pltpu.CompilerParams(
            dimension_semantics=("arbitrary",), vmem_limit_bytes=VMEM_LIMIT),
        name="post_router",
    )(yf, yb, bonus, g, ya, x, *consts)


RUN_ALIGN = 8
RUN_SIZES = tuple(1 << i for i in range(8, 2, -1))
TILE_SORT = TILE_ROW * TOP_K + N_EXPERTS * RUN_ALIGN


def _tile_runs(sc_ref, copy, wait):
    def per_expert(e, c):
        o = sc_ref[0, e]
        d = sc_ref[0, N_EXPERTS + e]
        ln = sc_ref[0, 2 * N_EXPERTS + e]
        for size in RUN_SIZES:
            bit = ln & size

            @pl.when(bit != 0)
            def _():
                cp = copy(pl.multiple_of(o, RUN_ALIGN), pl.multiple_of(d, RUN_ALIGN), size)
                if wait:
                    cp.wait()
                else:
                    cp.start()

            o = o + bit
            d = d + bit
        return c

    lax.fori_loop(0, N_EXPERTS, per_expert, 0)


def _slot_matrix(slot, vals, width):
    tt = slot.shape[0]
    lane = lax.broadcasted_iota(I32, (tt, width), 1)
    m = jnp.zeros((tt, width), F32)
    for kx in range(TOP_K):
        m = m + jnp.where(lane == slot[:, kx:kx + 1], vals[kx], 0.0)
    return m


def _scatter_kernel(sc_ref, slot_ref, xn_ref, zeros_ref, xs_ref, sb, sem):
    del zeros_ref
    p01 = _slot_matrix(slot_ref[...], (1.0,) * TOP_K, TILE_SORT).astype(BF16)
    sb[...] = lax.dot_general(p01, xn_ref[...].astype(BF16), TN, preferred_element_type=F32)
    copy = lambda o, d, size: pltpu.make_async_copy(sb.at[pl.ds(o, size)], xs_ref.at[pl.ds(d, size)], sem)
    _tile_runs(sc_ref, copy, wait=False)
    _tile_runs(sc_ref, copy, wait=True)


def _scatter_rows(xn, slot, runs, rows):
    n = xn.shape[0]
    tt = TILE_ROW
    zeros = jnp.zeros((rows, D_MODEL), F32)
    return pl.pallas_call(
        _scatter_kernel,
        grid=(n // tt,),
        in_specs=[pl.BlockSpec((None, 1, 3 * N_EXPERTS), lambda s: (s, 0, 0), memory_space=pltpu.SMEM),
                  pl.BlockSpec((tt, LANES), lambda s: (s, 0)),
                  pl.BlockSpec((tt, D_MODEL), lambda s: (s, 0)),
                  pl.BlockSpec(memory_space=pl.ANY)],
        out_specs=pl.BlockSpec(memory_space=pl.ANY),
        out_shape=jax.ShapeDtypeStruct((rows, D_MODEL), F32),
        scratch_shapes=[pltpu.VMEM((TILE_SORT, D_MODEL), F32), pltpu.SemaphoreType.DMA(())],
        input_output_aliases={3: 0},
        compiler_params=pltpu.CompilerParams(
            dimension_semantics=("arbitrary",), vmem_limit_bytes=VMEM_LIMIT),
        name="moe_scatter",
    )(runs, slot, xn, zeros)


def _expert_kernel(be_ref, na_ref, xs_ref, w1_ref, b1_ref, w2_ref, b2_ref, o_ref):
    del be_ref
    s = pl.program_id(0)

    @pl.when(s < na_ref[0])
    def _():
        xb = xs_ref[...].astype(BF16)
        hdn = _mm(xb, w1_ref[...]) + b1_ref[...]
        glu = jnp.minimum(hdn[:, :D_FF], SWIGLU_LIMIT)
        lin = jnp.clip(hdn[:, D_FF:], -SWIGLU_LIMIT, SWIGLU_LIMIT)
        act = glu * jax.nn.sigmoid(SWIGLU_ALPHA * glu) * (lin + 1.0)
        o_ref[...] = _mm(act.astype(BF16), w2_ref[...]) + b2_ref[...]

    @pl.when(s >= na_ref[0])
    def _():
        o_ref[...] = jnp.zeros_like(o_ref)


def _experts(xs, block_e, n_active, w):
    rows = xs.shape[0]
    nb = rows // MOE_BLOCK
    grid_spec = pltpu.PrefetchScalarGridSpec(
        num_scalar_prefetch=2,
        grid=(nb,),
        in_specs=[
            pl.BlockSpec((MOE_BLOCK, D_MODEL), lambda s, be, na: (s, 0)),
            pl.BlockSpec((None, D_MODEL, 2 * D_FF), lambda s, be, na: (be[s], 0, 0)),
            pl.BlockSpec((None, 1, 2 * D_FF), lambda s, be, na: (be[s], 0, 0)),
            pl.BlockSpec((None, D_FF, D_MODEL), lambda s, be, na: (be[s], 0, 0)),
            pl.BlockSpec((None, 1, D_MODEL), lambda s, be, na: (be[s], 0, 0)),
        ],
        out_specs=pl.BlockSpec((MOE_BLOCK, D_MODEL), lambda s, be, na: (s, 0)),
    )
    return pl.pallas_call(
        _expert_kernel,
        grid_spec=grid_spec,
        out_shape=jax.ShapeDtypeStruct((rows, D_MODEL), F32),
        compiler_params=pltpu.CompilerParams(
            dimension_semantics=("arbitrary",), vmem_limit_bytes=VMEM_LIMIT),
        name="moe_experts",
    )(block_e, n_active, xs, w["moe_w1"], w["moe_b1"], w["moe_w2"], w["moe_b2"])


def _combine_kernel(scur_ref, snxt_ref, slot_ref, h_ref, gate_ref, p_ref, npl_ref, pg_ref, pp_ref, nfin_ref,
                    os_ref, y_ref, gb, sem):
    s = pl.program_id(0)
    ns = pl.num_programs(0)
    slot = s % 2

    def runs(sc_ref, sl, wait):
        _tile_runs(sc_ref, lambda o, d, size: pltpu.make_async_copy(
            os_ref.at[pl.ds(d, size)], gb.at[sl, pl.ds(o, size)], sem.at[sl]), wait)

    @pl.when(s == 0)
    def _():
        gb[...] = jnp.zeros_like(gb)
        runs(scur_ref, 0, False)

    @pl.when(s + 1 < ns)
    def _():
        runs(snxt_ref, 1 - slot, False)

    runs(scur_ref, slot, True)

    gate = gate_ref[...]
    pm = _slot_matrix(slot_ref[...], [gate[:, kx:kx + 1] for kx in range(TOP_K)], TILE_SORT)
    ph, plo = _split2(pm)
    gbb = gb[slot].astype(BF16)
    h = h_ref[...] + _mm(ph, gbb) + _mm(plo, gbb)
    gt = jax.nn.sigmoid(_mm(_rms(h, npl_ref[...]).astype(BF16), pg_ref[...]))
    h = h + _mm(p_ref[...].astype(BF16), pp_ref[...]) * gt
    y_ref[...] = _rms(h, nfin_ref[...])


def _combine(h, gates, slot, runs, p, os_rows, w):
    n = h.shape[0]
    tt = TILE_ROW
    ns = n // tt

    def full(a):
        nd = a.ndim
        return pl.BlockSpec(a.shape, lambda s: (0,) * nd)

    consts = (w["norm_ple"], w["ple_gate"], w["ple_proj"], w["norm_final"])
    smem = lambda fn: pl.BlockSpec((None, 1, 3 * N_EXPERTS), fn, memory_space=pltpu.SMEM)
    return pl.pallas_call(
        _combine_kernel,
        grid=(ns,),
        in_specs=[smem(lambda s: (s, 0, 0)),
                  smem(lambda s: (jnp.minimum(s + 1, ns - 1), 0, 0)),
                  pl.BlockSpec((tt, LANES), lambda s: (s, 0)),
                  pl.BlockSpec((tt, D_MODEL), lambda s: (s, 0)),
                  pl.BlockSpec((tt, LANES), lambda s: (s, 0)),
                  pl.BlockSpec((tt, PLE_DIM), lambda s: (s, 0))]
        + [full(a) for a in consts]
        + [pl.BlockSpec(memory_space=pl.ANY)],
        out_specs=pl.BlockSpec((tt, D_MODEL), lambda s: (s, 0)),
        out_shape=jax.ShapeDtypeStruct((n, D_MODEL), F32),
        scratch_shapes=[pltpu.VMEM((2, TILE_SORT, D_MODEL), F32), pltpu.SemaphoreType.DMA((2,))],
        compiler_params=pltpu.CompilerParams(
            dimension_semantics=("arbitrary",), vmem_limit_bytes=VMEM_LIMIT),
        name="moe_combine",
    )(runs, runs, slot, h, gates, p, *consts, os_rows)


def _prep_weights(norm_mix, w_in, a_ln_g, a_ln_b, a_ws, a_bs, b_conv, b_w0, b_w2, b_a0, b_a2, b_g2,
                  b_kk, b_ka, b_rk, b_gn_g, b_gn_b, w_out, norm_ffn, router_w, router_b, moe_w1,
                  moe_b1, moe_w2, moe_b2, norm_ple, ple_proj, ple_gate, norm_final):
    row = lambda a: a.reshape(1, -1).astype(F32)

    def lora_pad(m):
        z = jnp.zeros((2, 2 * LORA, GB), F32)
        z = z.at[0, :LORA].set(m[0]).at[1, LORA:].set(m[1])
        return z.astype(BF16)

    seg = jnp.arange(GB, dtype=I32) // HEAD_DIM
    rw = jnp.pad(router_w[0].astype(F32), ((0, 0), (0, LANES - N_EXPERTS)))
    rw_hi = rw.astype(BF16)
    return {
        "norm_mix": row(norm_mix[0]),
        "w_in": w_in[0].astype(BF16),
        "a_ln_g": row(a_ln_g[0]),
        "a_ln_b": row(a_ln_b[0]),
        "a_ws": a_ws[0].reshape(-1, CHUNK_A).astype(BF16),
        "a_bs": jnp.repeat(a_bs[0].T.astype(F32), HEAD_DIM, axis=1),
        "b_conv": b_conv[0].astype(F32),
        "b_w0": b_w0[0].astype(F32),
        "b_w2": lora_pad(b_w2[0]),
        "b_a0": b_a0[0].astype(F32),
        "b_a2": lora_pad(b_a2[0]),
        "b_g2": b_g2[0].astype(BF16),
        "b_kk": row(b_kk[0]),
        "b_ka": row(b_ka[0]),
        "b_rk": row(b_rk[0]),
        "bd": (seg[:, None] == seg[None, :]).astype(BF16),
        "b_gn_g": row(b_gn_g[0]),
        "b_gn_b": row(b_gn_b[0]),
        "w_out": w_out[0].astype(BF16),
        "norm_ffn": row(norm_ffn[0]),
        "router_wh": rw_hi,
        "router_wl": (rw - rw_hi.astype(F32)).astype(BF16),
        "router_b": jnp.pad(router_b[0].astype(F32), (0, LANES - N_EXPERTS)).reshape(1, LANES),
        "moe_w1": moe_w1[0].astype(BF16),
        "moe_b1": moe_b1[0].astype(F32).reshape(N_EXPERTS, 1, 2 * D_FF),
        "moe_w2": moe_w2[0].astype(BF16),
        "moe_b2": moe_b2[0].astype(F32).reshape(N_EXPERTS, 1, D_MODEL),
        "norm_ple": row(norm_ple[0]),
        "ple_gate": ple_gate[0].astype(BF16),
        "ple_proj": ple_proj[0].astype(BF16),
        "norm_final": row(norm_final),
    }


def _forward(x, p, w):
    bsz, t, _ = x.shape
    n = bsz * t
    ya, r, v, kn, lw0, lw1, k0, k1, b0, b1, g, bonus = _inproj(x, w)
    yf, yb = _wkv(r, v, kn, k0, b0, lw0, k1, b1, lw1)
    h, xn, ids, gates, rank, counts, before = _post(yf, yb, bonus, g, ya, x, w)

    nt = n // TILE_ROW
    counts = counts[0, :N_EXPERTS].astype(I32)
    before = before[:, 0, :N_EXPERTS].astype(I32)
    tile_cnt = jnp.concatenate([before[1:], counts[None]], axis=0) - before
    tile_pad = (tile_cnt + RUN_ALIGN - 1) // RUN_ALIGN * RUN_ALIGN
    tile_off = jnp.cumsum(tile_pad, axis=1) - tile_pad
    before_pad = jnp.cumsum(tile_pad, axis=0) - tile_pad
    padded = (jnp.sum(tile_pad, axis=0) + MOE_BLOCK - 1) // MOE_BLOCK * MOE_BLOCK
    pends = jnp.cumsum(padded)
    pstarts = pends - padded
    n_blocks = -(-(n * TOP_K + nt * N_EXPERTS * (RUN_ALIGN - 1)) // MOE_BLOCK) + N_EXPERTS
    block_start = jnp.arange(n_blocks, dtype=I32) * MOE_BLOCK
    block_e = jnp.minimum(jnp.sum(pends[None, :] <= block_start[:, None], axis=1), N_EXPERTS - 1).astype(I32)
    n_active = (pends[-1:] // MOE_BLOCK).astype(I32)
    runs = jnp.concatenate([tile_off, pstarts[None, :] + before_pad, tile_pad], axis=1)
    runs = runs.reshape(nt, 1, 3 * N_EXPERTS).astype(I32)
    rel = jnp.repeat(tile_off - before, TILE_ROW, axis=0)
    slot = jnp.take_along_axis(rel, ids[:, :TOP_K], axis=1) + rank[:, :TOP_K]
    slot = jnp.pad(slot, ((0, 0), (0, LANES - TOP_K)), constant_values=-1).astype(I32)

    xs = _scatter_rows(xn, slot, runs, n_blocks * MOE_BLOCK)
    os_rows = _experts(xs, block_e, n_active, w)
    y = _combine(h, gates, slot, runs, p.reshape(n, PLE_DIM), os_rows, w)
    return y.reshape(bsz, t, D_MODEL)


def kernel(x_prompt, x_sample, p_prompt, p_sample, norm_mix, w_in, a_ln_g, a_ln_b, a_ws, a_bs, b_conv, b_w0, b_w2, b_a0, b_a2, b_g2, b_kk, b_ka, b_rk, b_gn_g, b_gn_b, w_out, norm_ffn, router_w, router_b, moe_w1, moe_b1, moe_w2, moe_b2, norm_ple, ple_proj, ple_gate, norm_final):
    assert norm_mix.shape[0] == 1, "single-layer trunk"
    w = _prep_weights(norm_mix, w_in, a_ln_g, a_ln_b, a_ws, a_bs, b_conv, b_w0, b_w2, b_a0, b_a2, b_g2,
                      b_kk, b_ka, b_rk, b_gn_g, b_gn_b, w_out, norm_ffn, router_w, router_b, moe_w1,
                      moe_b1, moe_w2, moe_b2, norm_ple, ple_proj, ple_gate, norm_final)
    y_prompt = _forward(x_prompt, p_prompt[0], w)
    y_sample = _forward(x_sample, p_sample[0], w)
    return (y_prompt, y_sample)
```

```python
import math

import jax
import jax.numpy as jnp
from jax import lax
from jax.experimental import pallas as pl
from jax.experimental.pallas import tpu as pltpu

F32 = jnp.float32
BF16 = jnp.bfloat16
I32 = jnp.int32

D_MODEL = 1024
HEAD_DIM = 64
GA = 512
GB = 512
N_PAIR = GB // 128
CHUNK_A = 128
LORA = 64
LORA_G = 128
B_CONV = 3 * GB + 4 * LORA + LORA_G
N_EXPERTS = 32
TOP_K = 4
D_FF = 1024
PLE_DIM = 256
SWIGLU_ALPHA = 1.702
SWIGLU_LIMIT = 7.0
EPS = 1e-6
GN_EPS = 64e-5
DECAY_SCALE = math.exp(-0.5)

LANES = 128
TILE_IN = 256
WKV_STEP = 256
WKV_CHUNK = 64
TILE_POST = 256
TILE_ROW = 256
MOE_BLOCK = 512
VMEM_LIMIT = 56 * 1024 * 1024

NT = (((1,), (1,)), ((), ()))
TN = (((0,), (0,)), ((), ()))


def _mm(a, b):
    return jnp.dot(a, b, preferred_element_type=F32)


def _split2(q):
    hi = q.astype(BF16)
    lo = (q - hi.astype(F32)).astype(BF16)
    return hi, lo


def _segsum(q, bd):
    hi, lo = _split2(q)
    return _mm(hi, bd) + _mm(lo, bd)


def _gelu(z):
    return 0.5 * z * (1.0 + lax.erf(z * (1.0 / math.sqrt(2.0))))


def _rms(xv, g):
    ms = jnp.mean(xv * xv, axis=-1, keepdims=True)
    return xv * lax.rsqrt(ms + EPS) * g


def _inproj_kernel(x_ref, xp_ref, xn_ref, nm_ref, win_ref, lng_ref, lnb_ref, ws_ref, bs_ref,
                   conv_ref, w0_ref, w2_ref, a0_ref, a2_ref, g2_ref, kk_ref, ka_ref, rk_ref, bd_ref,
                   ya_ref, r_ref, v_ref, kn_ref, lw0_ref, lw1_ref, k0_ref, k1_ref, b0_ref, b1_ref,
                   g_ref, bonus_ref):
    tt = x_ref.shape[0]
    i = pl.program_id(1)
    last = pl.num_programs(1) - 1
    nm = nm_ref[...]

    xe = jnp.concatenate([x_ref[...], xp_ref[...], xn_ref[...]], axis=0)
    xe = _rms(xe, nm).astype(BF16)
    za = _mm(xe[:tt], win_ref[:, :2 * GA])
    ze = _mm(xe, win_ref[:, 2 * GA:])
    zb = ze[:tt]
    row_prev = jnp.where(i > 0, ze[tt + 7:tt + 8], 0.0)
    row_next = jnp.where(i < last, ze[tt + 8:tt + 9], 0.0)

    u = _gelu(za[:, :GA])
    v = _gelu(za[:, GA:])
    mu = jnp.mean(v, axis=-1, keepdims=True)
    var = jnp.mean(jnp.square(v - mu), axis=-1, keepdims=True)
    v = ((v - mu) * lax.rsqrt(var + EPS) * lng_ref[...] + lnb_ref[...]).astype(BF16)
    lane_head = lax.broadcasted_iota(I32, (CHUNK_A, GA), 1) // HEAD_DIM
    for c in range(tt // CHUNK_A):
        rows = slice(c * CHUNK_A, (c + 1) * CHUNK_A)
        o = _mm(ws_ref[...], v[rows])
        s = bs_ref[...]
        for h in range(GA // HEAD_DIM):
            s = s + jnp.where(lane_head == h, o[h * CHUNK_A:(h + 1) * CHUNK_A], 0.0)
        ya_ref[rows, :] = (u[rows] * s).astype(ya_ref.dtype)

    conv = conv_ref[...]
    trow = lax.broadcasted_iota(I32, (tt, 1), 0)
    z_prev = jnp.where(trow == 0, row_prev, pltpu.roll(zb, 1, axis=0))
    z_next = jnp.where(trow == tt - 1, row_next, pltpu.roll(zb, tt - 1, axis=0))
    zc = z_prev * conv[0:1] + zb * conv[1:2] + z_next * conv[2:3]
    r = zc[:, :GB]
    k = zc[:, GB:2 * GB]
    vv = zc[:, 2 * GB:3 * GB]
    o0 = 3 * GB
    xw = jnp.tanh(zc[:, o0:o0 + 2 * LORA]).astype(BF16)
    xa = zc[:, o0 + 2 * LORA:o0 + 4 * LORA].astype(BF16)
    xg = jax.nn.sigmoid(zc[:, o0 + 4 * LORA:]).astype(BF16)
    bd = bd_ref[...]
    kk = k * kk_ref[...]
    kk = kk / jnp.maximum(jnp.sqrt(_segsum(kk * kk, bd)), 1e-12)
    ka = ka_ref[...]
    lw_refs = (lw0_ref, lw1_ref)
    k_refs = (k0_ref, k1_ref)
    b_refs = (b0_ref, b1_ref)
    ksum = None
    for d in range(2):
        yw = w0_ref[d:d + 1, :] + _mm(xw, w2_ref[d])
        lw = -DECAY_SCALE * jax.nn.sigmoid(yw)
        a = jax.nn.sigmoid(a0_ref[d:d + 1, :] + _mm(xa, a2_ref[d]))
        kd = k * (1.0 + (a - 1.0) * ka)
        bb = kk * a
        ksum = kd if ksum is None else ksum + kd
        for p in range(N_PAIR):
            ls = slice(p * LANES, (p + 1) * LANES)
            lw_refs[d][p] = lw[:, ls]
            k_refs[d][p] = kd[:, ls].astype(k0_ref.dtype)
            b_refs[d][p] = bb[:, ls].astype(b0_ref.dtype)
    for p in range(N_PAIR):
        ls = slice(p * LANES, (p + 1) * LANES)
        r_ref[p] = r[:, ls].astype(r_ref.dtype)
        v_ref[p] = vv[:, ls].astype(v_ref.dtype)
        kn_ref[p] = kk[:, ls].astype(kn_ref.dtype)
    g_ref[...] = _mm(xg, g2_ref[...])
    bonus_ref[...] = _segsum(r * ksum * rk_ref[...], bd) * vv


def _inproj(x, w):
    bsz, t, _ = x.shape
    tt = min(TILE_IN, t)
    nt = t // tt
    t8 = tt // 8

    def full(a):
        nd = a.ndim
        return pl.BlockSpec(a.shape, lambda b, i: (0,) * nd)

    consts = (w["norm_mix"], w["w_in"], w["a_ln_g"], w["a_ln_b"], w["a_ws"], w["a_bs"], w["b_conv"],
              w["b_w0"], w["b_w2"], w["b_a0"], w["b_a2"], w["b_g2"], w["b_kk"], w["b_ka"], w["b_rk"],
              w["bd"])
    in_specs = [
        pl.BlockSpec((None, tt, D_MODEL), lambda b, i: (b, i, 0)),
        pl.BlockSpec((None, 8, D_MODEL), lambda b, i: (b, jnp.maximum(i * t8 - 1, 0), 0)),
        pl.BlockSpec((None, 8, D_MODEL), lambda b, i: (b, jnp.minimum((i + 1) * t8, t // 8 - 1), 0)),
    ] + [full(a) for a in consts]
    pair = lambda dt: jax.ShapeDtypeStruct((bsz, N_PAIR, t, LANES), dt)
    flat = lambda dt: jax.ShapeDtypeStruct((bsz, t, GB), dt)
    pair_spec = pl.BlockSpec((None, N_PAIR, tt, LANES), lambda b, i: (b, 0, i, 0))
    flat_spec = pl.BlockSpec((None, tt, GB), lambda b, i: (b, i, 0))
    out_shape = (flat(BF16),
                 pair(BF16), pair(BF16), pair(BF16),
                 pair(F32), pair(F32),
                 pair(BF16), pair(BF16), pair(BF16), pair(BF16),
                 flat(F32), flat(F32))
    out_specs = (flat_spec,) + (pair_spec,) * 9 + (flat_spec, flat_spec)
    return pl.pallas_call(
        _inproj_kernel,
        grid=(bsz, nt),
        in_specs=in_specs,
        out_specs=out_specs,
        out_shape=out_shape,
        compiler_params=pltpu.CompilerParams(
            dimension_semantics=("parallel", "parallel"), vmem_limit_bytes=VMEM_LIMIT),
        name="inproj",
    )(x, x, x, *consts)


def _wkv_prep(p, refs, reverse, shared):
    r_ref, v_ref, kn_ref, k_ref, b_ref, lw_ref = refs
    row, col, m64 = shared
    strict = m64 & ((col > row) if reverse else (col < row))
    incl = m64 & ((col >= row) if reverse else (col <= row))

    lw = lw_ref[p]
    l1, l2 = _split2(lw)
    tri = jnp.where(incl, 1.0, 0.0).astype(BF16)
    cum = _mm(tri, l1) + _mm(tri, l2)
    nc = WKV_STEP // WKV_CHUNK
    end = 0 if reverse else WKV_CHUNK - 1
    tot = jnp.concatenate(
        [jnp.broadcast_to(cum[ck * WKV_CHUNK + end:ck * WKV_CHUNK + end + 1], (WKV_CHUNK, LANES))
         for ck in range(nc)], axis=0)
    rr = r_ref[p].astype(F32)
    kn = kn_ref[p].astype(F32)
    kd = k_ref[p].astype(F32)
    bb = b_ref[p].astype(F32)
    winv = jnp.exp(-cum)
    wd = jnp.exp(tot - cum)
    return dict(
        strict=strict, incl=incl, wtot=jnp.exp(tot),
        rt=(rr * jnp.exp(cum)).astype(BF16), at=(-kn * jnp.exp(cum - lw)).astype(BF16),
        bt=(bb * winv).astype(BF16), kt=(kd * winv).astype(BF16),
        bh=(bb * wd).astype(BF16), kh=(kd * wd).astype(BF16), vb=v_ref[p].astype(BF16))


def _wkv_chains(chains, s_scr, masks):
    c8, c16, c32, eye_c, eye_h, blk = masks
    n = WKV_STEP
    nc = n // WKV_CHUNK
    dot_nt = lambda a, b: lax.dot_general(a, b, NT, preferred_element_type=F32)
    dot_tn = lambda a, b: lax.dot_general(a, b, TN, preferred_element_type=F32)
    each = lambda fn: [fn(c) for c in chains]

    for c in chains:
        sl = slice(c["hh"] * HEAD_DIM, (c["hh"] + 1) * HEAD_DIM)
        pr = c["prep"]
        c.update(sl=sl, a=pr["at"][:, sl], r=pr["rt"][:, sl], b=pr["bt"][:, sl], k=pr["kt"][:, sl],
                 v=pr["vb"][:, sl], strict=pr["strict"], incl=pr["incl"])
    half = n // 2
    halves = (slice(0, half), slice(half, n))

    def chunk_prods(xk, yk, mk):
        return [[jnp.where(c[mk][:half, :half], dot_nt(c[xk][h], c[yk][h]), 0.0) for h in halves] for c in chains]

    def full_bf16(blocks):
        z = jnp.zeros((half, half), BF16)
        b0, b1 = (blk_.astype(BF16) for blk_ in blocks)
        return jnp.concatenate([jnp.concatenate([b0, z], axis=1), jnp.concatenate([z, b1], axis=1)], axis=0)

    lab2 = chunk_prods("a", "b", "strict")
    lak = [full_bf16(x) for x in chunk_prods("a", "k", "strict")]
    mrb = [full_bf16(x) for x in chunk_prods("r", "b", "incl")]
    mrk = [full_bf16(x) for x in chunk_prods("r", "k", "incl")]

    lab_c = [jnp.concatenate([x[:WKV_CHUNK] + x[WKV_CHUNK:] for x in pair], axis=1) for pair in lab2]
    bdiag = lambda zb: jnp.concatenate([zb] * nc, axis=0) * blk
    l8 = [jnp.where(c8, l, 0.0) for l in lab_c]
    p1 = [l.astype(BF16) for l in l8]
    p2 = [_mm(q, bdiag(q)).astype(BF16) for q in p1]
    p2d = [bdiag(q) for q in p2]
    x = [eye_c + l for l in l8]
    p4d = [bdiag(_mm(q, qd).astype(BF16)) for q, qd in zip(p2, p2d)]
    x = [xi + _mm(xi.astype(BF16), qd) for xi, qd in zip(x, p2d)]
    x = [xi + _mm(xi.astype(BF16), qd) for xi, qd in zip(x, p4d)]
    prev = c8
    for cur in (c16, c32, None):
        lvl = jnp.logical_not(prev) if cur is None else cur & jnp.logical_not(prev)
        xb = [xi.astype(BF16) for xi in x]
        t = [_mm(jnp.where(lvl, l, 0.0).astype(BF16), bdiag(xi)).astype(BF16) for l, xi in zip(lab_c, xb)]
        x = [xi + _mm(xbi, bdiag(ti)) for xi, xbi, ti in zip(x, xb, t)]
        prev = cur
    tb = [bdiag(xi.astype(BF16)) for xi in x]

    x1 = [_mm(l, c["v"]).astype(BF16) for l, c in zip(lak, chains)]
    wu = [_mm(t, c["a"]).astype(BF16) for t, c in zip(tb, chains)]
    uv = [_mm(t, xi).astype(BF16) for t, xi in zip(tb, x1)]
    q = [(c["r"].astype(F32) + _mm(m, w)).astype(BF16) for c, m, w in zip(chains, mrb, wu)]
    yl = [_mm(m, u) + _mm(mk, c["v"]) for m, u, mk, c in zip(mrb, uv, mrk, chains)]

    g, h = [], []
    for i, c in enumerate(chains):
        gi, hi = [], []
        for ck in range(nc):
            rs = slice(ck * WKV_CHUNK, (ck + 1) * WKV_CHUNK)
            b_c = c["prep"]["bh"][rs, c["sl"]]
            k_c = c["prep"]["kh"][rs, c["sl"]]
            wrow = c["prep"]["wtot"][ck * WKV_CHUNK:ck * WKV_CHUNK + 1, c["sl"]]
            gi.append((dot_tn(b_c, wu[i][rs])
                       + jnp.where(eye_h, jnp.broadcast_to(wrow, (HEAD_DIM, HEAD_DIM)), 0.0)).astype(BF16))
            hi.append(dot_tn(b_c, uv[i][rs]) + dot_tn(k_c, c["v"][rs]))
        g.append(gi)
        h.append(hi)

    s = [s_scr[c["d"], 2 * c["p"] + c["hh"]] for c in chains]
    for step in range(nc):
        for i, c in enumerate(chains):
            ck = nc - 1 - step if c["reverse"] else step
            rs = slice(ck * WKV_CHUNK, (ck + 1) * WKV_CHUNK)
            sb = s[i].astype(BF16)
            c["y_ref"][c["p"], rs, c["sl"]] = _mm(q[i][rs], sb) + yl[i][rs]
            s[i] = _mm(g[i][ck], sb) + h[i][ck]
    for i, c in enumerate(chains):
        s_scr[c["d"], 2 * c["p"] + c["hh"]] = s[i]


def _wkv_kernel(rf_ref, vf_ref, nf_ref, kf_ref, bf_ref, lf_ref,
                rb_ref, vb_ref, nb_ref, kb_ref, bb_ref, lb_ref, yf_ref, yb_ref, s_scr):
    n = WKV_STEP

    @pl.when(pl.program_id(1) == 0)
    def _():
        s_scr[...] = jnp.zeros_like(s_scr)

    def pair(p, carry):
        row = lax.broadcasted_iota(I32, (n, n), 0)
        col = lax.broadcasted_iota(I32, (n, n), 1)
        m64 = (row >> 6) == (col >> 6)
        er = lax.broadcasted_iota(I32, (HEAD_DIM, HEAD_DIM), 0)
        ec = lax.broadcasted_iota(I32, (HEAD_DIM, HEAD_DIM), 1)
        shared = (row, col, m64)
        rc = lax.broadcasted_iota(I32, (WKV_CHUNK, n), 0)
        lc = lax.broadcasted_iota(I32, (WKV_CHUNK, n), 1)
        li = lc & (WKV_CHUNK - 1)
        samec = lambda log2: (rc >> log2) == (li >> log2)
        masks = (samec(3), samec(4), samec(5), jnp.where(rc == li, 1.0, 0.0).astype(F32), er == ec,
                 jnp.where(m64, 1.0, 0.0).astype(BF16))
        chains = []
        for pp in (2 * p, 2 * p + 1):
            fwd = _wkv_prep(pp, (rf_ref, vf_ref, nf_ref, kf_ref, bf_ref, lf_ref), False, shared)
            bwd = _wkv_prep(pp, (rb_ref, vb_ref, nb_ref, kb_ref, bb_ref, lb_ref), True, shared)
            chains += [dict(prep=pr, hh=hh, d=d, p=pp, reverse=rev, y_ref=y_ref)
                       for pr, d, rev, y_ref in ((fwd, 0, False, yf_ref), (bwd, 1, True, yb_ref))
                       for hh in range(2)]
        _wkv_chains(chains, s_scr, masks)
        return carry

    lax.fori_loop(0, N_PAIR // 2, pair, 0)


def _wkv(r, v, kn, k0, b0, lw0, k1, b1, lw1):
    bsz, _, t, _ = r.shape
    assert t % WKV_STEP == 0
    nj = t // WKV_STEP
    fwd = pl.BlockSpec((None, N_PAIR, WKV_STEP, LANES), lambda bi, j: (bi, 0, j, 0))
    bwd = pl.BlockSpec((None, N_PAIR, WKV_STEP, LANES), lambda bi, j: (bi, 0, nj - 1 - j, 0))
    out = jax.ShapeDtypeStruct((bsz, N_PAIR, t, LANES), F32)
    return pl.pallas_call(
        _wkv_kernel,
        grid=(bsz, nj),
        in_specs=[fwd] * 6 + [bwd] * 6,
        out_specs=(fwd, bwd),
        out_shape=(out, out),
        scratch_shapes=[pltpu.VMEM((2, 2 * N_PAIR, HEAD_DIM, HEAD_DIM), F32)],
        compiler_params=pltpu.CompilerParams(
            dimension_semantics=("parallel", "arbitrary"), vmem_limit_bytes=VMEM_LIMIT),
        name="wkv",
    )(r, v, kn, k0, b0, lw0, r, v, kn, k1, b1, lw1)


def _post_kernel(yf_ref, yb_ref, bonus_ref, g_ref, ya_ref, x_ref, gng_ref, gnb_ref, bd_ref, wout_ref,
                 nf_ref, rwh_ref, rwl_ref, rb_ref,
                 h_ref, xn_ref, ids_ref, gate_ref, rank_ref, cnt_ref, before_ref, run_scr):
    tt = x_ref.shape[0]
    step = pl.program_id(0)

    @pl.when(step == 0)
    def _():
        run_scr[...] = jnp.zeros_like(run_scr)

    y = jnp.concatenate([yf_ref[p] + yb_ref[p] for p in range(N_PAIR)], axis=1)
    bd = bd_ref[...]
    inv = 1.0 / HEAD_DIM
    mu = _segsum(y, bd) * inv
    yc = y - mu
    var = _segsum(yc * yc, bd) * inv
    yn = yc * lax.rsqrt(var + GN_EPS) * gng_ref[...] + gnb_ref[...] + bonus_ref[...]
    ybm = (yn * g_ref[...]).astype(BF16)
    h = x_ref[...] + _mm(ya_ref[...], wout_ref[:GA, :]) + _mm(ybm, wout_ref[GA:, :])
    h_ref[...] = h
    xn = _rms(h, nf_ref[...])
    xn_ref[...] = xn

    lane = lax.broadcasted_iota(I32, (tt, LANES), 1)
    xh, xl = _split2(xn)
    logits = _mm(xh, rwh_ref[...]) + _mm(xl, rwh_ref[...]) + _mm(xh, rwl_ref[...])
    logits = jnp.where(lane < N_EXPERTS, logits + rb_ref[...], -jnp.inf)
    vals, ids, sel = [], [], jnp.zeros((tt, LANES), F32)
    cur = logits
    for _ in range(TOP_K):
        m = jnp.max(cur, axis=-1, keepdims=True)
        idx = jnp.min(jnp.where(cur == m, lane, LANES), axis=-1, keepdims=True)
        hit = lane == idx
        vals.append(m)
        ids.append(idx)
        sel = sel + jnp.where(hit, 1.0, 0.0)
        cur = jnp.where(hit, -jnp.inf, cur)
    es = [jnp.exp(vv - vals[0]) for vv in vals]
    den = es[0] + es[1] + es[2] + es[3]

    rt = lax.broadcasted_iota(I32, (tt, tt), 0)
    ct = lax.broadcasted_iota(I32, (tt, tt), 1)
    before = jnp.where(ct < rt, 1.0, 0.0).astype(BF16)
    before_ref[...] = run_scr[...]
    cnt = _mm(before, sel.astype(BF16)) + run_scr[...]
    ids_o = jnp.zeros((tt, LANES), I32)
    gate_o = jnp.zeros((tt, LANES), F32)
    rank_o = jnp.zeros((tt, LANES), F32)
    for kx in range(TOP_K):
        rk = jnp.sum(jnp.where(lane == ids[kx], cnt, 0.0), axis=-1, keepdims=True)
        ids_o = jnp.where(lane == kx, ids[kx], ids_o)
        gate_o = jnp.where(lane == kx, es[kx] / den, gate_o)
        rank_o = jnp.where(lane == kx, rk, rank_o)
    ids_ref[...] = ids_o
    gate_ref[...] = gate_o
    rank_ref[...] = rank_o.astype(I32)
    run = run_scr[...] + jnp.sum(sel, axis=0, keepdims=True)
    run_scr[...] = run
    cnt_ref[...] = run


def _post(yf, yb, bonus, g, ya, x, w):
    bsz, t, _ = x.shape
    n = bsz * t
    tt = min(TILE_POST, t)
    nt = t // tt

    def full(a):
        nd = a.ndim
        return pl.BlockSpec(a.shape, lambda s: (0,) * nd)

    consts = (w["b_gn_g"], w["b_gn_b"], w["bd"], w["w_out"], w["norm_ffn"], w["router_wh"], w["router_wl"],
              w["router_b"])
    pair_spec = pl.BlockSpec((None, N_PAIR, tt, LANES), lambda s: (s // nt, 0, s % nt, 0))
    tok3 = lambda width: pl.BlockSpec((None, tt, width), lambda s: (s // nt, s % nt, 0))
    tok2 = lambda width: pl.BlockSpec((tt, width), lambda s: (s, 0))
    in_specs = [pair_spec, pair_spec, tok3(GB), tok3(GB), tok3(GA), tok3(D_MODEL)] + [full(a) for a in consts]
    out_shape = (jax.ShapeDtypeStruct((n, D_MODEL), F32),
                 jax.ShapeDtypeStruct((n, D_MODEL), F32),
                 jax.ShapeDtypeStruct((n, LANES), I32),
                 jax.ShapeDtypeStruct((n, LANES), F32),
                 jax.ShapeDtypeStruct((n, LANES), I32),
                 jax.ShapeDtypeStruct((1, LANES), F32),
                 jax.ShapeDtypeStruct((bsz * nt, 1, LANES), F32))
    out_specs = (tok2(D_MODEL), tok2(D_MODEL), tok2(LANES), tok2(LANES), tok2(LANES),
                 pl.BlockSpec((1, LANES), lambda s: (0, 0)),
                 pl.BlockSpec((None, 1, LANES), lambda s: (s, 0, 0)))
    return pl.pallas_call(
        _post_kernel,
        grid=(bsz * nt,),
        in_specs=in_specs,
        out_specs=out_specs,
        out_shape=out_shape,
        scratch_shapes=[pltpu.VMEM((1, LANES), F32)],
        compiler_params=pltpu.CompilerParams(
            dimension_semantics=("arbitrary",), vmem_limit_bytes=VMEM_LIMIT),
        name="post_router",
    )(yf, yb, bonus, g, ya, x, *consts)


RUN_ALIGN = 8
RUN_SIZES = tuple(1 << i for i in range(8, 2, -1))
TILE_SORT = TILE_ROW * TOP_K + N_EXPERTS * RUN_ALIGN


def _tile_runs(sc_ref, copy, wait):
    def per_expert(e, c):
        o = sc_ref[0, e]
        d = sc_ref[0, N_EXPERTS + e]
        ln = sc_ref[0, 2 * N_EXPERTS + e]
        for size in RUN_SIZES:
            bit = ln & size

            @pl.when(bit != 0)
            def _():
                cp = copy(pl.multiple_of(o, RUN_ALIGN), pl.multiple_of(d, RUN_ALIGN), size)
                if wait:
                    cp.wait()
                else:
                    cp.start()

            o = o + bit
            d = d + bit
        return c

    lax.fori_loop(0, N_EXPERTS, per_expert, 0)


def _slot_matrix(slot, vals, width):
    tt = slot.shape[0]
    lane = lax.broadcasted_iota(I32, (tt, width), 1)
    m = jnp.zeros((tt, width), F32)
    for kx in range(TOP_K):
        m = m + jnp.where(lane == slot[:, kx:kx + 1], vals[kx], 0.0)
    return m


def _scatter_kernel(sc_ref, sp_ref, zt_ref, na_ref, slot_ref, xn_ref, xs_ref, sb, sem):
    s = pl.program_id(0)
    ns = pl.num_programs(0)
    cur = s % 2
    n_blocks = xs_ref.shape[0] // MOE_BLOCK

    def runs(tab_ref, sl, wait):
        _tile_runs(tab_ref, lambda o, d, size: pltpu.make_async_copy(
            sb.at[sl, pl.ds(o, size)], xs_ref.at[pl.ds(d, size)], sem.at[sl]), wait)

    def blocks(wait):
        def body(b, c):
            cp = pltpu.make_async_copy(sb.at[0, pl.ds(0, MOE_BLOCK)],
                                       xs_ref.at[pl.ds(pl.multiple_of(b * MOE_BLOCK, MOE_BLOCK), MOE_BLOCK)],
                                       sem.at[0])
            if wait:
                cp.wait()
            else:
                cp.start()
            return c
        lax.fori_loop(na_ref[0], n_blocks, body, 0)

    @pl.when(s == 0)
    def _():
        sb[0] = jnp.zeros(sb.shape[1:], F32)
        runs(zt_ref, 0, False)
        blocks(False)
        runs(zt_ref, 0, True)
        blocks(True)

    p01 = _slot_matrix(slot_ref[...], (1.0,) * TOP_K, TILE_SORT).astype(BF16)
    sb[cur] = lax.dot_general(p01, xn_ref[...].astype(BF16), TN, preferred_element_type=F32)
    runs(sc_ref, cur, False)

    @pl.when(s > 0)
    def _():
        runs(sp_ref, 1 - cur, True)

    @pl.when(s == ns - 1)
    def _():
        runs(sc_ref, cur, True)


def _scatter_rows(xn, slot, runs, ztab, n_active, rows):
    n = xn.shape[0]
    tt = TILE_ROW
    smem = lambda fn: pl.BlockSpec((None, 1, 3 * N_EXPERTS), fn, memory_space=pltpu.SMEM)
    return pl.pallas_call(
        _scatter_kernel,
        grid=(n // tt,),
        in_specs=[smem(lambda s: (s, 0, 0)),
                  smem(lambda s: (jnp.maximum(s - 1, 0), 0, 0)),
                  pl.BlockSpec(memory_space=pltpu.SMEM),
                  pl.BlockSpec(memory_space=pltpu.SMEM),
                  pl.BlockSpec((tt, LANES), lambda s: (s, 0)),
                  pl.BlockSpec((tt, D_MODEL), lambda s: (s, 0))],
        out_specs=pl.BlockSpec(memory_space=pl.ANY),
        out_shape=jax.ShapeDtypeStruct((rows, D_MODEL), F32),
        scratch_shapes=[pltpu.VMEM((2, TILE_SORT, D_MODEL), F32), pltpu.SemaphoreType.DMA((2,))],
        compiler_params=pltpu.CompilerParams(
            dimension_semantics=("arbitrary",), vmem_limit_bytes=VMEM_LIMIT),
        name="moe_scatter",
    )(runs, runs, ztab, n_active, slot, xn)


def _expert_kernel(be_ref, na_ref, xs_ref, w1_ref, b1_ref, w2_ref, b2_ref, o_ref):
    del be_ref
    s = pl.program_id(0)

    @pl.when(s < na_ref[0])
    def _():
        xb = xs_ref[...].astype(BF16)
        hdn = _mm(xb, w1_ref[...]) + b1_ref[...]
        glu = jnp.minimum(hdn[:, :D_FF], SWIGLU_LIMIT)
        lin = jnp.clip(hdn[:, D_FF:], -SWIGLU_LIMIT, SWIGLU_LIMIT)
        act = glu * jax.nn.sigmoid(SWIGLU_ALPHA * glu) * (lin + 1.0)
        o_ref[...] = _mm(act.astype(BF16), w2_ref[...]) + b2_ref[...]

    @pl.when(s >= na_ref[0])
    def _():
        o_ref[...] = jnp.zeros_like(o_ref)


def _experts(xs, block_e, n_active, w):
    rows = xs.shape[0]
    nb = rows // MOE_BLOCK
    grid_spec = pltpu.PrefetchScalarGridSpec(
        num_scalar_prefetch=2,
        grid=(nb,),
        in_specs=[
            pl.BlockSpec((MOE_BLOCK, D_MODEL), lambda s, be, na: (s, 0)),
            pl.BlockSpec((None, D_MODEL, 2 * D_FF), lambda s, be, na: (be[s], 0, 0)),
            pl.BlockSpec((None, 1, 2 * D_FF), lambda s, be, na: (be[s], 0, 0)),
            pl.BlockSpec((None, D_FF, D_MODEL), lambda s, be, na: (be[s], 0, 0)),
            pl.BlockSpec((None, 1, D_MODEL), lambda s, be, na: (be[s], 0, 0)),
        ],
        out_specs=pl.BlockSpec((MOE_BLOCK, D_MODEL), lambda s, be, na: (s, 0)),
    )
    return pl.pallas_call(
        _expert_kernel,
        grid_spec=grid_spec,
        out_shape=jax.ShapeDtypeStruct((rows, D_MODEL), F32),
        compiler_params=pltpu.CompilerParams(
            dimension_semantics=("arbitrary",), vmem_limit_bytes=VMEM_LIMIT),
        name="moe_experts",
    )(block_e, n_active, xs, w["moe_w1"], w["moe_b1"], w["moe_w2"], w["moe_b2"])


def _combine_kernel(scur_ref, snxt_ref, slot_ref, h_ref, gate_ref, p_ref, npl_ref, pg_ref, pp_ref, nfin_ref,
                    os_ref, y_ref, gb, sem):
    s = pl.program_id(0)
    ns = pl.num_programs(0)
    slot = s % 2

    def runs(sc_ref, sl, wait):
        _tile_runs(sc_ref, lambda o, d, size: pltpu.make_async_copy(
            os_ref.at[pl.ds(d, size)], gb.at[sl, pl.ds(o, size)], sem.at[sl]), wait)

    @pl.when(s == 0)
    def _():
        gb[...] = jnp.zeros_like(gb)
        runs(scur_ref, 0, False)

    @pl.when(s + 1 < ns)
    def _():
        runs(snxt_ref, 1 - slot, False)

    runs(scur_ref, slot, True)

    gate = gate_ref[...]
    pm = _slot_matrix(slot_ref[...], [gate[:, kx:kx + 1] for kx in range(TOP_K)], TILE_SORT)
    tt = pm.shape[0]
    moe = _mm(jnp.concatenate(_split2(pm), axis=0), gb[slot].astype(BF16))
    h = h_ref[...] + moe[:tt] + moe[tt:]
    gt = jax.nn.sigmoid(_mm(_rms(h, npl_ref[...]).astype(BF16), pg_ref[...]))
    h = h + _mm(p_ref[...].astype(BF16), pp_ref[...]) * gt
    y_ref[...] = _rms(h, nfin_ref[...])


def _combine(h, gates, slot, runs, p, os_rows, w):
    n = h.shape[0]
    tt = TILE_ROW
    ns = n // tt

    def full(a):
        nd = a.ndim
        return pl.BlockSpec(a.shape, lambda s: (0,) * nd)

    consts = (w["norm_ple"], w["ple_gate"], w["ple_proj"], w["norm_final"])
    smem = lambda fn: pl.BlockSpec((None, 1, 3 * N_EXPERTS), fn, memory_space=pltpu.SMEM)
    return pl.pallas_call(
        _combine_kernel,
        grid=(ns,),
        in_specs=[smem(lambda s: (s, 0, 0)),
                  smem(lambda s: (jnp.minimum(s + 1, ns - 1), 0, 0)),
                  pl.BlockSpec((tt, LANES), lambda s: (s, 0)),
                  pl.BlockSpec((tt, D_MODEL), lambda s: (s, 0)),
                  pl.BlockSpec((tt, LANES), lambda s: (s, 0)),
                  pl.BlockSpec((tt, PLE_DIM), lambda s: (s, 0))]
        + [full(a) for a in consts]
        + [pl.BlockSpec(memory_space=pl.ANY)],
        out_specs=pl.BlockSpec((tt, D_MODEL), lambda s: (s, 0)),
        out_shape=jax.ShapeDtypeStruct((n, D_MODEL), F32),
        scratch_shapes=[pltpu.VMEM((2, TILE_SORT, D_MODEL), F32), pltpu.SemaphoreType.DMA((2,))],
        compiler_params=pltpu.CompilerParams(
            dimension_semantics=("arbitrary",), vmem_limit_bytes=VMEM_LIMIT),
        name="moe_combine",
    )(runs, runs, slot, h, gates, p, *consts, os_rows)


def _prep_weights(norm_mix, w_in, a_ln_g, a_ln_b, a_ws, a_bs, b_conv, b_w0, b_w2, b_a0, b_a2, b_g2,
                  b_kk, b_ka, b_rk, b_gn_g, b_gn_b, w_out, norm_ffn, router_w, router_b, moe_w1,
                  moe_b1, moe_w2, moe_b2, norm_ple, ple_proj, ple_gate, norm_final):
    row = lambda a: a.reshape(1, -1).astype(F32)

    def lora_pad(m):
        z = jnp.zeros((2, 2 * LORA, GB), F32)
        z = z.at[0, :LORA].set(m[0]).at[1, LORA:].set(m[1])
        return z.astype(BF16)

    seg = jnp.arange(GB, dtype=I32) // HEAD_DIM
    rw = jnp.pad(router_w[0].astype(F32), ((0, 0), (0, LANES - N_EXPERTS)))
    rw_hi = rw.astype(BF16)
    return {
        "norm_mix": row(norm_mix[0]),
        "w_in": w_in[0].astype(BF16),
        "a_ln_g": row(a_ln_g[0]),
        "a_ln_b": row(a_ln_b[0]),
        "a_ws": a_ws[0].reshape(-1, CHUNK_A).astype(BF16),
        "a_bs": jnp.repeat(a_bs[0].T.astype(F32), HEAD_DIM, axis=1),
        "b_conv": b_conv[0].astype(F32),
        "b_w0": b_w0[0].astype(F32),
        "b_w2": lora_pad(b_w2[0]),
        "b_a0": b_a0[0].astype(F32),
        "b_a2": lora_pad(b_a2[0]),
        "b_g2": b_g2[0].astype(BF16),
        "b_kk": row(b_kk[0]),
        "b_ka": row(b_ka[0]),
        "b_rk": row(b_rk[0]),
        "bd": (seg[:, None] == seg[None, :]).astype(BF16),
        "b_gn_g": row(b_gn_g[0]),
        "b_gn_b": row(b_gn_b[0]),
        "w_out": w_out[0].astype(BF16),
        "norm_ffn": row(norm_ffn[0]),
        "router_wh": rw_hi,
        "router_wl": (rw - rw_hi.astype(F32)).astype(BF16),
        "router_b": jnp.pad(router_b[0].astype(F32), (0, LANES - N_EXPERTS)).reshape(1, LANES),
        "moe_w1": moe_w1[0].astype(BF16),
        "moe_b1": moe_b1[0].astype(F32).reshape(N_EXPERTS, 1, 2 * D_FF),
        "moe_w2": moe_w2[0].astype(BF16),
        "moe_b2": moe_b2[0].astype(F32).reshape(N_EXPERTS, 1, D_MODEL),
        "norm_ple": row(norm_ple[0]),
        "ple_gate": ple_gate[0].astype(BF16),
        "ple_proj": ple_proj[0].astype(BF16),
        "norm_final": row(norm_final),
    }


def _forward(x, p, w):
    bsz, t, _ = x.shape
    n = bsz * t
    ya, r, v, kn, lw0, lw1, k0, k1, b0, b1, g, bonus = _inproj(x, w)
    yf, yb = _wkv(r, v, kn, k0, b0, lw0, k1, b1, lw1)
    h, xn, ids, gates, rank, counts, before = _post(yf, yb, bonus, g, ya, x, w)

    nt = n // TILE_ROW
    counts = counts[0, :N_EXPERTS].astype(I32)
    before = before[:, 0, :N_EXPERTS].astype(I32)
    tile_cnt = jnp.concatenate([before[1:], counts[None]], axis=0) - before
    tile_pad = (tile_cnt + RUN_ALIGN - 1) // RUN_ALIGN * RUN_ALIGN
    tile_off = jnp.cumsum(tile_pad, axis=1) - tile_pad
    before_pad = jnp.cumsum(tile_pad, axis=0) - tile_pad
    padded = (jnp.sum(tile_pad, axis=0) + MOE_BLOCK - 1) // MOE_BLOCK * MOE_BLOCK
    pends = jnp.cumsum(padded)
    pstarts = pends - padded
    n_blocks = -(-(n * TOP_K + nt * N_EXPERTS * (RUN_ALIGN - 1)) // MOE_BLOCK) + N_EXPERTS
    block_start = jnp.arange(n_blocks, dtype=I32) * MOE_BLOCK
    block_e = jnp.minimum(jnp.sum(pends[None, :] <= block_start[:, None], axis=1), N_EXPERTS - 1).astype(I32)
    n_active = (pends[-1:] // MOE_BLOCK).astype(I32)
    runs = jnp.concatenate([tile_off, pstarts[None, :] + before_pad, tile_pad], axis=1)
    runs = runs.reshape(nt, 1, 3 * N_EXPERTS).astype(I32)
    rel = jnp.repeat(tile_off - before, TILE_ROW, axis=0)
    slot = jnp.take_along_axis(rel, ids[:, :TOP_K], axis=1) + rank[:, :TOP_K]
    slot = jnp.pad(slot, ((0, 0), (0, LANES - TOP_K)), constant_values=-1).astype(I32)

    rows_used = jnp.sum(tile_pad, axis=0)
    ztab = jnp.concatenate([jnp.zeros_like(padded), pstarts + rows_used, padded - rows_used])
    ztab = ztab.reshape(1, 3 * N_EXPERTS).astype(I32)

    xs = _scatter_rows(xn, slot, runs, ztab, n_active, n_blocks * MOE_BLOCK)
    os_rows = _experts(xs, block_e, n_active, w)
    y = _combine(h, gates, slot, runs, p.reshape(n, PLE_DIM), os_rows, w)
    return y.reshape(bsz, t, D_MODEL)


def kernel(x_prompt, x_sample, p_prompt, p_sample, norm_mix, w_in, a_ln_g, a_ln_b, a_ws, a_bs, b_conv, b_w0, b_w2, b_a0, b_a2, b_g2, b_kk, b_ka, b_rk, b_gn_g, b_gn_b, w_out, norm_ffn, router_w, router_b, moe_w1, moe_b1, moe_w2, moe_b2, norm_ple, ple_proj, ple_gate, norm_final):
    assert norm_mix.shape[0] == 1, "single-layer trunk"
    w = _prep_weights(norm_mix, w_in, a_ln_g, a_ln_b, a_ws, a_bs, b_conv, b_w0, b_w2, b_a0, b_a2, b_g2,
                      b_kk, b_ka, b_rk, b_gn_g, b_gn_b, w_out, norm_ffn, router_w, router_b, moe_w1,
                      moe_b1, moe_w2, moe_b2, norm_ple, ple_proj, ple_gate, norm_final)
    y_prompt = _forward(x_prompt, p_prompt[0], w)
    y_sample = _forward(x_sample, p_sample[0], w)
    return (y_prompt, y_sample)
```

```python
import math

import jax
import jax.numpy as jnp
from jax import lax
from jax.experimental import pallas as pl
from jax.experimental.pallas import tpu as pltpu

F32 = jnp.float32
BF16 = jnp.bfloat16
I32 = jnp.int32
U32 = jnp.uint32

D_MODEL = 1024
D_PACK = D_MODEL // 2
HEAD_DIM = 64
GA = 512
GB = 512
N_PAIR = GB // 128
CHUNK_A = 128
LORA = 64
LORA_G = 128
B_CONV = 3 * GB + 4 * LORA + LORA_G
N_EXPERTS = 32
TOP_K = 4
D_FF = 1024
PLE_DIM = 256
SWIGLU_ALPHA = 1.702
SWIGLU_LIMIT = 7.0
EPS = 1e-6
GN_EPS = 64e-5
DECAY_SCALE = math.exp(-0.5)

LANES = 128
TILE_IN = 256
WKV_STEP = 256
WKV_CHUNK = 64
TILE_POST = 256
TILE_ROW = 256
MOE_BLOCK = 512
VMEM_LIMIT = 56 * 1024 * 1024

NT = (((1,), (1,)), ((), ()))
TN = (((0,), (0,)), ((), ()))


def _mm(a, b):
    return jnp.dot(a, b, preferred_element_type=F32)


def _split2(q):
    hi = q.astype(BF16)
    lo = (q - hi.astype(F32)).astype(BF16)
    return hi, lo


def _segsum(q, bd):
    hi, lo = _split2(q)
    return _mm(hi, bd) + _mm(lo, bd)


def _gelu(z):
    return 0.5 * z * (1.0 + lax.erf(z * (1.0 / math.sqrt(2.0))))


def _pack_bf16_pairs(a):
    half = a.shape[1] // 2
    u = lax.bitcast_convert_type(a.astype(BF16).astype(F32), U32)
    return (u[:, :half] & jnp.uint32(0xFFFF0000)) | (u[:, half:] >> jnp.uint32(16))


def _unpack_bf16_pairs(u):
    left = lax.bitcast_convert_type(u & jnp.uint32(0xFFFF0000), F32)
    right = lax.bitcast_convert_type(u << jnp.uint32(16), F32)
    return left.astype(BF16), right.astype(BF16)


def _rms(xv, g):
    ms = jnp.mean(xv * xv, axis=-1, keepdims=True)
    return xv * lax.rsqrt(ms + EPS) * g


def _inproj_kernel(x_ref, xp_ref, xn_ref, nm_ref, win_ref, lng_ref, lnb_ref, ws_ref, bs_ref,
                   conv_ref, w0_ref, w2_ref, a0_ref, a2_ref, g2_ref, kk_ref, ka_ref, rk_ref, bd_ref,
                   ya_ref, r_ref, v_ref, kn_ref, lw0_ref, lw1_ref, k0_ref, k1_ref, b0_ref, b1_ref,
                   g_ref, bonus_ref):
    tt = x_ref.shape[0]
    i = pl.program_id(1)
    last = pl.num_programs(1) - 1
    nm = nm_ref[...]

    xe = jnp.concatenate([x_ref[...], xp_ref[...], xn_ref[...]], axis=0)
    xe = _rms(xe, nm).astype(BF16)
    za = _mm(xe[:tt], win_ref[:, :2 * GA])
    ze = _mm(xe, win_ref[:, 2 * GA:])
    zb = ze[:tt]
    row_prev = jnp.where(i > 0, ze[tt + 7:tt + 8], 0.0)
    row_next = jnp.where(i < last, ze[tt + 8:tt + 9], 0.0)

    u = _gelu(za[:, :GA])
    v = _gelu(za[:, GA:])
    mu = jnp.mean(v, axis=-1, keepdims=True)
    var = jnp.mean(jnp.square(v - mu), axis=-1, keepdims=True)
    v = ((v - mu) * lax.rsqrt(var + EPS) * lng_ref[...] + lnb_ref[...]).astype(BF16)
    lane_head = lax.broadcasted_iota(I32, (CHUNK_A, GA), 1) // HEAD_DIM
    for c in range(tt // CHUNK_A):
        rows = slice(c * CHUNK_A, (c + 1) * CHUNK_A)
        o = _mm(ws_ref[...], v[rows])
        s = bs_ref[...]
        for h in range(GA // HEAD_DIM):
            s = s + jnp.where(lane_head == h, o[h * CHUNK_A:(h + 1) * CHUNK_A], 0.0)
        ya_ref[rows, :] = (u[rows] * s).astype(ya_ref.dtype)

    conv = conv_ref[...]
    trow = lax.broadcasted_iota(I32, (tt, 1), 0)
    z_prev = jnp.where(trow == 0, row_prev, pltpu.roll(zb, 1, axis=0))
    z_next = jnp.where(trow == tt - 1, row_next, pltpu.roll(zb, tt - 1, axis=0))
    zc = z_prev * conv[0:1] + zb * conv[1:2] + z_next * conv[2:3]
    r = zc[:, :GB]
    k = zc[:, GB:2 * GB]
    vv = zc[:, 2 * GB:3 * GB]
    o0 = 3 * GB
    xw = jnp.tanh(zc[:, o0:o0 + 2 * LORA]).astype(BF16)
    xa = zc[:, o0 + 2 * LORA:o0 + 4 * LORA].astype(BF16)
    xg = jax.nn.sigmoid(zc[:, o0 + 4 * LORA:]).astype(BF16)
    bd = bd_ref[...]
    kk = k * kk_ref[...]
    kk = kk / jnp.maximum(jnp.sqrt(_segsum(kk * kk, bd)), 1e-12)
    ka = ka_ref[...]
    lw_refs = (lw0_ref, lw1_ref)
    k_refs = (k0_ref, k1_ref)
    b_refs = (b0_ref, b1_ref)
    ksum = None
    for d in range(2):
        yw = w0_ref[d:d + 1, :] + _mm(xw, w2_ref[d])
        lw = -DECAY_SCALE * jax.nn.sigmoid(yw)
        a = jax.nn.sigmoid(a0_ref[d:d + 1, :] + _mm(xa, a2_ref[d]))
        kd = k * (1.0 + (a - 1.0) * ka)
        bb = kk * a
        ksum = kd if ksum is None else ksum + kd
        for p in range(N_PAIR):
            ls = slice(p * LANES, (p + 1) * LANES)
            lw_refs[d][p] = lw[:, ls]
            k_refs[d][p] = kd[:, ls].astype(k0_ref.dtype)
            b_refs[d][p] = bb[:, ls].astype(b0_ref.dtype)
    for p in range(N_PAIR):
        ls = slice(p * LANES, (p + 1) * LANES)
        r_ref[p] = r[:, ls].astype(r_ref.dtype)
        v_ref[p] = vv[:, ls].astype(v_ref.dtype)
        kn_ref[p] = kk[:, ls].astype(kn_ref.dtype)
    g_ref[...] = _mm(xg, g2_ref[...])
    bonus_ref[...] = _segsum(r * ksum * rk_ref[...], bd) * vv


def _inproj(x, w):
    bsz, t, _ = x.shape
    tt = min(TILE_IN, t)
    nt = t // tt
    t8 = tt // 8

    def full(a):
        nd = a.ndim
        return pl.BlockSpec(a.shape, lambda b, i: (0,) * nd)

    consts = (w["norm_mix"], w["w_in"], w["a_ln_g"], w["a_ln_b"], w["a_ws"], w["a_bs"], w["b_conv"],
              w["b_w0"], w["b_w2"], w["b_a0"], w["b_a2"], w["b_g2"], w["b_kk"], w["b_ka"], w["b_rk"],
              w["bd"])
    in_specs = [
        pl.BlockSpec((None, tt, D_MODEL), lambda b, i: (b, i, 0)),
        pl.BlockSpec((None, 8, D_MODEL), lambda b, i: (b, jnp.maximum(i * t8 - 1, 0), 0)),
        pl.BlockSpec((None, 8, D_MODEL), lambda b, i: (b, jnp.minimum((i + 1) * t8, t // 8 - 1), 0)),
    ] + [full(a) for a in consts]
    pair = lambda dt: jax.ShapeDtypeStruct((bsz, N_PAIR, t, LANES), dt)
    flat = lambda dt: jax.ShapeDtypeStruct((bsz, t, GB), dt)
    pair_spec = pl.BlockSpec((None, N_PAIR, tt, LANES), lambda b, i: (b, 0, i, 0))
    flat_spec = pl.BlockSpec((None, tt, GB), lambda b, i: (b, i, 0))
    out_shape = (flat(BF16),
                 pair(BF16), pair(BF16), pair(BF16),
                 pair(F32), pair(F32),
                 pair(BF16), pair(BF16), pair(BF16), pair(BF16),
                 flat(F32), flat(F32))
    out_specs = (flat_spec,) + (pair_spec,) * 9 + (flat_spec, flat_spec)
    return pl.pallas_call(
        _inproj_kernel,
        grid=(bsz, nt),
        in_specs=in_specs,
        out_specs=out_specs,
        out_shape=out_shape,
        compiler_params=pltpu.CompilerParams(
            dimension_semantics=("parallel", "parallel"), vmem_limit_bytes=VMEM_LIMIT),
        name="inproj",
    )(x, x, x, *consts)


def _wkv_prep(p, refs, reverse, shared):
    r_ref, v_ref, kn_ref, k_ref, b_ref, lw_ref = refs
    row, col, m64 = shared
    strict = m64 & ((col > row) if reverse else (col < row))
    incl = m64 & ((col >= row) if reverse else (col <= row))

    lw = lw_ref[p]
    l1, l2 = _split2(lw)
    tri = jnp.where(incl, 1.0, 0.0).astype(BF16)
    cum = _mm(tri, l1) + _mm(tri, l2)
    nc = WKV_STEP // WKV_CHUNK
    end = 0 if reverse else WKV_CHUNK - 1
    tot = jnp.concatenate(
        [jnp.broadcast_to(cum[ck * WKV_CHUNK + end:ck * WKV_CHUNK + end + 1], (WKV_CHUNK, LANES))
         for ck in range(nc)], axis=0)
    rr = r_ref[p].astype(F32)
    kn = kn_ref[p].astype(F32)
    kd = k_ref[p].astype(F32)
    bb = b_ref[p].astype(F32)
    winv = jnp.exp(-cum)
    wd = jnp.exp(tot - cum)
    return dict(
        strict=strict, incl=incl, wtot=jnp.exp(tot),
        rt=(rr * jnp.exp(cum)).astype(BF16), at=(-kn * jnp.exp(cum - lw)).astype(BF16),
        bt=(bb * winv).astype(BF16), kt=(kd * winv).astype(BF16),
        bh=(bb * wd).astype(BF16), kh=(kd * wd).astype(BF16), vb=v_ref[p].astype(BF16))


def _wkv_chains(chains, s_scr, masks):
    c8, c16, c32, cb, eye_c, eye_h, blk = masks
    n = WKV_STEP
    nc = n // WKV_CHUNK
    dot_nt = lambda a, b: lax.dot_general(a, b, NT, preferred_element_type=F32)
    dot_tn = lambda a, b: lax.dot_general(a, b, TN, preferred_element_type=F32)
    each = lambda fn: [fn(c) for c in chains]

    for c in chains:
        sl = slice(c["hh"] * HEAD_DIM, (c["hh"] + 1) * HEAD_DIM)
        pr = c["prep"]
        c.update(sl=sl, a=pr["at"][:, sl], r=pr["rt"][:, sl], b=pr["bt"][:, sl], k=pr["kt"][:, sl],
                 v=pr["vb"][:, sl], strict=pr["strict"], incl=pr["incl"])
    lab = each(lambda c: jnp.where(c["strict"], dot_nt(c["a"], c["b"]), 0.0))
    lak = each(lambda c: jnp.where(c["strict"], dot_nt(c["a"], c["k"]), 0.0).astype(BF16))
    mrb = each(lambda c: jnp.where(c["incl"], dot_nt(c["r"], c["b"]), 0.0).astype(BF16))
    mrk = each(lambda c: jnp.where(c["incl"], dot_nt(c["r"], c["k"]), 0.0).astype(BF16))

    lab_c = [sum(jnp.where(cb == ck, l[ck * WKV_CHUNK:(ck + 1) * WKV_CHUNK], 0.0) for ck in range(nc))
             for l in lab]
    bdiag = lambda zb: jnp.concatenate([zb] * nc, axis=0) * blk
    l8 = [jnp.where(c8, l, 0.0) for l in lab_c]
    p1 = [l.astype(BF16) for l in l8]
    p2 = [_mm(q, bdiag(q)).astype(BF16) for q in p1]
    p2d = [bdiag(q) for q in p2]
    x = [eye_c + l for l in l8]
    p4d = [bdiag(_mm(q, qd).astype(BF16)) for q, qd in zip(p2, p2d)]
    x = [xi + _mm(xi.astype(BF16), qd) for xi, qd in zip(x, p2d)]
    x = [xi + _mm(xi.astype(BF16), qd) for xi, qd in zip(x, p4d)]
    prev = c8
    for cur in (c16, c32, None):
        lvl = jnp.logical_not(prev) if cur is None else cur & jnp.logical_not(prev)
        xb = [xi.astype(BF16) for xi in x]
        t = [_mm(jnp.where(lvl, l, 0.0).astype(BF16), bdiag(xi)).astype(BF16) for l, xi in zip(lab_c, xb)]
        x = [xi + _mm(xbi, bdiag(ti)) for xi, xbi, ti in zip(x, xb, t)]
        prev = cur
    tb = [bdiag(xi.astype(BF16)) for xi in x]

    x1 = [_mm(l, c["v"]).astype(BF16) for l, c in zip(lak, chains)]
    wu = [_mm(t, c["a"]).astype(BF16) for t, c in zip(tb, chains)]
    uv = [_mm(t, xi).astype(BF16) for t, xi in zip(tb, x1)]
    q = [(c["r"].astype(F32) + _mm(m, w)).astype(BF16) for c, m, w in zip(chains, mrb, wu)]
    yl = [_mm(m, u) + _mm(mk, c["v"]) for m, u, mk, c in zip(mrb, uv, mrk, chains)]

    g, h = [], []
    for i, c in enumerate(chains):
        gi, hi = [], []
        for ck in range(nc):
            rs = slice(ck * WKV_CHUNK, (ck + 1) * WKV_CHUNK)
            b_c = c["prep"]["bh"][rs, c["sl"]]
            k_c = c["prep"]["kh"][rs, c["sl"]]
            wrow = c["prep"]["wtot"][ck * WKV_CHUNK:ck * WKV_CHUNK + 1, c["sl"]]
            gi.append((dot_tn(b_c, wu[i][rs])
                       + jnp.where(eye_h, jnp.broadcast_to(wrow, (HEAD_DIM, HEAD_DIM)), 0.0)).astype(BF16))
            hi.append(dot_tn(b_c, uv[i][rs]) + dot_tn(k_c, c["v"][rs]))
        g.append(gi)
        h.append(hi)

    s = [s_scr[c["d"], 2 * c["p"] + c["hh"]] for c in chains]
    for step in range(nc):
        for i, c in enumerate(chains):
            ck = nc - 1 - step if c["reverse"] else step
            rs = slice(ck * WKV_CHUNK, (ck + 1) * WKV_CHUNK)
            sb = s[i].astype(BF16)
            c["y_ref"][c["p"], rs, c["sl"]] = _mm(q[i][rs], sb) + yl[i][rs]
            s[i] = _mm(g[i][ck], sb) + h[i][ck]
    for i, c in enumerate(chains):
        s_scr[c["d"], 2 * c["p"] + c["hh"]] = s[i]


def _wkv_kernel(rf_ref, vf_ref, nf_ref, kf_ref, bf_ref, lf_ref,
                rb_ref, vb_ref, nb_ref, kb_ref, bb_ref, lb_ref, yf_ref, yb_ref, s_scr):
    n = WKV_STEP

    @pl.when(pl.program_id(1) == 0)
    def _():
        s_scr[...] = jnp.zeros_like(s_scr)

    def pair(p, carry):
        row = lax.broadcasted_iota(I32, (n, n), 0)
        col = lax.broadcasted_iota(I32, (n, n), 1)
        m64 = (row >> 6) == (col >> 6)
        er = lax.broadcasted_iota(I32, (HEAD_DIM, HEAD_DIM), 0)
        ec = lax.broadcasted_iota(I32, (HEAD_DIM, HEAD_DIM), 1)
        shared = (row, col, m64)
        rc = lax.broadcasted_iota(I32, (WKV_CHUNK, n), 0)
        lc = lax.broadcasted_iota(I32, (WKV_CHUNK, n), 1)
        li = lc & (WKV_CHUNK - 1)
        samec = lambda log2: (rc >> log2) == (li >> log2)
        masks = (samec(3), samec(4), samec(5), lc >> 6, jnp.where(rc == li, 1.0, 0.0).astype(F32), er == ec,
                 jnp.where(m64, 1.0, 0.0).astype(BF16))
        chains = []
        for pp in (2 * p, 2 * p + 1):
            fwd = _wkv_prep(pp, (rf_ref, vf_ref, nf_ref, kf_ref, bf_ref, lf_ref), False, shared)
            bwd = _wkv_prep(pp, (rb_ref, vb_ref, nb_ref, kb_ref, bb_ref, lb_ref), True, shared)
            chains += [dict(prep=pr, hh=hh, d=d, p=pp, reverse=rev, y_ref=y_ref)
                       for pr, d, rev, y_ref in ((fwd, 0, False, yf_ref), (bwd, 1, True, yb_ref))
                       for hh in range(2)]
        _wkv_chains(chains, s_scr, masks)
        return carry

    lax.fori_loop(0, N_PAIR // 2, pair, 0)


def _wkv(r, v, kn, k0, b0, lw0, k1, b1, lw1):
    bsz, _, t, _ = r.shape
    assert t % WKV_STEP == 0
    nj = t // WKV_STEP
    fwd = pl.BlockSpec((None, N_PAIR, WKV_STEP, LANES), lambda bi, j: (bi, 0, j, 0))
    bwd = pl.BlockSpec((None, N_PAIR, WKV_STEP, LANES), lambda bi, j: (bi, 0, nj - 1 - j, 0))
    out = jax.ShapeDtypeStruct((bsz, N_PAIR, t, LANES), F32)
    return pl.pallas_call(
        _wkv_kernel,
        grid=(bsz, nj),
        in_specs=[fwd] * 6 + [bwd] * 6,
        out_specs=(fwd, bwd),
        out_shape=(out, out),
        scratch_shapes=[pltpu.VMEM((2, 2 * N_PAIR, HEAD_DIM, HEAD_DIM), F32)],
        compiler_params=pltpu.CompilerParams(
            dimension_semantics=("parallel", "arbitrary"), vmem_limit_bytes=VMEM_LIMIT),
        name="wkv",
    )(r, v, kn, k0, b0, lw0, r, v, kn, k1, b1, lw1)


def _post_kernel(yf_ref, yb_ref, bonus_ref, g_ref, ya_ref, x_ref, gng_ref, gnb_ref, bd_ref, wout_ref,
                 nf_ref, rwh_ref, rwl_ref, rb_ref,
                 h_ref, xn_ref, ids_ref, gate_ref, rank_ref, cnt_ref, before_ref, run_scr):
    tt = x_ref.shape[0]
    step = pl.program_id(0)

    @pl.when(step == 0)
    def _():
        run_scr[...] = jnp.zeros_like(run_scr)

    y = jnp.concatenate([yf_ref[p] + yb_ref[p] for p in range(N_PAIR)], axis=1)
    bd = bd_ref[...]
    inv = 1.0 / HEAD_DIM
    mu = _segsum(y, bd) * inv
    yc = y - mu
    var = _segsum(yc * yc, bd) * inv
    yn = yc * lax.rsqrt(var + GN_EPS) * gng_ref[...] + gnb_ref[...] + bonus_ref[...]
    ybm = (yn * g_ref[...]).astype(BF16)
    h = x_ref[...] + _mm(ya_ref[...], wout_ref[:GA, :]) + _mm(ybm, wout_ref[GA:, :])
    h_ref[...] = h
    xn = _rms(h, nf_ref[...])
    xn_ref[...] = xn

    lane = lax.broadcasted_iota(I32, (tt, LANES), 1)
    xh, xl = _split2(xn)
    logits = _mm(xh, rwh_ref[...]) + _mm(xl, rwh_ref[...]) + _mm(xh, rwl_ref[...])
    logits = jnp.where(lane < N_EXPERTS, logits + rb_ref[...], -jnp.inf)
    vals, ids, sel = [], [], jnp.zeros((tt, LANES), F32)
    cur = logits
    for _ in range(TOP_K):
        m = jnp.max(cur, axis=-1, keepdims=True)
        idx = jnp.min(jnp.where(cur == m, lane, LANES), axis=-1, keepdims=True)
        hit = lane == idx
        vals.append(m)
        ids.append(idx)
        sel = sel + jnp.where(hit, 1.0, 0.0)
        cur = jnp.where(hit, -jnp.inf, cur)
    es = [jnp.exp(vv - vals[0]) for vv in vals]
    den = es[0] + es[1] + es[2] + es[3]

    rt = lax.broadcasted_iota(I32, (tt, tt), 0)
    ct = lax.broadcasted_iota(I32, (tt, tt), 1)
    before = jnp.where(ct < rt, 1.0, 0.0).astype(BF16)
    before_ref[...] = run_scr[...]
    cnt = _mm(before, sel.astype(BF16)) + run_scr[...]
    ids_o = jnp.zeros((tt, LANES), I32)
    gate_o = jnp.zeros((tt, LANES), F32)
    rank_o = jnp.zeros((tt, LANES), F32)
    for kx in range(TOP_K):
        rk = jnp.sum(jnp.where(lane == ids[kx], cnt, 0.0), axis=-1, keepdims=True)
        ids_o = jnp.where(lane == kx, ids[kx], ids_o)
        gate_o = jnp.where(lane == kx, es[kx] / den, gate_o)
        rank_o = jnp.where(lane == kx, rk, rank_o)
    ids_ref[...] = ids_o
    gate_ref[...] = gate_o
    rank_ref[...] = rank_o.astype(I32)
    run = run_scr[...] + jnp.sum(sel, axis=0, keepdims=True)
    run_scr[...] = run
    cnt_ref[...] = run


def _post(yf, yb, bonus, g, ya, x, w):
    bsz, t, _ = x.shape
    n = bsz * t
    tt = min(TILE_POST, t)
    nt = t // tt

    def full(a):
        nd = a.ndim
        return pl.BlockSpec(a.shape, lambda s: (0,) * nd)

    consts = (w["b_gn_g"], w["b_gn_b"], w["bd"], w["w_out"], w["norm_ffn"], w["router_wh"], w["router_wl"],
              w["router_b"])
    pair_spec = pl.BlockSpec((None, N_PAIR, tt, LANES), lambda s: (s // nt, 0, s % nt, 0))
    tok3 = lambda width: pl.BlockSpec((None, tt, width), lambda s: (s // nt, s % nt, 0))
    tok2 = lambda width: pl.BlockSpec((tt, width), lambda s: (s, 0))
    in_specs = [pair_spec, pair_spec, tok3(GB), tok3(GB), tok3(GA), tok3(D_MODEL)] + [full(a) for a in consts]
    out_shape = (jax.ShapeDtypeStruct((n, D_MODEL), F32),
                 jax.ShapeDtypeStruct((n, D_MODEL), F32),
                 jax.ShapeDtypeStruct((n, LANES), I32),
                 jax.ShapeDtypeStruct((n, LANES), F32),
                 jax.ShapeDtypeStruct((n, LANES), I32),
                 jax.ShapeDtypeStruct((1, LANES), F32),
                 jax.ShapeDtypeStruct((bsz * nt, 1, LANES), F32))
    out_specs = (tok2(D_MODEL), tok2(D_MODEL), tok2(LANES), tok2(LANES), tok2(LANES),
                 pl.BlockSpec((1, LANES), lambda s: (0, 0)),
                 pl.BlockSpec((None, 1, LANES), lambda s: (s, 0, 0)))
    return pl.pallas_call(
        _post_kernel,
        grid=(bsz * nt,),
        in_specs=in_specs,
        out_specs=out_specs,
        out_shape=out_shape,
        scratch_shapes=[pltpu.VMEM((1, LANES), F32)],
        compiler_params=pltpu.CompilerParams(
            dimension_semantics=("arbitrary",), vmem_limit_bytes=VMEM_LIMIT),
        name="post_router",
    )(yf, yb, bonus, g, ya, x, *consts)


RUN_ALIGN = 8
RUN_SIZES = tuple(1 << i for i in range(8, 2, -1))
TILE_SORT = TILE_ROW * TOP_K + N_EXPERTS * RUN_ALIGN


def _tile_runs(sc_ref, copy, wait):
    def per_expert(e, c):
        o = sc_ref[0, e]
        d = sc_ref[0, N_EXPERTS + e]
        ln = sc_ref[0, 2 * N_EXPERTS + e]
        for size in RUN_SIZES:
            bit = ln & size

            @pl.when(bit != 0)
            def _():
                cp = copy(pl.multiple_of(o, RUN_ALIGN), pl.multiple_of(d, RUN_ALIGN), size)
                if wait:
                    cp.wait()
                else:
                    cp.start()

            o = o + bit
            d = d + bit
        return c

    lax.fori_loop(0, N_EXPERTS, per_expert, 0)


def _slot_matrix(slot, vals, width):
    tt = slot.shape[0]
    lane = lax.broadcasted_iota(I32, (tt, width), 1)
    m = jnp.zeros((tt, width), F32)
    for kx in range(TOP_K):
        m = m + jnp.where(lane == slot[:, kx:kx + 1], vals[kx], 0.0)
    return m


def _scatter_kernel(sc_ref, sp_ref, zt_ref, na_ref, slot_ref, xn_ref, xs_ref, sb, sem):
    s = pl.program_id(0)
    ns = pl.num_programs(0)
    cur = s % 2
    n_blocks = xs_ref.shape[0] // MOE_BLOCK

    def runs(tab_ref, sl, wait):
        _tile_runs(tab_ref, lambda o, d, size: pltpu.make_async_copy(
            sb.at[sl, pl.ds(o, size)], xs_ref.at[pl.ds(d, size)], sem.at[sl]), wait)

    def blocks(wait):
        def body(b, c):
            cp = pltpu.make_async_copy(sb.at[0, pl.ds(0, MOE_BLOCK)],
                                       xs_ref.at[pl.ds(pl.multiple_of(b * MOE_BLOCK, MOE_BLOCK), MOE_BLOCK)],
                                       sem.at[0])
            if wait:
                cp.wait()
            else:
                cp.start()
            return c
        lax.fori_loop(na_ref[0], n_blocks, body, 0)

    @pl.when(s == 0)
    def _():
        sb[0] = jnp.zeros(sb.shape[1:], U32)
        runs(zt_ref, 0, False)
        blocks(False)
        runs(zt_ref, 0, True)
        blocks(True)

    p01 = _slot_matrix(slot_ref[...], (1.0,) * TOP_K, TILE_SORT).astype(BF16)
    sb[cur] = _pack_bf16_pairs(lax.dot_general(p01, xn_ref[...].astype(BF16), TN, preferred_element_type=F32))
    runs(sc_ref, cur, False)

    @pl.when(s > 0)
    def _():
        runs(sp_ref, 1 - cur, True)

    @pl.when(s == ns - 1)
    def _():
        runs(sc_ref, cur, True)


def _scatter_rows(xn, slot, runs, ztab, n_active, rows):
    n = xn.shape[0]
    tt = TILE_ROW
    smem = lambda fn: pl.BlockSpec((None, 1, 3 * N_EXPERTS), fn, memory_space=pltpu.SMEM)
    return pl.pallas_call(
        _scatter_kernel,
        grid=(n // tt,),
        in_specs=[smem(lambda s: (s, 0, 0)),
                  smem(lambda s: (jnp.maximum(s - 1, 0), 0, 0)),
                  pl.BlockSpec(memory_space=pltpu.SMEM),
                  pl.BlockSpec(memory_space=pltpu.SMEM),
                  pl.BlockSpec((tt, LANES), lambda s: (s, 0)),
                  pl.BlockSpec((tt, D_MODEL), lambda s: (s, 0))],
        out_specs=pl.BlockSpec(memory_space=pl.ANY),
        out_shape=jax.ShapeDtypeStruct((rows, D_PACK), U32),
        scratch_shapes=[pltpu.VMEM((2, TILE_SORT, D_PACK), U32), pltpu.SemaphoreType.DMA((2,))],
        compiler_params=pltpu.CompilerParams(
            dimension_semantics=("arbitrary",), vmem_limit_bytes=VMEM_LIMIT),
        name="moe_scatter",
    )(runs, runs, ztab, n_active, slot, xn)


def _expert_kernel(be_ref, na_ref, xs_ref, w1_ref, b1_ref, w2_ref, b2_ref, o_ref):
    del be_ref
    s = pl.program_id(0)

    @pl.when(s < na_ref[0])
    def _():
        xl, xr = _unpack_bf16_pairs(xs_ref[...])
        hdn = _mm(xl, w1_ref[:D_PACK, :]) + _mm(xr, w1_ref[D_PACK:, :]) + b1_ref[...]
        glu = jnp.minimum(hdn[:, :D_FF], SWIGLU_LIMIT)
        lin = jnp.clip(hdn[:, D_FF:], -SWIGLU_LIMIT, SWIGLU_LIMIT)
        act = glu * jax.nn.sigmoid(SWIGLU_ALPHA * glu) * (lin + 1.0)
        o_ref[...] = _pack_bf16_pairs(_mm(act.astype(BF16), w2_ref[...]) + b2_ref[...])

    @pl.when(s >= na_ref[0])
    def _():
        o_ref[...] = jnp.zeros_like(o_ref)


def _experts(xs, block_e, n_active, w):
    rows = xs.shape[0]
    nb = rows // MOE_BLOCK
    grid_spec = pltpu.PrefetchScalarGridSpec(
        num_scalar_prefetch=2,
        grid=(nb,),
        in_specs=[
            pl.BlockSpec((MOE_BLOCK, D_PACK), lambda s, be, na: (s, 0)),
            pl.BlockSpec((None, D_MODEL, 2 * D_FF), lambda s, be, na: (be[s], 0, 0)),
            pl.BlockSpec((None, 1, 2 * D_FF), lambda s, be, na: (be[s], 0, 0)),
            pl.BlockSpec((None, D_FF, D_MODEL), lambda s, be, na: (be[s], 0, 0)),
            pl.BlockSpec((None, 1, D_MODEL), lambda s, be, na: (be[s], 0, 0)),
        ],
        out_specs=pl.BlockSpec((MOE_BLOCK, D_PACK), lambda s, be, na: (s, 0)),
    )
    return pl.pallas_call(
        _expert_kernel,
        grid_spec=grid_spec,
        out_shape=jax.ShapeDtypeStruct((rows, D_PACK), U32),
        compiler_params=pltpu.CompilerParams(
            dimension_semantics=("arbitrary",), vmem_limit_bytes=VMEM_LIMIT),
        name="moe_experts",
    )(block_e, n_active, xs, w["moe_w1"], w["moe_b1"], w["moe_w2"], w["moe_b2"])


def _combine_kernel(scur_ref, snxt_ref, slot_ref, h_ref, gate_ref, p_ref, npl_ref, pg_ref, pp_ref, nfin_ref,
                    os_ref, y_ref, gb, sem):
    s = pl.program_id(0)
    ns = pl.num_programs(0)
    slot = s % 2

    def runs(sc_ref, sl, wait):
        _tile_runs(sc_ref, lambda o, d, size: pltpu.make_async_copy(
            os_ref.at[pl.ds(d, size)], gb.at[sl, pl.ds(o, size)], sem.at[sl]), wait)

    @pl.when(s == 0)
    def _():
        gb[...] = jnp.zeros_like(gb)
        runs(scur_ref, 0, False)

    @pl.when(s + 1 < ns)
    def _():
        runs(snxt_ref, 1 - slot, False)

    runs(scur_ref, slot, True)

    gate = gate_ref[...]
    pm = _slot_matrix(slot_ref[...], [gate[:, kx:kx + 1] for kx in range(TOP_K)], TILE_SORT)
    tt = pm.shape[0]
    pst = jnp.concatenate(_split2(pm), axis=0)
    gl, gr = _unpack_bf16_pairs(gb[slot])
    moe = jnp.concatenate([_mm(pst, gl), _mm(pst, gr)], axis=1)
    h = h_ref[...] + moe[:tt] + moe[tt:]
    gt = jax.nn.sigmoid(_mm(_rms(h, npl_ref[...]).astype(BF16), pg_ref[...]))
    h = h + _mm(p_ref[...].astype(BF16), pp_ref[...]) * gt
    y_ref[...] = _rms(h, nfin_ref[...])


def _combine(h, gates, slot, runs, p, os_rows, w):
    n = h.shape[0]
    tt = TILE_ROW
    ns = n // tt

    def full(a):
        nd = a.ndim
        return pl.BlockSpec(a.shape, lambda s: (0,) * nd)

    consts = (w["norm_ple"], w["ple_gate"], w["ple_proj"], w["norm_final"])
    smem = lambda fn: pl.BlockSpec((None, 1, 3 * N_EXPERTS), fn, memory_space=pltpu.SMEM)
    return pl.pallas_call(
        _combine_kernel,
        grid=(ns,),
        in_specs=[smem(lambda s: (s, 0, 0)),
                  smem(lambda s: (jnp.minimum(s + 1, ns - 1), 0, 0)),
                  pl.BlockSpec((tt, LANES), lambda s: (s, 0)),
                  pl.BlockSpec((tt, D_MODEL), lambda s: (s, 0)),
                  pl.BlockSpec((tt, LANES), lambda s: (s, 0)),
                  pl.BlockSpec((tt, PLE_DIM), lambda s: (s, 0))]
        + [full(a) for a in consts]
        + [pl.BlockSpec(memory_space=pl.ANY)],
        out_specs=pl.BlockSpec((tt, D_MODEL), lambda s: (s, 0)),
        out_shape=jax.ShapeDtypeStruct((n, D_MODEL), F32),
        scratch_shapes=[pltpu.VMEM((2, TILE_SORT, D_PACK), U32), pltpu.SemaphoreType.DMA((2,))],
        compiler_params=pltpu.CompilerParams(
            dimension_semantics=("arbitrary",), vmem_limit_bytes=VMEM_LIMIT),
        name="moe_combine",
    )(runs, runs, slot, h, gates, p, *consts, os_rows)


def _prep_weights(norm_mix, w_in, a_ln_g, a_ln_b, a_ws, a_bs, b_conv, b_w0, b_w2, b_a0, b_a2, b_g2,
                  b_kk, b_ka, b_rk, b_gn_g, b_gn_b, w_out, norm_ffn, router_w, router_b, moe_w1,
                  moe_b1, moe_w2, moe_b2, norm_ple, ple_proj, ple_gate, norm_final):
    row = lambda a: a.reshape(1, -1).astype(F32)

    def lora_pad(m):
        z = jnp.zeros((2, 2 * LORA, GB), F32)
        z = z.at[0, :LORA].set(m[0]).at[1, LORA:].set(m[1])
        return z.astype(BF16)

    seg = jnp.arange(GB, dtype=I32) // HEAD_DIM
    rw = jnp.pad(router_w[0].astype(F32), ((0, 0), (0, LANES - N_EXPERTS)))
    rw_hi = rw.astype(BF16)
    return {
        "norm_mix": row(norm_mix[0]),
        "w_in": w_in[0].astype(BF16),
        "a_ln_g": row(a_ln_g[0]),
        "a_ln_b": row(a_ln_b[0]),
        "a_ws": a_ws[0].reshape(-1, CHUNK_A).astype(BF16),
        "a_bs": jnp.repeat(a_bs[0].T.astype(F32), HEAD_DIM, axis=1),
        "b_conv": b_conv[0].astype(F32),
        "b_w0": b_w0[0].astype(F32),
        "b_w2": lora_pad(b_w2[0]),
        "b_a0": b_a0[0].astype(F32),
        "b_a2": lora_pad(b_a2[0]),
        "b_g2": b_g2[0].astype(BF16),
        "b_kk": row(b_kk[0]),
        "b_ka": row(b_ka[0]),
        "b_rk": row(b_rk[0]),
        "bd": (seg[:, None] == seg[None, :]).astype(BF16),
        "b_gn_g": row(b_gn_g[0]),
        "b_gn_b": row(b_gn_b[0]),
        "w_out": w_out[0].astype(BF16),
        "norm_ffn": row(norm_ffn[0]),
        "router_wh": rw_hi,
        "router_wl": (rw - rw_hi.astype(F32)).astype(BF16),
        "router_b": jnp.pad(router_b[0].astype(F32), (0, LANES - N_EXPERTS)).reshape(1, LANES),
        "moe_w1": moe_w1[0].astype(BF16),
        "moe_b1": moe_b1[0].astype(F32).reshape(N_EXPERTS, 1, 2 * D_FF),
        "moe_w2": moe_w2[0].astype(BF16),
        "moe_b2": moe_b2[0].astype(F32).reshape(N_EXPERTS, 1, D_MODEL),
        "norm_ple": row(norm_ple[0]),
        "ple_gate": ple_gate[0].astype(BF16),
        "ple_proj": ple_proj[0].astype(BF16),
        "norm_final": row(norm_final),
    }


def _forward(x, p, w):
    bsz, t, _ = x.shape
    n = bsz * t
    ya, r, v, kn, lw0, lw1, k0, k1, b0, b1, g, bonus = _inproj(x, w)
    yf, yb = _wkv(r, v, kn, k0, b0, lw0, k1, b1, lw1)
    h, xn, ids, gates, rank, counts, before = _post(yf, yb, bonus, g, ya, x, w)

    nt = n // TILE_ROW
    counts = counts[0, :N_EXPERTS].astype(I32)
    before = before[:, 0, :N_EXPERTS].astype(I32)
    tile_cnt = jnp.concatenate([before[1:], counts[None]], axis=0) - before
    tile_pad = (tile_cnt + RUN_ALIGN - 1) // RUN_ALIGN * RUN_ALIGN
    tile_off = jnp.cumsum(tile_pad, axis=1) - tile_pad
    before_pad = jnp.cumsum(tile_pad, axis=0) - tile_pad
    padded = (jnp.sum(tile_pad, axis=0) + MOE_BLOCK - 1) // MOE_BLOCK * MOE_BLOCK
    pends = jnp.cumsum(padded)
    pstarts = pends - padded
    n_blocks = -(-(n * TOP_K + nt * N_EXPERTS * (RUN_ALIGN - 1)) // MOE_BLOCK) + N_EXPERTS
    block_start = jnp.arange(n_blocks, dtype=I32) * MOE_BLOCK
    block_e = jnp.minimum(jnp.sum(pends[None, :] <= block_start[:, None], axis=1), N_EXPERTS - 1).astype(I32)
    n_active = (pends[-1:] // MOE_BLOCK).astype(I32)
    runs = jnp.concatenate([tile_off, pstarts[None, :] + before_pad, tile_pad], axis=1)
    runs = runs.reshape(nt, 1, 3 * N_EXPERTS).astype(I32)
    rel = jnp.repeat(tile_off - before, TILE_ROW, axis=0)
    slot = jnp.take_along_axis(rel, ids[:, :TOP_K], axis=1) + rank[:, :TOP_K]
    slot = jnp.pad(slot, ((0, 0), (0, LANES - TOP_K)), constant_values=-1).astype(I32)

    rows_used = jnp.sum(tile_pad, axis=0)
    ztab = jnp.concatenate([jnp.zeros_like(padded), pstarts + rows_used, padded - rows_used])
    ztab = ztab.reshape(1, 3 * N_EXPERTS).astype(I32)

    xs = _scatter_rows(xn, slot, runs, ztab, n_active, n_blocks * MOE_BLOCK)
    os_rows = _experts(xs, block_e, n_active, w)
    y = _combine(h, gates, slot, runs, p.reshape(n, PLE_DIM), os_rows, w)
    return y.reshape(bsz, t, D_MODEL)


def kernel(x_prompt, x_sample, p_prompt, p_sample, norm_mix, w_in, a_ln_g, a_ln_b, a_ws, a_bs, b_conv, b_w0, b_w2, b_a0, b_a2, b_g2, b_kk, b_ka, b_rk, b_gn_g, b_gn_b, w_out, norm_ffn, router_w, router_b, moe_w1, moe_b1, moe_w2, moe_b2, norm_ple, ple_proj, ple_gate, norm_final):
    assert norm_mix.shape[0] == 1, "single-layer trunk"
    w = _prep_weights(norm_mix, w_in, a_ln_g, a_ln_b, a_ws, a_bs, b_conv, b_w0, b_w2, b_a0, b_a2, b_g2,
                      b_kk, b_ka, b_rk, b_gn_g, b_gn_b, w_out, norm_ffn, router_w, router_b, moe_w1,
                      moe_b1, moe_w2, moe_b2, norm_ple, ple_proj, ple_gate, norm_final)
    y_prompt = _forward(x_prompt, p_prompt[0], w)
    y_sample = _forward(x_sample, p_sample[0], w)
    return (y_prompt, y_sample)
```

```python
import math

import jax
import jax.numpy as jnp
from jax import lax
from jax.experimental import pallas as pl
from jax.experimental.pallas import tpu as pltpu

F32 = jnp.float32
BF16 = jnp.bfloat16
I32 = jnp.int32
U32 = jnp.uint32

D_MODEL = 1024
D_PACK = D_MODEL // 2
HEAD_DIM = 64
GA = 512
GB = 512
N_PAIR = GB // 128
CHUNK_A = 128
LORA = 64
LORA_G = 128
B_CONV = 3 * GB + 4 * LORA + LORA_G
N_EXPERTS = 32
TOP_K = 4
D_FF = 1024
PLE_DIM = 256
SWIGLU_ALPHA = 1.702
SWIGLU_LIMIT = 7.0
EPS = 1e-6
GN_EPS = 64e-5
DECAY_SCALE = math.exp(-0.5)

LANES = 128
TILE_IN = 256
WKV_STEP = 256
WKV_CHUNK = 64
TILE_POST = 256
TILE_ROW = 256
MOE_BLOCK = 512
VMEM_LIMIT = 56 * 1024 * 1024

NT = (((1,), (1,)), ((), ()))
TN = (((0,), (0,)), ((), ()))


def _mm(a, b):
    return jnp.dot(a, b, preferred_element_type=F32)


def _split2(q):
    hi = q.astype(BF16)
    lo = (q - hi.astype(F32)).astype(BF16)
    return hi, lo


def _segsum(q, bd):
    hi, lo = _split2(q)
    return _mm(hi, bd) + _mm(lo, bd)


def _gelu(z):
    return 0.5 * z * (1.0 + lax.erf(z * (1.0 / math.sqrt(2.0))))


def _pack_bf16_pairs(a):
    half = a.shape[1] // 2
    u = lax.bitcast_convert_type(a.astype(BF16).astype(F32), U32)
    return (u[:, :half] & jnp.uint32(0xFFFF0000)) | (u[:, half:] >> jnp.uint32(16))


def _unpack_bf16_pairs(u):
    left = lax.bitcast_convert_type(u & jnp.uint32(0xFFFF0000), F32)
    right = lax.bitcast_convert_type(u << jnp.uint32(16), F32)
    return left.astype(BF16), right.astype(BF16)


def _rms(xv, g):
    ms = jnp.mean(xv * xv, axis=-1, keepdims=True)
    return xv * lax.rsqrt(ms + EPS) * g


def _inproj_kernel(x_ref, xp_ref, xn_ref, nm_ref, win_ref, lng_ref, lnb_ref, ws_ref, bs_ref,
                   conv_ref, w0_ref, w2_ref, a0_ref, a2_ref, g2_ref, kk_ref, ka_ref, rk_ref, bd_ref,
                   ya_ref, r_ref, v_ref, kn_ref, lw0_ref, lw1_ref, k0_ref, k1_ref, b0_ref, b1_ref,
                   g_ref, bonus_ref):
    tt = x_ref.shape[0]
    i = pl.program_id(1)
    last = pl.num_programs(1) - 1
    nm = nm_ref[...]

    xe = jnp.concatenate([x_ref[...], xp_ref[...], xn_ref[...]], axis=0)
    xe = _rms(xe, nm).astype(BF16)
    za = _mm(xe[:tt], win_ref[:, :2 * GA])
    ze = _mm(xe, win_ref[:, 2 * GA:])
    zb = ze[:tt]
    row_prev = jnp.where(i > 0, ze[tt + 7:tt + 8], 0.0)
    row_next = jnp.where(i < last, ze[tt + 8:tt + 9], 0.0)

    u = _gelu(za[:, :GA])
    v = _gelu(za[:, GA:])
    mu = jnp.mean(v, axis=-1, keepdims=True)
    var = jnp.mean(jnp.square(v - mu), axis=-1, keepdims=True)
    v = ((v - mu) * lax.rsqrt(var + EPS) * lng_ref[...] + lnb_ref[...]).astype(BF16)
    lane_head = lax.broadcasted_iota(I32, (CHUNK_A, GA), 1) // HEAD_DIM
    for c in range(tt // CHUNK_A):
        rows = slice(c * CHUNK_A, (c + 1) * CHUNK_A)
        o = _mm(ws_ref[...], v[rows])
        s = bs_ref[...]
        for h in range(GA // HEAD_DIM):
            s = s + jnp.where(lane_head == h, o[h * CHUNK_A:(h + 1) * CHUNK_A], 0.0)
        ya_ref[rows, :] = (u[rows] * s).astype(ya_ref.dtype)

    conv = conv_ref[...]
    trow = lax.broadcasted_iota(I32, (tt, 1), 0)
    z_prev = jnp.where(trow == 0, row_prev, pltpu.roll(zb, 1, axis=0))
    z_next = jnp.where(trow == tt - 1, row_next, pltpu.roll(zb, tt - 1, axis=0))
    zc = z_prev * conv[0:1] + zb * conv[1:2] + z_next * conv[2:3]
    r = zc[:, :GB]
    k = zc[:, GB:2 * GB]
    vv = zc[:, 2 * GB:3 * GB]
    o0 = 3 * GB
    xw = jnp.tanh(zc[:, o0:o0 + 2 * LORA]).astype(BF16)
    xa = zc[:, o0 + 2 * LORA:o0 + 4 * LORA].astype(BF16)
    xg = jax.nn.sigmoid(zc[:, o0 + 4 * LORA:]).astype(BF16)
    bd = bd_ref[...]
    kk = k * kk_ref[...]
    kk = kk / jnp.maximum(jnp.sqrt(_segsum(kk * kk, bd)), 1e-12)
    ka = ka_ref[...]
    lw_refs = (lw0_ref, lw1_ref)
    k_refs = (k0_ref, k1_ref)
    b_refs = (b0_ref, b1_ref)
    ksum = None
    for d in range(2):
        yw = w0_ref[d:d + 1, :] + _mm(xw, w2_ref[d])
        lw = -DECAY_SCALE * jax.nn.sigmoid(yw)
        a = jax.nn.sigmoid(a0_ref[d:d + 1, :] + _mm(xa, a2_ref[d]))
        kd = k * (1.0 + (a - 1.0) * ka)
        bb = kk * a
        ksum = kd if ksum is None else ksum + kd
        for p in range(N_PAIR):
            ls = slice(p * LANES, (p + 1) * LANES)
            lw_refs[d][p] = lw[:, ls]
            k_refs[d][p] = kd[:, ls].astype(k0_ref.dtype)
            b_refs[d][p] = bb[:, ls].astype(b0_ref.dtype)
    for p in range(N_PAIR):
        ls = slice(p * LANES, (p + 1) * LANES)
        r_ref[p] = r[:, ls].astype(r_ref.dtype)
        v_ref[p] = vv[:, ls].astype(v_ref.dtype)
        kn_ref[p] = kk[:, ls].astype(kn_ref.dtype)
    g_ref[...] = _mm(xg, g2_ref[...])
    bonus_ref[...] = _segsum(r * ksum * rk_ref[...], bd) * vv


def _inproj(x, w):
    bsz, t, _ = x.shape
    tt = min(TILE_IN, t)
    nt = t // tt
    t8 = tt // 8

    def full(a):
        nd = a.ndim
        return pl.BlockSpec(a.shape, lambda b, i: (0,) * nd)

    consts = (w["norm_mix"], w["w_in"], w["a_ln_g"], w["a_ln_b"], w["a_ws"], w["a_bs"], w["b_conv"],
              w["b_w0"], w["b_w2"], w["b_a0"], w["b_a2"], w["b_g2"], w["b_kk"], w["b_ka"], w["b_rk"],
              w["bd"])
    in_specs = [
        pl.BlockSpec((None, tt, D_MODEL), lambda b, i: (b, i, 0)),
        pl.BlockSpec((None, 8, D_MODEL), lambda b, i: (b, jnp.maximum(i * t8 - 1, 0), 0)),
        pl.BlockSpec((None, 8, D_MODEL), lambda b, i: (b, jnp.minimum((i + 1) * t8, t // 8 - 1), 0)),
    ] + [full(a) for a in consts]
    pair = lambda dt: jax.ShapeDtypeStruct((bsz, N_PAIR, t, LANES), dt)
    flat = lambda dt: jax.ShapeDtypeStruct((bsz, t, GB), dt)
    pair_spec = pl.BlockSpec((None, N_PAIR, tt, LANES), lambda b, i: (b, 0, i, 0))
    flat_spec = pl.BlockSpec((None, tt, GB), lambda b, i: (b, i, 0))
    out_shape = (flat(BF16),
                 pair(BF16), pair(BF16), pair(BF16),
                 pair(F32), pair(F32),
                 pair(BF16), pair(BF16), pair(BF16), pair(BF16),
                 flat(F32), flat(F32))
    out_specs = (flat_spec,) + (pair_spec,) * 9 + (flat_spec, flat_spec)
    return pl.pallas_call(
        _inproj_kernel,
        grid=(bsz, nt),
        in_specs=in_specs,
        out_specs=out_specs,
        out_shape=out_shape,
        compiler_params=pltpu.CompilerParams(
            dimension_semantics=("parallel", "parallel"), vmem_limit_bytes=VMEM_LIMIT),
        name="inproj",
    )(x, x, x, *consts)


def _wkv_prep(p, refs, reverse, shared):
    r_ref, v_ref, kn_ref, k_ref, b_ref, lw_ref = refs
    row, col, m64 = shared
    strict = m64 & ((col > row) if reverse else (col < row))
    incl = m64 & ((col >= row) if reverse else (col <= row))

    lw = lw_ref[p]
    l1, l2 = _split2(lw)
    tri = jnp.where(incl, 1.0, 0.0).astype(BF16)
    cum = _mm(tri, l1) + _mm(tri, l2)
    nc = WKV_STEP // WKV_CHUNK
    end = 0 if reverse else WKV_CHUNK - 1
    tot = jnp.concatenate(
        [jnp.broadcast_to(cum[ck * WKV_CHUNK + end:ck * WKV_CHUNK + end + 1], (WKV_CHUNK, LANES))
         for ck in range(nc)], axis=0)
    rr = r_ref[p].astype(F32)
    kn = kn_ref[p].astype(F32)
    kd = k_ref[p].astype(F32)
    bb = b_ref[p].astype(F32)
    winv = jnp.exp(-cum)
    wd = jnp.exp(tot - cum)
    return dict(
        strict=strict, incl=incl, wtot=jnp.exp(tot),
        rt=(rr * jnp.exp(cum)).astype(BF16), at=(-kn * jnp.exp(cum - lw)).astype(BF16),
        bt=(bb * winv).astype(BF16), kt=(kd * winv).astype(BF16),
        bh=(bb * wd).astype(BF16), kh=(kd * wd).astype(BF16), vb=v_ref[p].astype(BF16))


def _wkv_chains(chains, s_scr, masks):
    c8, c16, c32, cb, eye_c, eye_h, blk = masks
    n = WKV_STEP
    nc = n // WKV_CHUNK
    dot_nt = lambda a, b: lax.dot_general(a, b, NT, preferred_element_type=F32)
    dot_tn = lambda a, b: lax.dot_general(a, b, TN, preferred_element_type=F32)
    each = lambda fn: [fn(c) for c in chains]

    for c in chains:
        sl = slice(c["hh"] * HEAD_DIM, (c["hh"] + 1) * HEAD_DIM)
        pr = c["prep"]
        c.update(sl=sl, a=pr["at"][:, sl], r=pr["rt"][:, sl], b=pr["bt"][:, sl], k=pr["kt"][:, sl],
                 v=pr["vb"][:, sl], strict=pr["strict"], incl=pr["incl"])
    lab = each(lambda c: jnp.where(c["strict"], dot_nt(c["a"], c["b"]), 0.0))
    lak = each(lambda c: jnp.where(c["strict"], dot_nt(c["a"], c["k"]), 0.0).astype(BF16))
    mrb = each(lambda c: jnp.where(c["incl"], dot_nt(c["r"], c["b"]), 0.0).astype(BF16))
    mrk = each(lambda c: jnp.where(c["incl"], dot_nt(c["r"], c["k"]), 0.0).astype(BF16))

    lab_c = [sum(jnp.where(cb == ck, l[ck * WKV_CHUNK:(ck + 1) * WKV_CHUNK], 0.0) for ck in range(nc))
             for l in lab]
    bdiag = lambda zb: jnp.concatenate([zb] * nc, axis=0) * blk
    l8 = [jnp.where(c8, l, 0.0) for l in lab_c]
    p1 = [l.astype(BF16) for l in l8]
    p2 = [_mm(q, bdiag(q)).astype(BF16) for q in p1]
    p2d = [bdiag(q) for q in p2]
    x = [eye_c + l for l in l8]
    p4d = [bdiag(_mm(q, qd).astype(BF16)) for q, qd in zip(p2, p2d)]
    x = [xi + _mm(xi.astype(BF16), qd) for xi, qd in zip(x, p2d)]
    x = [xi + _mm(xi.astype(BF16), qd) for xi, qd in zip(x, p4d)]
    prev = c8
    for cur in (c16, c32, None):
        lvl = jnp.logical_not(prev) if cur is None else cur & jnp.logical_not(prev)
        xb = [xi.astype(BF16) for xi in x]
        t = [_mm(jnp.where(lvl, l, 0.0).astype(BF16), bdiag(xi)).astype(BF16) for l, xi in zip(lab_c, xb)]
        x = [xi + _mm(xbi, bdiag(ti)) for xi, xbi, ti in zip(x, xb, t)]
        prev = cur
    tb = [bdiag(xi.astype(BF16)) for xi in x]

    x1 = [_mm(l, c["v"]).astype(BF16) for l, c in zip(lak, chains)]
    wu = [_mm(t, c["a"]).astype(BF16) for t, c in zip(tb, chains)]
    uv = [_mm(t, xi).astype(BF16) for t, xi in zip(tb, x1)]
    q = [(c["r"].astype(F32) + _mm(m, w)).astype(BF16) for c, m, w in zip(chains, mrb, wu)]
    yl = [_mm(m, u) + _mm(mk, c["v"]) for m, u, mk, c in zip(mrb, uv, mrk, chains)]

    g, h = [], []
    for i, c in enumerate(chains):
        gi, hi = [], []
        for ck in range(nc):
            rs = slice(ck * WKV_CHUNK, (ck + 1) * WKV_CHUNK)
            b_c = c["prep"]["bh"][rs, c["sl"]]
            k_c = c["prep"]["kh"][rs, c["sl"]]
            wrow = c["prep"]["wtot"][ck * WKV_CHUNK:ck * WKV_CHUNK + 1, c["sl"]]
            gi.append((dot_tn(b_c, wu[i][rs])
                       + jnp.where(eye_h, jnp.broadcast_to(wrow, (HEAD_DIM, HEAD_DIM)), 0.0)).astype(BF16))
            hi.append(dot_tn(b_c, uv[i][rs]) + dot_tn(k_c, c["v"][rs]))
        g.append(gi)
        h.append(hi)

    s = [s_scr[c["d"], 2 * c["p"] + c["hh"]] for c in chains]
    for step in range(nc):
        for i, c in enumerate(chains):
            ck = nc - 1 - step if c["reverse"] else step
            rs = slice(ck * WKV_CHUNK, (ck + 1) * WKV_CHUNK)
            sb = s[i].astype(BF16)
            c["y_ref"][c["p"], rs, c["sl"]] = _mm(q[i][rs], sb) + yl[i][rs]
            s[i] = _mm(g[i][ck], sb) + h[i][ck]
    for i, c in enumerate(chains):
        s_scr[c["d"], 2 * c["p"] + c["hh"]] = s[i]


def _wkv_kernel(rf_ref, vf_ref, nf_ref, kf_ref, bf_ref, lf_ref,
                rb_ref, vb_ref, nb_ref, kb_ref, bb_ref, lb_ref, yf_ref, yb_ref, s_scr):
    n = WKV_STEP

    @pl.when(pl.program_id(1) == 0)
    def _():
        s_scr[...] = jnp.zeros_like(s_scr)

    def pair(p, carry):
        row = lax.broadcasted_iota(I32, (n, n), 0)
        col = lax.broadcasted_iota(I32, (n, n), 1)
        m64 = (row >> 6) == (col >> 6)
        er = lax.broadcasted_iota(I32, (HEAD_DIM, HEAD_DIM), 0)
        ec = lax.broadcasted_iota(I32, (HEAD_DIM, HEAD_DIM), 1)
        shared = (row, col, m64)
        rc = lax.broadcasted_iota(I32, (WKV_CHUNK, n), 0)
        lc = lax.broadcasted_iota(I32, (WKV_CHUNK, n), 1)
        li = lc & (WKV_CHUNK - 1)
        samec = lambda log2: (rc >> log2) == (li >> log2)
        masks = (samec(3), samec(4), samec(5), lc >> 6, jnp.where(rc == li, 1.0, 0.0).astype(F32), er == ec,
                 jnp.where(m64, 1.0, 0.0).astype(BF16))
        chains = []
        for pp in (2 * p, 2 * p + 1):
            fwd = _wkv_prep(pp, (rf_ref, vf_ref, nf_ref, kf_ref, bf_ref, lf_ref), False, shared)
            bwd = _wkv_prep(pp, (rb_ref, vb_ref, nb_ref, kb_ref, bb_ref, lb_ref), True, shared)
            chains += [dict(prep=pr, hh=hh, d=d, p=pp, reverse=rev, y_ref=y_ref)
                       for pr, d, rev, y_ref in ((fwd, 0, False, yf_ref), (bwd, 1, True, yb_ref))
                       for hh in range(2)]
        _wkv_chains(chains, s_scr, masks)
        return carry

    lax.fori_loop(0, N_PAIR // 2, pair, 0)


def _wkv(r, v, kn, k0, b0, lw0, k1, b1, lw1):
    bsz, _, t, _ = r.shape
    assert t % WKV_STEP == 0
    nj = t // WKV_STEP
    fwd = pl.BlockSpec((None, N_PAIR, WKV_STEP, LANES), lambda bi, j: (bi, 0, j, 0))
    bwd = pl.BlockSpec((None, N_PAIR, WKV_STEP, LANES), lambda bi, j: (bi, 0, nj - 1 - j, 0))
    out = jax.ShapeDtypeStruct((bsz, N_PAIR, t, LANES), F32)
    return pl.pallas_call(
        _wkv_kernel,
        grid=(bsz, nj),
        in_specs=[fwd] * 6 + [bwd] * 6,
        out_specs=(fwd, bwd),
        out_shape=(out, out),
        scratch_shapes=[pltpu.VMEM((2, 2 * N_PAIR, HEAD_DIM, HEAD_DIM), F32)],
        compiler_params=pltpu.CompilerParams(
            dimension_semantics=("parallel", "arbitrary"), vmem_limit_bytes=VMEM_LIMIT),
        name="wkv",
    )(r, v, kn, k0, b0, lw0, r, v, kn, k1, b1, lw1)


def _post_kernel(yf_ref, yb_ref, bonus_ref, g_ref, ya_ref, x_ref, gng_ref, gnb_ref, bd_ref, wout_ref,
                 nf_ref, rwh_ref, rwl_ref, rb_ref,
                 h_ref, xn_ref, ids_ref, gate_ref, rank_ref, cnt_ref, before_ref, run_scr):
    tt = x_ref.shape[0]
    step = pl.program_id(0)

    @pl.when(step == 0)
    def _():
        run_scr[...] = jnp.zeros_like(run_scr)

    y = jnp.concatenate([yf_ref[p] + yb_ref[p] for p in range(N_PAIR)], axis=1)
    bd = bd_ref[...]
    inv = 1.0 / HEAD_DIM
    mu = _segsum(y, bd) * inv
    yc = y - mu
    var = _segsum(yc * yc, bd) * inv
    yn = yc * lax.rsqrt(var + GN_EPS) * gng_ref[...] + gnb_ref[...] + bonus_ref[...]
    ybm = (yn * g_ref[...]).astype(BF16)
    h = x_ref[...] + _mm(ya_ref[...], wout_ref[:GA, :]) + _mm(ybm, wout_ref[GA:, :])
    h_ref[...] = h
    xn = _rms(h, nf_ref[...])
    xn_ref[...] = xn

    lane = lax.broadcasted_iota(I32, (tt, LANES), 1)
    xh, xl = _split2(xn)
    logits = _mm(xh, rwh_ref[...]) + _mm(xl, rwh_ref[...]) + _mm(xh, rwl_ref[...])
    logits = jnp.where(lane < N_EXPERTS, logits + rb_ref[...], -jnp.inf)
    vals, ids, sel = [], [], jnp.zeros((tt, LANES), F32)
    cur = logits
    for _ in range(TOP_K):
        m = jnp.max(cur, axis=-1, keepdims=True)
        idx = jnp.min(jnp.where(cur == m, lane, LANES), axis=-1, keepdims=True)
        hit = lane == idx
        vals.append(m)
        ids.append(idx)
        sel = sel + jnp.where(hit, 1.0, 0.0)
        cur = jnp.where(hit, -jnp.inf, cur)
    es = [jnp.exp(vv - vals[0]) for vv in vals]
    den = es[0] + es[1] + es[2] + es[3]

    rt = lax.broadcasted_iota(I32, (tt, tt), 0)
    ct = lax.broadcasted_iota(I32, (tt, tt), 1)
    before = jnp.where(ct < rt, 1.0, 0.0).astype(BF16)
    before_ref[...] = run_scr[...]
    cnt = _mm(before, sel.astype(BF16)) + run_scr[...]
    ids_o = jnp.zeros((tt, LANES), I32)
    gate_o = jnp.zeros((tt, LANES), F32)
    rank_o = jnp.zeros((tt, LANES), F32)
    for kx in range(TOP_K):
        rk = jnp.sum(jnp.where(lane == ids[kx], cnt, 0.0), axis=-1, keepdims=True)
        ids_o = jnp.where(lane == kx, ids[kx], ids_o)
        gate_o = jnp.where(lane == kx, es[kx] / den, gate_o)
        rank_o = jnp.where(lane == kx, rk, rank_o)
    ids_ref[...] = ids_o
    gate_ref[...] = gate_o
    rank_ref[...] = rank_o.astype(I32)
    run = run_scr[...] + jnp.sum(sel, axis=0, keepdims=True)
    run_scr[...] = run
    cnt_ref[...] = run


def _post(yf, yb, bonus, g, ya, x, w):
    bsz, t, _ = x.shape
    n = bsz * t
    tt = min(TILE_POST, t)
    nt = t // tt

    def full(a):
        nd = a.ndim
        return pl.BlockSpec(a.shape, lambda s: (0,) * nd)

    consts = (w["b_gn_g"], w["b_gn_b"], w["bd"], w["w_out"], w["norm_ffn"], w["router_wh"], w["router_wl"],
              w["router_b"])
    pair_spec = pl.BlockSpec((None, N_PAIR, tt, LANES), lambda s: (s // nt, 0, s % nt, 0))
    tok3 = lambda width: pl.BlockSpec((None, tt, width), lambda s: (s // nt, s % nt, 0))
    tok2 = lambda width: pl.BlockSpec((tt, width), lambda s: (s, 0))
    in_specs = [pair_spec, pair_spec, tok3(GB), tok3(GB), tok3(GA), tok3(D_MODEL)] + [full(a) for a in consts]
    out_shape = (jax.ShapeDtypeStruct((n, D_MODEL), F32),
                 jax.ShapeDtypeStruct((n, D_MODEL), F32),
                 jax.ShapeDtypeStruct((n, LANES), I32),
                 jax.ShapeDtypeStruct((n, LANES), F32),
                 jax.ShapeDtypeStruct((n, LANES), I32),
                 jax.ShapeDtypeStruct((1, LANES), F32),
                 jax.ShapeDtypeStruct((bsz * nt, 1, LANES), F32))
    out_specs = (tok2(D_MODEL), tok2(D_MODEL), tok2(LANES), tok2(LANES), tok2(LANES),
                 pl.BlockSpec((1, LANES), lambda s: (0, 0)),
                 pl.BlockSpec((None, 1, LANES), lambda s: (s, 0, 0)))
    return pl.pallas_call(
        _post_kernel,
        grid=(bsz * nt,),
        in_specs=in_specs,
        out_specs=out_specs,
        out_shape=out_shape,
        scratch_shapes=[pltpu.VMEM((1, LANES), F32)],
        compiler_params=pltpu.CompilerParams(
            dimension_semantics=("arbitrary",), vmem_limit_bytes=VMEM_LIMIT),
        name="post_router",
    )(yf, yb, bonus, g, ya, x, *consts)


RUN_ALIGN = 8
RUN_SIZES = tuple(1 << i for i in range(8, 2, -1))
RUN_SMALL = 64
TILE_SORT = TILE_ROW * TOP_K + N_EXPERTS * RUN_ALIGN
N_RUNS = N_EXPERTS + 1
DUMP_ROWS = 2 * TILE_ROW


def _tile_runs(sc_ref, copy, wait=False):
    def per_run(e, c):
        o0 = sc_ref[0, e]
        d0 = sc_ref[0, N_RUNS + e]
        ln = sc_ref[0, 2 * N_RUNS + e]

        def pieces(sizes, o, d):
            for size in sizes:
                bit = ln & size

                @pl.when(bit != 0)
                def _():
                    cp = copy(e, pl.multiple_of(o, RUN_ALIGN), pl.multiple_of(d, RUN_ALIGN), size)
                    if wait:
                        cp.wait()
                    else:
                        cp.start()

                o = o + bit
                d = d + bit

        big = tuple(sz for sz in RUN_SIZES if sz >= RUN_SMALL)
        small = tuple(sz for sz in RUN_SIZES if sz < RUN_SMALL)

        @pl.when(ln >= RUN_SMALL)
        def _():
            pieces(big, o0, d0)

        skip = ln & ~(RUN_SMALL - 1)
        pieces(small, o0 + skip, d0 + skip)
        return c

    lax.fori_loop(0, N_RUNS, per_run, 0)


def _slot_matrix(slot, vals, width):
    tt = slot.shape[0]
    lane = lax.broadcasted_iota(I32, (tt, width), 1)
    m = jnp.zeros((tt, width), F32)
    for kx in range(TOP_K):
        m = m + jnp.where(lane == slot[:, kx:kx + 1], vals[kx], 0.0)
    return m


def _scatter_kernel(sc_ref, zt_ref, na_ref, slot_ref, xn_ref, xs_ref, sb, sem):
    s = pl.program_id(0)
    ns = pl.num_programs(0)
    cur = s % 2
    n_blocks = (xs_ref.shape[0] - DUMP_ROWS) // MOE_BLOCK

    def runs(tab_ref, sl, wait=False):
        _tile_runs(tab_ref, lambda e, o, d, size: pltpu.make_async_copy(
            sb.at[sl, pl.ds(o, size)], xs_ref.at[pl.ds(d, size)], sem.at[sl]), wait)

    def wait_tile(sl):
        for j in range(TILE_SORT // TILE_ROW):
            pltpu.make_async_copy(sb.at[sl, pl.ds(j * TILE_ROW, TILE_ROW)], xs_ref.at[pl.ds(0, TILE_ROW)],
                                  sem.at[sl]).wait()

    def blocks(wait):
        def body(b, c):
            cp = pltpu.make_async_copy(sb.at[0, pl.ds(0, MOE_BLOCK)],
                                       xs_ref.at[pl.ds(pl.multiple_of(b * MOE_BLOCK, MOE_BLOCK), MOE_BLOCK)],
                                       sem.at[0])
            if wait:
                cp.wait()
            else:
                cp.start()
            return c
        lax.fori_loop(na_ref[0], n_blocks + DUMP_ROWS // MOE_BLOCK, body, 0)

    @pl.when(s == 0)
    def _():
        sb[0] = jnp.zeros(sb.shape[1:], U32)
        runs(zt_ref, 0)
        blocks(False)
        runs(zt_ref, 0, wait=True)
        blocks(True)

    p01 = _slot_matrix(slot_ref[...], (1.0,) * TOP_K, TILE_SORT).astype(BF16)
    sb[cur] = _pack_bf16_pairs(lax.dot_general(p01, xn_ref[...].astype(BF16), TN, preferred_element_type=F32))
    runs(sc_ref, cur)

    @pl.when(s > 0)
    def _():
        wait_tile(1 - cur)

    @pl.when(s == ns - 1)
    def _():
        wait_tile(cur)


def _scatter_rows(xn, slot, runs, ztab, n_active, rows):
    n = xn.shape[0]
    tt = TILE_ROW
    return pl.pallas_call(
        _scatter_kernel,
        grid=(n // tt,),
        in_specs=[pl.BlockSpec((None, 1, 3 * N_RUNS), lambda s: (s, 0, 0), memory_space=pltpu.SMEM),
                  pl.BlockSpec(memory_space=pltpu.SMEM),
                  pl.BlockSpec(memory_space=pltpu.SMEM),
                  pl.BlockSpec((tt, LANES), lambda s: (s, 0)),
                  pl.BlockSpec((tt, D_MODEL), lambda s: (s, 0))],
        out_specs=pl.BlockSpec(memory_space=pl.ANY),
        out_shape=jax.ShapeDtypeStruct((rows + DUMP_ROWS, D_PACK), U32),
        scratch_shapes=[pltpu.VMEM((2, TILE_SORT, D_PACK), U32), pltpu.SemaphoreType.DMA((2,))],
        compiler_params=pltpu.CompilerParams(
            dimension_semantics=("arbitrary",), vmem_limit_bytes=VMEM_LIMIT),
        name="moe_scatter",
    )(runs, ztab, n_active, slot, xn)


def _expert_kernel(be_ref, na_ref, xs_ref, w1_ref, b1_ref, w2_ref, b2_ref, o_ref):
    del be_ref
    s = pl.program_id(0)

    @pl.when(s < na_ref[0])
    def _():
        xl, xr = _unpack_bf16_pairs(xs_ref[...])
        hdn = _mm(xl, w1_ref[:D_PACK, :]) + _mm(xr, w1_ref[D_PACK:, :]) + b1_ref[...]
        glu = jnp.minimum(hdn[:, :D_FF], SWIGLU_LIMIT)
        lin = jnp.clip(hdn[:, D_FF:], -SWIGLU_LIMIT, SWIGLU_LIMIT)
        act = glu * jax.nn.sigmoid(SWIGLU_ALPHA * glu) * (lin + 1.0)
        o_ref[...] = _pack_bf16_pairs(_mm(act.astype(BF16), w2_ref[...]) + b2_ref[...])

    @pl.when(s >= na_ref[0])
    def _():
        o_ref[...] = jnp.zeros_like(o_ref)


def _experts(xs, block_e, n_active, w):
    nb = block_e.shape[0]
    rows = nb * MOE_BLOCK
    grid_spec = pltpu.PrefetchScalarGridSpec(
        num_scalar_prefetch=2,
        grid=(nb,),
        in_specs=[
            pl.BlockSpec((MOE_BLOCK, D_PACK), lambda s, be, na: (s, 0)),
            pl.BlockSpec((None, D_MODEL, 2 * D_FF), lambda s, be, na: (be[s], 0, 0)),
            pl.BlockSpec((None, 1, 2 * D_FF), lambda s, be, na: (be[s], 0, 0)),
            pl.BlockSpec((None, D_FF, D_MODEL), lambda s, be, na: (be[s], 0, 0)),
            pl.BlockSpec((None, 1, D_MODEL), lambda s, be, na: (be[s], 0, 0)),
        ],
        out_specs=pl.BlockSpec((MOE_BLOCK, D_PACK), lambda s, be, na: (s, 0)),
    )
    return pl.pallas_call(
        _expert_kernel,
        grid_spec=grid_spec,
        out_shape=jax.ShapeDtypeStruct((rows, D_PACK), U32),
        compiler_params=pltpu.CompilerParams(
            dimension_semantics=("arbitrary",), vmem_limit_bytes=VMEM_LIMIT),
        name="moe_experts",
    )(block_e, n_active, xs, w["moe_w1"], w["moe_b1"], w["moe_w2"], w["moe_b2"])


def _combine_kernel(scur_ref, snxt_ref, slot_ref, h_ref, gate_ref, p_ref, npl_ref, pg_ref, pp_ref, nfin_ref,
                    os_ref, y_ref, gb, sem):
    s = pl.program_id(0)
    ns = pl.num_programs(0)
    slot = s % 2

    def fetch(sc_ref, sl):
        _tile_runs(sc_ref, lambda e, o, d, size: pltpu.make_async_copy(
            os_ref.at[pl.ds(pl.multiple_of(jnp.where(e == N_EXPERTS, 0, d), RUN_ALIGN), size)],
            gb.at[sl, pl.ds(o, size)], sem.at[sl]))

    @pl.when(s == 0)
    def _():
        fetch(scur_ref, 0)

    @pl.when(s + 1 < ns)
    def _():
        fetch(snxt_ref, 1 - slot)

    for j in range(TILE_SORT // TILE_ROW):
        pltpu.make_async_copy(os_ref.at[pl.ds(0, TILE_ROW)], gb.at[slot, pl.ds(j * TILE_ROW, TILE_ROW)],
                              sem.at[slot]).wait()

    gate = gate_ref[...]
    pm = _slot_matrix(slot_ref[...], [gate[:, kx:kx + 1] for kx in range(TOP_K)], TILE_SORT)
    tt = pm.shape[0]
    pst = jnp.concatenate(_split2(pm), axis=0)
    gl, gr = _unpack_bf16_pairs(gb[slot])
    moe = jnp.concatenate([_mm(pst, gl), _mm(pst, gr)], axis=1)
    h = h_ref[...] + moe[:tt] + moe[tt:]
    gt = jax.nn.sigmoid(_mm(_rms(h, npl_ref[...]).astype(BF16), pg_ref[...]))
    h = h + _mm(p_ref[...].astype(BF16), pp_ref[...]) * gt
    y_ref[...] = _rms(h, nfin_ref[...])


def _combine(h, gates, slot, runs, p, os_rows, w):
    n = h.shape[0]
    tt = TILE_ROW
    ns = n // tt

    def full(a):
        nd = a.ndim
        return pl.BlockSpec(a.shape, lambda s: (0,) * nd)

    consts = (w["norm_ple"], w["ple_gate"], w["ple_proj"], w["norm_final"])
    smem = lambda fn: pl.BlockSpec((None, 1, 3 * N_RUNS), fn, memory_space=pltpu.SMEM)
    return pl.pallas_call(
        _combine_kernel,
        grid=(ns,),
        in_specs=[smem(lambda s: (s, 0, 0)),
                  smem(lambda s: (jnp.minimum(s + 1, ns - 1), 0, 0)),
                  pl.BlockSpec((tt, LANES), lambda s: (s, 0)),
                  pl.BlockSpec((tt, D_MODEL), lambda s: (s, 0)),
                  pl.BlockSpec((tt, LANES), lambda s: (s, 0)),
                  pl.BlockSpec((tt, PLE_DIM), lambda s: (s, 0))]
        + [full(a) for a in consts]
        + [pl.BlockSpec(memory_space=pl.ANY)],
        out_specs=pl.BlockSpec((tt, D_MODEL), lambda s: (s, 0)),
        out_shape=jax.ShapeDtypeStruct((n, D_MODEL), F32),
        scratch_shapes=[pltpu.VMEM((2, TILE_SORT, D_PACK), U32), pltpu.SemaphoreType.DMA((2,))],
        compiler_params=pltpu.CompilerParams(
            dimension_semantics=("arbitrary",), vmem_limit_bytes=VMEM_LIMIT),
        name="moe_combine",
    )(runs, runs, slot, h, gates, p, *consts, os_rows)


def _prep_weights(norm_mix, w_in, a_ln_g, a_ln_b, a_ws, a_bs, b_conv, b_w0, b_w2, b_a0, b_a2, b_g2,
                  b_kk, b_ka, b_rk, b_gn_g, b_gn_b, w_out, norm_ffn, router_w, router_b, moe_w1,
                  moe_b1, moe_w2, moe_b2, norm_ple, ple_proj, ple_gate, norm_final):
    row = lambda a: a.reshape(1, -1).astype(F32)

    def lora_pad(m):
        z = jnp.zeros((2, 2 * LORA, GB), F32)
        z = z.at[0, :LORA].set(m[0]).at[1, LORA:].set(m[1])
        return z.astype(BF16)

    seg = jnp.arange(GB, dtype=I32) // HEAD_DIM
    rw = jnp.pad(router_w[0].astype(F32), ((0, 0), (0, LANES - N_EXPERTS)))
    rw_hi = rw.astype(BF16)
    return {
        "norm_mix": row(norm_mix[0]),
        "w_in": w_in[0].astype(BF16),
        "a_ln_g": row(a_ln_g[0]),
        "a_ln_b": row(a_ln_b[0]),
        "a_ws": a_ws[0].reshape(-1, CHUNK_A).astype(BF16),
        "a_bs": jnp.repeat(a_bs[0].T.astype(F32), HEAD_DIM, axis=1),
        "b_conv": b_conv[0].astype(F32),
        "b_w0": b_w0[0].astype(F32),
        "b_w2": lora_pad(b_w2[0]),
        "b_a0": b_a0[0].astype(F32),
        "b_a2": lora_pad(b_a2[0]),
        "b_g2": b_g2[0].astype(BF16),
        "b_kk": row(b_kk[0]),
        "b_ka": row(b_ka[0]),
        "b_rk": row(b_rk[0]),
        "bd": (seg[:, None] == seg[None, :]).astype(BF16),
        "b_gn_g": row(b_gn_g[0]),
        "b_gn_b": row(b_gn_b[0]),
        "w_out": w_out[0].astype(BF16),
        "norm_ffn": row(norm_ffn[0]),
        "router_wh": rw_hi,
        "router_wl": (rw - rw_hi.astype(F32)).astype(BF16),
        "router_b": jnp.pad(router_b[0].astype(F32), (0, LANES - N_EXPERTS)).reshape(1, LANES),
        "moe_w1": moe_w1[0].astype(BF16),
        "moe_b1": moe_b1[0].astype(F32).reshape(N_EXPERTS, 1, 2 * D_FF),
        "moe_w2": moe_w2[0].astype(BF16),
        "moe_b2": moe_b2[0].astype(F32).reshape(N_EXPERTS, 1, D_MODEL),
        "norm_ple": row(norm_ple[0]),
        "ple_gate": ple_gate[0].astype(BF16),
        "ple_proj": ple_proj[0].astype(BF16),
        "norm_final": row(norm_final),
    }


def _forward(x, p, w):
    bsz, t, _ = x.shape
    n = bsz * t
    ya, r, v, kn, lw0, lw1, k0, k1, b0, b1, g, bonus = _inproj(x, w)
    yf, yb = _wkv(r, v, kn, k0, b0, lw0, k1, b1, lw1)
    h, xn, ids, gates, rank, counts, before = _post(yf, yb, bonus, g, ya, x, w)

    nt = n // TILE_ROW
    counts = counts[0, :N_EXPERTS].astype(I32)
    before = before[:, 0, :N_EXPERTS].astype(I32)
    tile_cnt = jnp.concatenate([before[1:], counts[None]], axis=0) - before
    tile_pad = (tile_cnt + RUN_ALIGN - 1) // RUN_ALIGN * RUN_ALIGN
    tile_off = jnp.cumsum(tile_pad, axis=1) - tile_pad
    before_pad = jnp.cumsum(tile_pad, axis=0) - tile_pad
    padded = (jnp.sum(tile_pad, axis=0) + MOE_BLOCK - 1) // MOE_BLOCK * MOE_BLOCK
    pends = jnp.cumsum(padded)
    pstarts = pends - padded
    n_blocks = -(-(n * TOP_K + nt * N_EXPERTS * (RUN_ALIGN - 1)) // MOE_BLOCK) + N_EXPERTS
    block_start = jnp.arange(n_blocks, dtype=I32) * MOE_BLOCK
    block_e = jnp.minimum(jnp.sum(pends[None, :] <= block_start[:, None], axis=1), N_EXPERTS - 1).astype(I32)
    n_active = (pends[-1:] // MOE_BLOCK).astype(I32)
    used = jnp.sum(tile_pad, axis=1, keepdims=True)
    dump = n_blocks * MOE_BLOCK + (jnp.arange(nt, dtype=I32)[:, None] % 2) * TILE_ROW
    runs = jnp.concatenate([tile_off, used, pstarts[None, :] + before_pad, dump, tile_pad, TILE_SORT - used],
                           axis=1).reshape(nt, 1, 3 * N_RUNS).astype(I32)
    rel = jnp.repeat(tile_off - before, TILE_ROW, axis=0)
    slot = jnp.take_along_axis(rel, ids[:, :TOP_K], axis=1) + rank[:, :TOP_K]
    slot = jnp.pad(slot, ((0, 0), (0, LANES - TOP_K)), constant_values=-1).astype(I32)

    rows_used = jnp.sum(tile_pad, axis=0)
    tails = padded - rows_used
    zero1 = jnp.zeros((1,), I32)
    ztab = jnp.concatenate([jnp.zeros_like(padded), zero1, pstarts + rows_used, zero1, tails, zero1])
    ztab = ztab.reshape(1, 3 * N_RUNS).astype(I32)

    xs = _scatter_rows(xn, slot, runs, ztab, n_active, n_blocks * MOE_BLOCK)
    os_rows = _experts(xs, block_e, n_active, w)
    y = _combine(h, gates, slot, runs, p.reshape(n, PLE_DIM), os_rows, w)
    return y.reshape(bsz, t, D_MODEL)


def kernel(x_prompt, x_sample, p_prompt, p_sample, norm_mix, w_in, a_ln_g, a_ln_b, a_ws, a_bs, b_conv, b_w0, b_w2, b_a0, b_a2, b_g2, b_kk, b_ka, b_rk, b_gn_g, b_gn_b, w_out, norm_ffn, router_w, router_b, moe_w1, moe_b1, moe_w2, moe_b2, norm_ple, ple_proj, ple_gate, norm_final):
    assert norm_mix.shape[0] == 1, "single-layer trunk"
    w = _prep_weights(norm_mix, w_in, a_ln_g, a_ln_b, a_ws, a_bs, b_conv, b_w0, b_w2, b_a0, b_a2, b_g2,
                      b_kk, b_ka, b_rk, b_gn_g, b_gn_b, w_out, norm_ffn, router_w, router_b, moe_w1,
                      moe_b1, moe_w2, moe_b2, norm_ple, ple_proj, ple_gate, norm_final)
    y_prompt = _forward(x_prompt, p_prompt[0], w)
    y_sample = _forward(x_sample, p_sample[0], w)
    return (y_prompt, y_sample)
```

```python
import math

import jax
import jax.numpy as jnp
from jax import lax
from jax.experimental import pallas as pl
from jax.experimental.pallas import tpu as pltpu

F32 = jnp.float32
BF16 = jnp.bfloat16
I32 = jnp.int32
U32 = jnp.uint32

D_MODEL = 1024
D_PACK = D_MODEL // 2
HEAD_DIM = 64
GA = 512
GB = 512
N_PAIR = GB // 128
CHUNK_A = 128
LORA = 64
LORA_G = 128
B_CONV = 3 * GB + 4 * LORA + LORA_G
N_EXPERTS = 32
TOP_K = 4
D_FF = 1024
PLE_DIM = 256
SWIGLU_ALPHA = 1.702
SWIGLU_LIMIT = 7.0
EPS = 1e-6
GN_EPS = 64e-5
DECAY_SCALE = math.exp(-0.5)

LANES = 128
TILE_IN = 512
WKV_STEP = 256
WKV_CHUNK = 64
TILE_POST = 256
TILE_ROW = 256
MOE_BLOCK = 512
VMEM_LIMIT = 56 * 1024 * 1024

NT = (((1,), (1,)), ((), ()))
TN = (((0,), (0,)), ((), ()))


def _mm(a, b):
    return jnp.dot(a, b, preferred_element_type=F32)


def _split2(q):
    hi = q.astype(BF16)
    lo = (q - hi.astype(F32)).astype(BF16)
    return hi, lo


def _segsum(q, bd):
    hi, lo = _split2(q)
    return _mm(hi, bd) + _mm(lo, bd)


def _gelu(z):
    return 0.5 * z * (1.0 + lax.erf(z * (1.0 / math.sqrt(2.0))))


def _pack_bf16_pairs(a):
    half = a.shape[1] // 2
    u = lax.bitcast_convert_type(a.astype(BF16).astype(F32), U32)
    return (u[:, :half] & jnp.uint32(0xFFFF0000)) | (u[:, half:] >> jnp.uint32(16))


def _unpack_bf16_pairs(u):
    left = lax.bitcast_convert_type(u & jnp.uint32(0xFFFF0000), F32)
    right = lax.bitcast_convert_type(u << jnp.uint32(16), F32)
    return left.astype(BF16), right.astype(BF16)


def _rms(xv, g):
    ms = jnp.mean(xv * xv, axis=-1, keepdims=True)
    return xv * lax.rsqrt(ms + EPS) * g


def _inproj_kernel(x_ref, xp_ref, xn_ref, nm_ref, win_ref, lng_ref, lnb_ref, ws_ref, bs_ref,
                   conv_ref, w0_ref, w2_ref, a0_ref, a2_ref, g2_ref, kk_ref, ka_ref, rk_ref, bd_ref,
                   ya_ref, r_ref, v_ref, kn_ref, lw0_ref, lw1_ref, k0_ref, k1_ref, b0_ref, b1_ref,
                   g_ref, bonus_ref):
    tt = x_ref.shape[0]
    i = pl.program_id(1)
    last = pl.num_programs(1) - 1
    nm = nm_ref[...]

    xe = jnp.concatenate([x_ref[...], xp_ref[...], xn_ref[...]], axis=0)
    xe = _rms(xe, nm).astype(BF16)
    za = _mm(xe[:tt], win_ref[:, :2 * GA])
    ze = _mm(xe, win_ref[:, 2 * GA:])
    zb = ze[:tt]
    row_prev = jnp.where(i > 0, ze[tt + 7:tt + 8], 0.0)
    row_next = jnp.where(i < last, ze[tt + 8:tt + 9], 0.0)

    u = _gelu(za[:, :GA])
    v = _gelu(za[:, GA:])
    mu = jnp.mean(v, axis=-1, keepdims=True)
    var = jnp.mean(jnp.square(v - mu), axis=-1, keepdims=True)
    v = ((v - mu) * lax.rsqrt(var + EPS) * lng_ref[...] + lnb_ref[...]).astype(BF16)
    lane_head = lax.broadcasted_iota(I32, (CHUNK_A, GA), 1) // HEAD_DIM
    for c in range(tt // CHUNK_A):
        rows = slice(c * CHUNK_A, (c + 1) * CHUNK_A)
        o = _mm(ws_ref[...], v[rows])
        s = bs_ref[...]
        for h in range(GA // HEAD_DIM):
            s = s + jnp.where(lane_head == h, o[h * CHUNK_A:(h + 1) * CHUNK_A], 0.0)
        ya_ref[rows, :] = (u[rows] * s).astype(ya_ref.dtype)

    conv = conv_ref[...]
    trow = lax.broadcasted_iota(I32, (tt, 1), 0)
    z_prev = jnp.where(trow == 0, row_prev, pltpu.roll(zb, 1, axis=0))
    z_next = jnp.where(trow == tt - 1, row_next, pltpu.roll(zb, tt - 1, axis=0))
    zc = z_prev * conv[0:1] + zb * conv[1:2] + z_next * conv[2:3]
    r = zc[:, :GB]
    k = zc[:, GB:2 * GB]
    vv = zc[:, 2 * GB:3 * GB]
    o0 = 3 * GB
    xw = jnp.tanh(zc[:, o0:o0 + 2 * LORA]).astype(BF16)
    xa = zc[:, o0 + 2 * LORA:o0 + 4 * LORA].astype(BF16)
    xg = jax.nn.sigmoid(zc[:, o0 + 4 * LORA:]).astype(BF16)
    bd = bd_ref[...]
    kk = k * kk_ref[...]
    kk = kk / jnp.maximum(jnp.sqrt(_segsum(kk * kk, bd)), 1e-12)
    ka = ka_ref[...]
    lw_refs = (lw0_ref, lw1_ref)
    k_refs = (k0_ref, k1_ref)
    b_refs = (b0_ref, b1_ref)
    ksum = None
    for d in range(2):
        yw = w0_ref[d:d + 1, :] + _mm(xw, w2_ref[d])
        lw = -DECAY_SCALE * jax.nn.sigmoid(yw)
        a = jax.nn.sigmoid(a0_ref[d:d + 1, :] + _mm(xa, a2_ref[d]))
        kd = k * (1.0 + (a - 1.0) * ka)
        bb = kk * a
        ksum = kd if ksum is None else ksum + kd
        for p in range(N_PAIR):
            ls = slice(p * LANES, (p + 1) * LANES)
            lw_refs[d][p] = lw[:, ls]
            k_refs[d][p] = kd[:, ls].astype(k0_ref.dtype)
            b_refs[d][p] = bb[:, ls].astype(b0_ref.dtype)
    for p in range(N_PAIR):
        ls = slice(p * LANES, (p + 1) * LANES)
        r_ref[p] = r[:, ls].astype(r_ref.dtype)
        v_ref[p] = vv[:, ls].astype(v_ref.dtype)
        kn_ref[p] = kk[:, ls].astype(kn_ref.dtype)
    g_ref[...] = _mm(xg, g2_ref[...])
    bonus_ref[...] = _segsum(r * ksum * rk_ref[...], bd) * vv


def _inproj(x, w):
    bsz, t, _ = x.shape
    tt = min(TILE_IN, t)
    nt = t // tt
    t8 = tt // 8

    def full(a):
        nd = a.ndim
        return pl.BlockSpec(a.shape, lambda b, i: (0,) * nd)

    consts = (w["norm_mix"], w["w_in"], w["a_ln_g"], w["a_ln_b"], w["a_ws"], w["a_bs"], w["b_conv"],
              w["b_w0"], w["b_w2"], w["b_a0"], w["b_a2"], w["b_g2"], w["b_kk"], w["b_ka"], w["b_rk"],
              w["bd"])
    in_specs = [
        pl.BlockSpec((None, tt, D_MODEL), lambda b, i: (b, i, 0)),
        pl.BlockSpec((None, 8, D_MODEL), lambda b, i: (b, jnp.maximum(i * t8 - 1, 0), 0)),
        pl.BlockSpec((None, 8, D_MODEL), lambda b, i: (b, jnp.minimum((i + 1) * t8, t // 8 - 1), 0)),
    ] + [full(a) for a in consts]
    pair = lambda dt: jax.ShapeDtypeStruct((bsz, N_PAIR, t, LANES), dt)
    flat = lambda dt: jax.ShapeDtypeStruct((bsz, t, GB), dt)
    pair_spec = pl.BlockSpec((None, N_PAIR, tt, LANES), lambda b, i: (b, 0, i, 0))
    flat_spec = pl.BlockSpec((None, tt, GB), lambda b, i: (b, i, 0))
    out_shape = (flat(BF16),
                 pair(BF16), pair(BF16), pair(BF16),
                 pair(F32), pair(F32),
                 pair(BF16), pair(BF16), pair(BF16), pair(BF16),
                 flat(F32), flat(F32))
    out_specs = (flat_spec,) + (pair_spec,) * 9 + (flat_spec, flat_spec)
    return pl.pallas_call(
        _inproj_kernel,
        grid=(bsz, nt),
        in_specs=in_specs,
        out_specs=out_specs,
        out_shape=out_shape,
        compiler_params=pltpu.CompilerParams(
            dimension_semantics=("parallel", "parallel"), vmem_limit_bytes=VMEM_LIMIT),
        name="inproj",
    )(x, x, x, *consts)


def _wkv_prep(p, refs, reverse, shared):
    r_ref, v_ref, kn_ref, k_ref, b_ref, lw_ref = refs
    row, col, m64 = shared
    strict = m64 & ((col > row) if reverse else (col < row))
    incl = m64 & ((col >= row) if reverse else (col <= row))

    lw = lw_ref[p]
    l1, l2 = _split2(lw)
    tri = jnp.where(incl, 1.0, 0.0).astype(BF16)
    cum = _mm(tri, l1) + _mm(tri, l2)
    nc = WKV_STEP // WKV_CHUNK
    end = 0 if reverse else WKV_CHUNK - 1
    tot = jnp.concatenate(
        [jnp.broadcast_to(cum[ck * WKV_CHUNK + end:ck * WKV_CHUNK + end + 1], (WKV_CHUNK, LANES))
         for ck in range(nc)], axis=0)
    rr = r_ref[p].astype(F32)
    kn = kn_ref[p].astype(F32)
    kd = k_ref[p].astype(F32)
    bb = b_ref[p].astype(F32)
    winv = jnp.exp(-cum)
    wd = jnp.exp(tot - cum)
    return dict(
        strict=strict, incl=incl, wtot=jnp.exp(tot),
        rt=(rr * jnp.exp(cum)).astype(BF16), at=(-kn * jnp.exp(cum - lw)).astype(BF16),
        bt=(bb * winv).astype(BF16), kt=(kd * winv).astype(BF16),
        bh=(bb * wd).astype(BF16), kh=(kd * wd).astype(BF16), vb=v_ref[p].astype(BF16))


def _wkv_chains(chains, s_scr, masks):
    c8, c16, c32, cb, eye_c, eye_h, blk = masks
    n = WKV_STEP
    nc = n // WKV_CHUNK
    dot_nt = lambda a, b: lax.dot_general(a, b, NT, preferred_element_type=F32)
    dot_tn = lambda a, b: lax.dot_general(a, b, TN, preferred_element_type=F32)
    each = lambda fn: [fn(c) for c in chains]

    for c in chains:
        sl = slice(c["hh"] * HEAD_DIM, (c["hh"] + 1) * HEAD_DIM)
        pr = c["prep"]
        c.update(sl=sl, a=pr["at"][:, sl], r=pr["rt"][:, sl], b=pr["bt"][:, sl], k=pr["kt"][:, sl],
                 v=pr["vb"][:, sl], strict=pr["strict"], incl=pr["incl"])
    lab = each(lambda c: jnp.where(c["strict"], dot_nt(c["a"], c["b"]), 0.0))
    lak = each(lambda c: jnp.where(c["strict"], dot_nt(c["a"], c["k"]), 0.0).astype(BF16))
    mrb = each(lambda c: jnp.where(c["incl"], dot_nt(c["r"], c["b"]), 0.0).astype(BF16))
    mrk = each(lambda c: jnp.where(c["incl"], dot_nt(c["r"], c["k"]), 0.0).astype(BF16))

    lab_c = [sum(jnp.where(cb == ck, l[ck * WKV_CHUNK:(ck + 1) * WKV_CHUNK], 0.0) for ck in range(nc))
             for l in lab]
    bdiag = lambda zb: jnp.concatenate([zb] * nc, axis=0) * blk
    l8 = [jnp.where(c8, l, 0.0) for l in lab_c]
    p1 = [l.astype(BF16) for l in l8]
    p2 = [_mm(q, bdiag(q)).astype(BF16) for q in p1]
    p2d = [bdiag(q) for q in p2]
    x = [eye_c + l for l in l8]
    p4d = [bdiag(_mm(q, qd).astype(BF16)) for q, qd in zip(p2, p2d)]
    x = [xi + _mm(xi.astype(BF16), qd) for xi, qd in zip(x, p2d)]
    x = [xi + _mm(xi.astype(BF16), qd) for xi, qd in zip(x, p4d)]
    prev = c8
    for cur in (c16, c32, None):
        lvl = jnp.logical_not(prev) if cur is None else cur & jnp.logical_not(prev)
        xb = [xi.astype(BF16) for xi in x]
        t = [_mm(jnp.where(lvl, l, 0.0).astype(BF16), bdiag(xi)).astype(BF16) for l, xi in zip(lab_c, xb)]
        x = [xi + _mm(xbi, bdiag(ti)) for xi, xbi, ti in zip(x, xb, t)]
        prev = cur
    tb = [bdiag(xi.astype(BF16)) for xi in x]

    x1 = [_mm(l, c["v"]).astype(BF16) for l, c in zip(lak, chains)]
    wu = [_mm(t, c["a"]).astype(BF16) for t, c in zip(tb, chains)]
    uv = [_mm(t, xi).astype(BF16) for t, xi in zip(tb, x1)]
    q = [(c["r"].astype(F32) + _mm(m, w)).astype(BF16) for c, m, w in zip(chains, mrb, wu)]
    yl = [_mm(m, u) + _mm(mk, c["v"]) for m, u, mk, c in zip(mrb, uv, mrk, chains)]

    g, h = [], []
    for i, c in enumerate(chains):
        gi, hi = [], []
        for ck in range(nc):
            rs = slice(ck * WKV_CHUNK, (ck + 1) * WKV_CHUNK)
            b_c = c["prep"]["bh"][rs, c["sl"]]
            k_c = c["prep"]["kh"][rs, c["sl"]]
            wrow = c["prep"]["wtot"][ck * WKV_CHUNK:ck * WKV_CHUNK + 1, c["sl"]]
            gi.append((dot_tn(b_c, wu[i][rs])
                       + jnp.where(eye_h, jnp.broadcast_to(wrow, (HEAD_DIM, HEAD_DIM)), 0.0)).astype(BF16))
            hi.append(dot_tn(b_c, uv[i][rs]) + dot_tn(k_c, c["v"][rs]))
        g.append(gi)
        h.append(hi)

    s = [s_scr[c["d"], 2 * c["p"] + c["hh"]] for c in chains]
    for step in range(nc):
        for i, c in enumerate(chains):
            ck = nc - 1 - step if c["reverse"] else step
            rs = slice(ck * WKV_CHUNK, (ck + 1) * WKV_CHUNK)
            sb = s[i].astype(BF16)
            c["y_ref"][c["p"], rs, c["sl"]] = _mm(q[i][rs], sb) + yl[i][rs]
            s[i] = _mm(g[i][ck], sb) + h[i][ck]
    for i, c in enumerate(chains):
        s_scr[c["d"], 2 * c["p"] + c["hh"]] = s[i]


def _wkv_kernel(rf_ref, vf_ref, nf_ref, kf_ref, bf_ref, lf_ref,
                rb_ref, vb_ref, nb_ref, kb_ref, bb_ref, lb_ref, yf_ref, yb_ref, s_scr):
    n = WKV_STEP

    @pl.when(pl.program_id(1) == 0)
    def _():
        s_scr[...] = jnp.zeros_like(s_scr)

    def pair(p, carry):
        row = lax.broadcasted_iota(I32, (n, n), 0)
        col = lax.broadcasted_iota(I32, (n, n), 1)
        m64 = (row >> 6) == (col >> 6)
        er = lax.broadcasted_iota(I32, (HEAD_DIM, HEAD_DIM), 0)
        ec = lax.broadcasted_iota(I32, (HEAD_DIM, HEAD_DIM), 1)
        shared = (row, col, m64)
        rc = lax.broadcasted_iota(I32, (WKV_CHUNK, n), 0)
        lc = lax.broadcasted_iota(I32, (WKV_CHUNK, n), 1)
        li = lc & (WKV_CHUNK - 1)
        samec = lambda log2: (rc >> log2) == (li >> log2)
        masks = (samec(3), samec(4), samec(5), lc >> 6, jnp.where(rc == li, 1.0, 0.0).astype(F32), er == ec,
                 jnp.where(m64, 1.0, 0.0).astype(BF16))
        chains = []
        for pp in range(N_PAIR):
            fwd = _wkv_prep(pp, (rf_ref, vf_ref, nf_ref, kf_ref, bf_ref, lf_ref), False, shared)
            bwd = _wkv_prep(pp, (rb_ref, vb_ref, nb_ref, kb_ref, bb_ref, lb_ref), True, shared)
            chains += [dict(prep=pr, hh=hh, d=d, p=pp, reverse=rev, y_ref=y_ref)
                       for pr, d, rev, y_ref in ((fwd, 0, False, yf_ref), (bwd, 1, True, yb_ref))
                       for hh in range(2)]
        _wkv_chains(chains, s_scr, masks)
        return carry

    pair(0, 0)


def _wkv(r, v, kn, k0, b0, lw0, k1, b1, lw1):
    bsz, _, t, _ = r.shape
    assert t % WKV_STEP == 0
    nj = t // WKV_STEP
    fwd = pl.BlockSpec((None, N_PAIR, WKV_STEP, LANES), lambda bi, j: (bi, 0, j, 0))
    bwd = pl.BlockSpec((None, N_PAIR, WKV_STEP, LANES), lambda bi, j: (bi, 0, nj - 1 - j, 0))
    out = jax.ShapeDtypeStruct((bsz, N_PAIR, t, LANES), F32)
    return pl.pallas_call(
        _wkv_kernel,
        grid=(bsz, nj),
        in_specs=[fwd] * 6 + [bwd] * 6,
        out_specs=(fwd, bwd),
        out_shape=(out, out),
        scratch_shapes=[pltpu.VMEM((2, 2 * N_PAIR, HEAD_DIM, HEAD_DIM), F32)],
        compiler_params=pltpu.CompilerParams(
            dimension_semantics=("parallel", "arbitrary"), vmem_limit_bytes=VMEM_LIMIT),
        name="wkv",
    )(r, v, kn, k0, b0, lw0, r, v, kn, k1, b1, lw1)


def _post_kernel(yf_ref, yb_ref, bonus_ref, g_ref, ya_ref, x_ref, gng_ref, gnb_ref, bd_ref, wout_ref,
                 nf_ref, rwh_ref, rwl_ref, rb_ref,
                 h_ref, xn_ref, ids_ref, gate_ref, rank_ref, cnt_ref, before_ref, run_scr):
    tt = x_ref.shape[0]
    step = pl.program_id(0)

    @pl.when(step == 0)
    def _():
        run_scr[...] = jnp.zeros_like(run_scr)

    y = jnp.concatenate([yf_ref[p] + yb_ref[p] for p in range(N_PAIR)], axis=1)
    bd = bd_ref[...]
    inv = 1.0 / HEAD_DIM
    mu = _segsum(y, bd) * inv
    yc = y - mu
    var = _segsum(yc * yc, bd) * inv
    yn = yc * lax.rsqrt(var + GN_EPS) * gng_ref[...] + gnb_ref[...] + bonus_ref[...]
    ybm = (yn * g_ref[...]).astype(BF16)
    h = x_ref[...] + _mm(ya_ref[...], wout_ref[:GA, :]) + _mm(ybm, wout_ref[GA:, :])
    h_ref[...] = h
    xn = _rms(h, nf_ref[...])
    xn_ref[...] = xn

    lane = lax.broadcasted_iota(I32, (tt, LANES), 1)
    xh, xl = _split2(xn)
    logits = _mm(xh, rwh_ref[...]) + _mm(xl, rwh_ref[...]) + _mm(xh, rwl_ref[...])
    logits = jnp.where(lane < N_EXPERTS, logits + rb_ref[...], -jnp.inf)
    vals, ids, sel = [], [], jnp.zeros((tt, LANES), F32)
    cur = logits
    for _ in range(TOP_K):
        m = jnp.max(cur, axis=-1, keepdims=True)
        idx = jnp.min(jnp.where(cur == m, lane, LANES), axis=-1, keepdims=True)
        hit = lane == idx
        vals.append(m)
        ids.append(idx)
        sel = sel + jnp.where(hit, 1.0, 0.0)
        cur = jnp.where(hit, -jnp.inf, cur)
    es = [jnp.exp(vv - vals[0]) for vv in vals]
    den = es[0] + es[1] + es[2] + es[3]

    rt = lax.broadcasted_iota(I32, (tt, tt), 0)
    ct = lax.broadcasted_iota(I32, (tt, tt), 1)
    before = jnp.where(ct < rt, 1.0, 0.0).astype(BF16)
    before_ref[...] = run_scr[...]
    cnt = _mm(before, sel.astype(BF16)) + run_scr[...]
    ids_o = jnp.zeros((tt, LANES), I32)
    gate_o = jnp.zeros((tt, LANES), F32)
    rank_o = jnp.zeros((tt, LANES), F32)
    for kx in range(TOP_K):
        rk = jnp.sum(jnp.where(lane == ids[kx], cnt, 0.0), axis=-1, keepdims=True)
        ids_o = jnp.where(lane == kx, ids[kx], ids_o)
        gate_o = jnp.where(lane == kx, es[kx] / den, gate_o)
        rank_o = jnp.where(lane == kx, rk, rank_o)
    ids_ref[...] = ids_o
    gate_ref[...] = gate_o
    rank_ref[...] = rank_o.astype(I32)
    run = run_scr[...] + jnp.sum(sel, axis=0, keepdims=True)
    run_scr[...] = run
    cnt_ref[...] = run


def _post(yf, yb, bonus, g, ya, x, w):
    bsz, t, _ = x.shape
    n = bsz * t
    tt = min(TILE_POST, t)
    nt = t // tt

    def full(a):
        nd = a.ndim
        return pl.BlockSpec(a.shape, lambda s: (0,) * nd)

    consts = (w["b_gn_g"], w["b_gn_b"], w["bd"], w["w_out"], w["norm_ffn"], w["router_wh"], w["router_wl"],
              w["router_b"])
    pair_spec = pl.BlockSpec((None, N_PAIR, tt, LANES), lambda s: (s // nt, 0, s % nt, 0))
    tok3 = lambda width: pl.BlockSpec((None, tt, width), lambda s: (s // nt, s % nt, 0))
    tok2 = lambda width: pl.BlockSpec((tt, width), lambda s: (s, 0))
    in_specs = [pair_spec, pair_spec, tok3(GB), tok3(GB), tok3(GA), tok3(D_MODEL)] + [full(a) for a in consts]
    out_shape = (jax.ShapeDtypeStruct((n, D_MODEL), F32),
                 jax.ShapeDtypeStruct((n, D_MODEL), F32),
                 jax.ShapeDtypeStruct((n, LANES), I32),
                 jax.ShapeDtypeStruct((n, LANES), F32),
                 jax.ShapeDtypeStruct((n, LANES), I32),
                 jax.ShapeDtypeStruct((1, LANES), F32),
                 jax.ShapeDtypeStruct((bsz * nt, 1, LANES), F32))
    out_specs = (tok2(D_MODEL), tok2(D_MODEL), tok2(LANES), tok2(LANES), tok2(LANES),
                 pl.BlockSpec((1, LANES), lambda s: (0, 0)),
                 pl.BlockSpec((None, 1, LANES), lambda s: (s, 0, 0)))
    return pl.pallas_call(
        _post_kernel,
        grid=(bsz * nt,),
        in_specs=in_specs,
        out_specs=out_specs,
        out_shape=out_shape,
        scratch_shapes=[pltpu.VMEM((1, LANES), F32)],
        compiler_params=pltpu.CompilerParams(
            dimension_semantics=("arbitrary",), vmem_limit_bytes=VMEM_LIMIT),
        name="post_router",
    )(yf, yb, bonus, g, ya, x, *consts)


RUN_ALIGN = 8
RUN_SIZES = tuple(1 << i for i in range(8, 2, -1))
RUN_SMALL = 64
TILE_SORT = TILE_ROW * TOP_K + N_EXPERTS * RUN_ALIGN
N_RUNS = N_EXPERTS + 1
DUMP_ROWS = 2 * TILE_ROW


def _tile_runs(sc_ref, copy, wait=False):
    def per_run(e, c):
        o0 = sc_ref[0, e]
        d0 = sc_ref[0, N_RUNS + e]
        ln = sc_ref[0, 2 * N_RUNS + e]

        def pieces(sizes, o, d):
            for size in sizes:
                bit = ln & size

                @pl.when(bit != 0)
                def _():
                    cp = copy(e, pl.multiple_of(o, RUN_ALIGN), pl.multiple_of(d, RUN_ALIGN), size)
                    if wait:
                        cp.wait()
                    else:
                        cp.start()

                o = o + bit
                d = d + bit

        big = tuple(sz for sz in RUN_SIZES if sz >= RUN_SMALL)
        small = tuple(sz for sz in RUN_SIZES if sz < RUN_SMALL)

        @pl.when(ln >= RUN_SMALL)
        def _():
            pieces(big, o0, d0)

        skip = ln & ~(RUN_SMALL - 1)
        pieces(small, o0 + skip, d0 + skip)
        return c

    lax.fori_loop(0, N_RUNS, per_run, 0)


def _slot_matrix(slot, vals, width):
    tt = slot.shape[0]
    lane = lax.broadcasted_iota(I32, (tt, width), 1)
    m = jnp.zeros((tt, width), F32)
    for kx in range(TOP_K):
        m = m + jnp.where(lane == slot[:, kx:kx + 1], vals[kx], 0.0)
    return m


def _scatter_kernel(sc_ref, zt_ref, na_ref, slot_ref, xn_ref, xs_ref, sb, sem):
    s = pl.program_id(0)
    ns = pl.num_programs(0)
    cur = s % 2
    n_blocks = (xs_ref.shape[0] - DUMP_ROWS) // MOE_BLOCK

    def runs(tab_ref, sl, wait=False):
        _tile_runs(tab_ref, lambda e, o, d, size: pltpu.make_async_copy(
            sb.at[sl, pl.ds(o, size)], xs_ref.at[pl.ds(d, size)], sem.at[sl]), wait)

    def wait_tile(sl):
        for j in range(TILE_SORT // TILE_ROW):
            pltpu.make_async_copy(sb.at[sl, pl.ds(j * TILE_ROW, TILE_ROW)], xs_ref.at[pl.ds(0, TILE_ROW)],
                                  sem.at[sl]).wait()

    def blocks(wait):
        def body(b, c):
            cp = pltpu.make_async_copy(sb.at[0, pl.ds(0, MOE_BLOCK)],
                                       xs_ref.at[pl.ds(pl.multiple_of(b * MOE_BLOCK, MOE_BLOCK), MOE_BLOCK)],
                                       sem.at[0])
            if wait:
                cp.wait()
            else:
                cp.start()
            return c
        lax.fori_loop(na_ref[0], n_blocks + DUMP_ROWS // MOE_BLOCK, body, 0)

    @pl.when(s == 0)
    def _():
        sb[0] = jnp.zeros(sb.shape[1:], U32)
        runs(zt_ref, 0)
        blocks(False)
        runs(zt_ref, 0, wait=True)
        blocks(True)

    p01 = _slot_matrix(slot_ref[...], (1.0,) * TOP_K, TILE_SORT).astype(BF16)
    sb[cur] = _pack_bf16_pairs(lax.dot_general(p01, xn_ref[...].astype(BF16), TN, preferred_element_type=F32))
    runs(sc_ref, cur)

    @pl.when(s > 0)
    def _():
        wait_tile(1 - cur)

    @pl.when(s == ns - 1)
    def _():
        wait_tile(cur)


def _scatter_rows(xn, slot, runs, ztab, n_active, rows):
    n = xn.shape[0]
    tt = TILE_ROW
    return pl.pallas_call(
        _scatter_kernel,
        grid=(n // tt,),
        in_specs=[pl.BlockSpec((None, 1, 3 * N_RUNS), lambda s: (s, 0, 0), memory_space=pltpu.SMEM),
                  pl.BlockSpec(memory_space=pltpu.SMEM),
                  pl.BlockSpec(memory_space=pltpu.SMEM),
                  pl.BlockSpec((tt, LANES), lambda s: (s, 0)),
                  pl.BlockSpec((tt, D_MODEL), lambda s: (s, 0))],
        out_specs=pl.BlockSpec(memory_space=pl.ANY),
        out_shape=jax.ShapeDtypeStruct((rows + DUMP_ROWS, D_PACK), U32),
        scratch_shapes=[pltpu.VMEM((2, TILE_SORT, D_PACK), U32), pltpu.SemaphoreType.DMA((2,))],
        compiler_params=pltpu.CompilerParams(
            dimension_semantics=("arbitrary",), vmem_limit_bytes=VMEM_LIMIT),
        name="moe_scatter",
    )(runs, ztab, n_active, slot, xn)


def _expert_kernel(be_ref, na_ref, xs_ref, w1_ref, b1_ref, w2_ref, b2_ref, o_ref):
    del be_ref
    s = pl.program_id(0)

    @pl.when(s < na_ref[0])
    def _():
        xl, xr = _unpack_bf16_pairs(xs_ref[...])
        hdn = _mm(xl, w1_ref[:D_PACK, :]) + _mm(xr, w1_ref[D_PACK:, :]) + b1_ref[...]
        glu = jnp.minimum(hdn[:, :D_FF], SWIGLU_LIMIT)
        lin = jnp.clip(hdn[:, D_FF:], -SWIGLU_LIMIT, SWIGLU_LIMIT)
        act = glu * jax.nn.sigmoid(SWIGLU_ALPHA * glu) * (lin + 1.0)
        o_ref[...] = _pack_bf16_pairs(_mm(act.astype(BF16), w2_ref[...]) + b2_ref[...])

    @pl.when(s >= na_ref[0])
    def _():
        o_ref[...] = jnp.zeros_like(o_ref)


def _experts(xs, block_e, n_active, w):
    nb = block_e.shape[0]
    rows = nb * MOE_BLOCK
    grid_spec = pltpu.PrefetchScalarGridSpec(
        num_scalar_prefetch=2,
        grid=(nb,),
        in_specs=[
            pl.BlockSpec((MOE_BLOCK, D_PACK), lambda s, be, na: (s, 0)),
            pl.BlockSpec((None, D_MODEL, 2 * D_FF), lambda s, be, na: (be[s], 0, 0)),
            pl.BlockSpec((None, 1, 2 * D_FF), lambda s, be, na: (be[s], 0, 0)),
            pl.BlockSpec((None, D_FF, D_MODEL), lambda s, be, na: (be[s], 0, 0)),
            pl.BlockSpec((None, 1, D_MODEL), lambda s, be, na: (be[s], 0, 0)),
        ],
        out_specs=pl.BlockSpec((MOE_BLOCK, D_PACK), lambda s, be, na: (s, 0)),
    )
    return pl.pallas_call(
        _expert_kernel,
        grid_spec=grid_spec,
        out_shape=jax.ShapeDtypeStruct((rows, D_PACK), U32),
        compiler_params=pltpu.CompilerParams(
            dimension_semantics=("arbitrary",), vmem_limit_bytes=VMEM_LIMIT),
        name="moe_experts",
    )(block_e, n_active, xs, w["moe_w1"], w["moe_b1"], w["moe_w2"], w["moe_b2"])


def _combine_kernel(scur_ref, snxt_ref, slot_ref, h_ref, gate_ref, p_ref, npl_ref, pg_ref, pp_ref, nfin_ref,
                    os_ref, y_ref, gb, sem):
    s = pl.program_id(0)
    ns = pl.num_programs(0)
    slot = s % 2

    def fetch(sc_ref, sl):
        _tile_runs(sc_ref, lambda e, o, d, size: pltpu.make_async_copy(
            os_ref.at[pl.ds(pl.multiple_of(jnp.where(e == N_EXPERTS, 0, d), RUN_ALIGN), size)],
            gb.at[sl, pl.ds(o, size)], sem.at[sl]))

    @pl.when(s == 0)
    def _():
        fetch(scur_ref, 0)

    @pl.when(s + 1 < ns)
    def _():
        fetch(snxt_ref, 1 - slot)

    for j in range(TILE_SORT // TILE_ROW):
        pltpu.make_async_copy(os_ref.at[pl.ds(0, TILE_ROW)], gb.at[slot, pl.ds(j * TILE_ROW, TILE_ROW)],
                              sem.at[slot]).wait()

    gate = gate_ref[...]
    pm = _slot_matrix(slot_ref[...], [gate[:, kx:kx + 1] for kx in range(TOP_K)], TILE_SORT)
    tt = pm.shape[0]
    pst = jnp.concatenate(_split2(pm), axis=0)
    gl, gr = _unpack_bf16_pairs(gb[slot])
    moe = jnp.concatenate([_mm(pst, gl), _mm(pst, gr)], axis=1)
    h = h_ref[...] + moe[:tt] + moe[tt:]
    gt = jax.nn.sigmoid(_mm(_rms(h, npl_ref[...]).astype(BF16), pg_ref[...]))
    h = h + _mm(p_ref[...].astype(BF16), pp_ref[...]) * gt
    y_ref[...] = _rms(h, nfin_ref[...])


def _combine(h, gates, slot, runs, p, os_rows, w):
    n = h.shape[0]
    tt = TILE_ROW
    ns = n // tt

    def full(a):
        nd = a.ndim
        return pl.BlockSpec(a.shape, lambda s: (0,) * nd)

    consts = (w["norm_ple"], w["ple_gate"], w["ple_proj"], w["norm_final"])
    smem = lambda fn: pl.BlockSpec((None, 1, 3 * N_RUNS), fn, memory_space=pltpu.SMEM)
    return pl.pallas_call(
        _combine_kernel,
        grid=(ns,),
        in_specs=[smem(lambda s: (s, 0, 0)),
                  smem(lambda s: (jnp.minimum(s + 1, ns - 1), 0, 0)),
                  pl.BlockSpec((tt, LANES), lambda s: (s, 0)),
                  pl.BlockSpec((tt, D_MODEL), lambda s: (s, 0)),
                  pl.BlockSpec((tt, LANES), lambda s: (s, 0)),
                  pl.BlockSpec((tt, PLE_DIM), lambda s: (s, 0))]
        + [full(a) for a in consts]
        + [pl.BlockSpec(memory_space=pl.ANY)],
        out_specs=pl.BlockSpec((tt, D_MODEL), lambda s: (s, 0)),
        out_shape=jax.ShapeDtypeStruct((n, D_MODEL), F32),
        scratch_shapes=[pltpu.VMEM((2, TILE_SORT, D_PACK), U32), pltpu.SemaphoreType.DMA((2,))],
        compiler_params=pltpu.CompilerParams(
            dimension_semantics=("arbitrary",), vmem_limit_bytes=VMEM_LIMIT),
        name="moe_combine",
    )(runs, runs, slot, h, gates, p, *consts, os_rows)


def _prep_weights(norm_mix, w_in, a_ln_g, a_ln_b, a_ws, a_bs, b_conv, b_w0, b_w2, b_a0, b_a2, b_g2,
                  b_kk, b_ka, b_rk, b_gn_g, b_gn_b, w_out, norm_ffn, router_w, router_b, moe_w1,
                  moe_b1, moe_w2, moe_b2, norm_ple, ple_proj, ple_gate, norm_final):
    row = lambda a: a.reshape(1, -1).astype(F32)

    def lora_pad(m):
        z = jnp.zeros((2, 2 * LORA, GB), F32)
        z = z.at[0, :LORA].set(m[0]).at[1, LORA:].set(m[1])
        return z.astype(BF16)

    seg = jnp.arange(GB, dtype=I32) // HEAD_DIM
    rw = jnp.pad(router_w[0].astype(F32), ((0, 0), (0, LANES - N_EXPERTS)))
    rw_hi = rw.astype(BF16)
    return {
        "norm_mix": row(norm_mix[0]),
        "w_in": w_in[0].astype(BF16),
        "a_ln_g": row(a_ln_g[0]),
        "a_ln_b": row(a_ln_b[0]),
        "a_ws": a_ws[0].reshape(-1, CHUNK_A).astype(BF16),
        "a_bs": jnp.repeat(a_bs[0].T.astype(F32), HEAD_DIM, axis=1),
        "b_conv": b_conv[0].astype(F32),
        "b_w0": b_w0[0].astype(F32),
        "b_w2": lora_pad(b_w2[0]),
        "b_a0": b_a0[0].astype(F32),
        "b_a2": lora_pad(b_a2[0]),
        "b_g2": b_g2[0].astype(BF16),
        "b_kk": row(b_kk[0]),
        "b_ka": row(b_ka[0]),
        "b_rk": row(b_rk[0]),
        "bd": (seg[:, None] == seg[None, :]).astype(BF16),
        "b_gn_g": row(b_gn_g[0]),
        "b_gn_b": row(b_gn_b[0]),
        "w_out": w_out[0].astype(BF16),
        "norm_ffn": row(norm_ffn[0]),
        "router_wh": rw_hi,
        "router_wl": (rw - rw_hi.astype(F32)).astype(BF16),
        "router_b": jnp.pad(router_b[0].astype(F32), (0, LANES - N_EXPERTS)).reshape(1, LANES),
        "moe_w1": moe_w1[0].astype(BF16),
        "moe_b1": moe_b1[0].astype(F32).reshape(N_EXPERTS, 1, 2 * D_FF),
        "moe_w2": moe_w2[0].astype(BF16),
        "moe_b2": moe_b2[0].astype(F32).reshape(N_EXPERTS, 1, D_MODEL),
        "norm_ple": row(norm_ple[0]),
        "ple_gate": ple_gate[0].astype(BF16),
        "ple_proj": ple_proj[0].astype(BF16),
        "norm_final": row(norm_final),
    }


def _forward(x, p, w):
    bsz, t, _ = x.shape
    n = bsz * t
    ya, r, v, kn, lw0, lw1, k0, k1, b0, b1, g, bonus = _inproj(x, w)
    yf, yb = _wkv(r, v, kn, k0, b0, lw0, k1, b1, lw1)
    h, xn, ids, gates, rank, counts, before = _post(yf, yb, bonus, g, ya, x, w)

    nt = n // TILE_ROW
    counts = counts[0, :N_EXPERTS].astype(I32)
    before = before[:, 0, :N_EXPERTS].astype(I32)
    tile_cnt = jnp.concatenate([before[1:], counts[None]], axis=0) - before
    tile_pad = (tile_cnt + RUN_ALIGN - 1) // RUN_ALIGN * RUN_ALIGN
    tile_off = jnp.cumsum(tile_pad, axis=1) - tile_pad
    before_pad = jnp.cumsum(tile_pad, axis=0) - tile_pad
    padded = (jnp.sum(tile_pad, axis=0) + MOE_BLOCK - 1) // MOE_BLOCK * MOE_BLOCK
    pends = jnp.cumsum(padded)
    pstarts = pends - padded
    n_blocks = -(-(n * TOP_K + nt * N_EXPERTS * (RUN_ALIGN - 1)) // MOE_BLOCK) + N_EXPERTS
    block_start = jnp.arange(n_blocks, dtype=I32) * MOE_BLOCK
    block_e = jnp.minimum(jnp.sum(pends[None, :] <= block_start[:, None], axis=1), N_EXPERTS - 1).astype(I32)
    n_active = (pends[-1:] // MOE_BLOCK).astype(I32)
    used = jnp.sum(tile_pad, axis=1, keepdims=True)
    dump = n_blocks * MOE_BLOCK + (jnp.arange(nt, dtype=I32)[:, None] % 2) * TILE_ROW
    runs = jnp.concatenate([tile_off, used, pstarts[None, :] + before_pad, dump, tile_pad, TILE_SORT - used],
                           axis=1).reshape(nt, 1, 3 * N_RUNS).astype(I32)
    rel = jnp.repeat(tile_off - before, TILE_ROW, axis=0)
    slot = jnp.take_along_axis(rel, ids[:, :TOP_K], axis=1) + rank[:, :TOP_K]
    slot = jnp.pad(slot, ((0, 0), (0, LANES - TOP_K)), constant_values=-1).astype(I32)

    rows_used = jnp.sum(tile_pad, axis=0)
    tails = padded - rows_used
    zero1 = jnp.zeros((1,), I32)
    ztab = jnp.concatenate([jnp.zeros_like(padded), zero1, pstarts + rows_used, zero1, tails, zero1])
    ztab = ztab.reshape(1, 3 * N_RUNS).astype(I32)

    xs = _scatter_rows(xn, slot, runs, ztab, n_active, n_blocks * MOE_BLOCK)
    os_rows = _experts(xs, block_e, n_active, w)
    y = _combine(h, gates, slot, runs, p.reshape(n, PLE_DIM), os_rows, w)
    return y.reshape(bsz, t, D_MODEL)


def kernel(x_prompt, x_sample, p_prompt, p_sample, norm_mix, w_in, a_ln_g, a_ln_b, a_ws, a_bs, b_conv, b_w0, b_w2, b_a0, b_a2, b_g2, b_kk, b_ka, b_rk, b_gn_g, b_gn_b, w_out, norm_ffn, router_w, router_b, moe_w1, moe_b1, moe_w2, moe_b2, norm_ple, ple_proj, ple_gate, norm_final):
    assert norm_mix.shape[0] == 1, "single-layer trunk"
    w = _prep_weights(norm_mix, w_in, a_ln_g, a_ln_b, a_ws, a_bs, b_conv, b_w0, b_w2, b_a0, b_a2, b_g2,
                      b_kk, b_ka, b_rk, b_gn_g, b_gn_b, w_out, norm_ffn, router_w, router_b, moe_w1,
                      moe_b1, moe_w2, moe_b2, norm_ple, ple_proj, ple_gate, norm_final)
    y_prompt = _forward(x_prompt, p_prompt[0], w)
    y_sample = _forward(x_sample, p_sample[0], w)
    return (y_prompt, y_sample)
```

```python
import math

import jax
import jax.numpy as jnp
from jax import lax
from jax.experimental import pallas as pl
from jax.experimental.pallas import tpu as pltpu

F32 = jnp.float32
BF16 = jnp.bfloat16
I32 = jnp.int32
U32 = jnp.uint32

D_MODEL = 1024
D_PACK = D_MODEL // 2
HEAD_DIM = 64
GA = 512
GB = 512
N_PAIR = GB // 128
CHUNK_A = 128
LORA = 64
LORA_G = 128
B_CONV = 3 * GB + 4 * LORA + LORA_G
N_EXPERTS = 32
TOP_K = 4
D_FF = 1024
PLE_DIM = 256
SWIGLU_ALPHA = 1.702
SWIGLU_LIMIT = 7.0
EPS = 1e-6
GN_EPS = 64e-5
DECAY_SCALE = math.exp(-0.5)

LANES = 128
TILE_IN = 512
WKV_STEP = 256
WKV_CHUNK = 64
TILE_POST = 256
TILE_ROW = 256
MOE_BLOCK = 512
VMEM_LIMIT = 56 * 1024 * 1024

NT = (((1,), (1,)), ((), ()))
TN = (((0,), (0,)), ((), ()))


def _mm(a, b):
    return jnp.dot(a, b, preferred_element_type=F32)


def _split2(q):
    hi = q.astype(BF16)
    lo = (q - hi.astype(F32)).astype(BF16)
    return hi, lo


def _segsum(q, bd):
    hi, lo = _split2(q)
    return _mm(hi, bd) + _mm(lo, bd)


def _gelu(z):
    return 0.5 * z * (1.0 + lax.erf(z * (1.0 / math.sqrt(2.0))))


def _pack_bf16_pairs(a):
    half = a.shape[1] // 2
    u = lax.bitcast_convert_type(a.astype(BF16).astype(F32), U32)
    return (u[:, :half] & jnp.uint32(0xFFFF0000)) | (u[:, half:] >> jnp.uint32(16))


def _unpack_bf16_pairs(u):
    left = lax.bitcast_convert_type(u & jnp.uint32(0xFFFF0000), F32)
    right = lax.bitcast_convert_type(u << jnp.uint32(16), F32)
    return left.astype(BF16), right.astype(BF16)


def _rms(xv, g):
    ms = jnp.mean(xv * xv, axis=-1, keepdims=True)
    return xv * lax.rsqrt(ms + EPS) * g


def _inproj_kernel(x_ref, xp_ref, xn_ref, nm_ref, win_ref, lng_ref, lnb_ref, ws_ref, bs_ref,
                   conv_ref, w0_ref, w2_ref, a0_ref, a2_ref, g2_ref, kk_ref, ka_ref, rk_ref, bd_ref,
                   ya_ref, r_ref, v_ref, kn_ref, lw0_ref, lw1_ref, k0_ref, k1_ref, b0_ref, b1_ref,
                   g_ref, bonus_ref):
    tt = x_ref.shape[0]
    i = pl.program_id(1)
    last = pl.num_programs(1) - 1
    nm = nm_ref[...]

    xe = jnp.concatenate([x_ref[...], xp_ref[...], xn_ref[...]], axis=0)
    xe = _rms(xe, nm).astype(BF16)
    za = _mm(xe[:tt], win_ref[:, :2 * GA])
    ze = _mm(xe, win_ref[:, 2 * GA:])
    zb = ze[:tt]
    row_prev = jnp.where(i > 0, ze[tt + 7:tt + 8], 0.0)
    row_next = jnp.where(i < last, ze[tt + 8:tt + 9], 0.0)

    u = _gelu(za[:, :GA])
    v = _gelu(za[:, GA:])
    mu = jnp.mean(v, axis=-1, keepdims=True)
    var = jnp.mean(jnp.square(v - mu), axis=-1, keepdims=True)
    v = ((v - mu) * lax.rsqrt(var + EPS) * lng_ref[...] + lnb_ref[...]).astype(BF16)
    lane_head = lax.broadcasted_iota(I32, (CHUNK_A, GA), 1) // HEAD_DIM
    for c in range(tt // CHUNK_A):
        rows = slice(c * CHUNK_A, (c + 1) * CHUNK_A)
        o = _mm(ws_ref[...], v[rows])
        s = bs_ref[...]
        for h in range(GA // HEAD_DIM):
            s = s + jnp.where(lane_head == h, o[h * CHUNK_A:(h + 1) * CHUNK_A], 0.0)
        ya_ref[rows, :] = (u[rows] * s).astype(ya_ref.dtype)

    conv = conv_ref[...]
    trow = lax.broadcasted_iota(I32, (tt, 1), 0)
    z_prev = jnp.where(trow == 0, row_prev, pltpu.roll(zb, 1, axis=0))
    z_next = jnp.where(trow == tt - 1, row_next, pltpu.roll(zb, tt - 1, axis=0))
    zc = z_prev * conv[0:1] + zb * conv[1:2] + z_next * conv[2:3]
    r = zc[:, :GB]
    k = zc[:, GB:2 * GB]
    vv = zc[:, 2 * GB:3 * GB]
    o0 = 3 * GB
    xw = jnp.tanh(zc[:, o0:o0 + 2 * LORA]).astype(BF16)
    xa = zc[:, o0 + 2 * LORA:o0 + 4 * LORA].astype(BF16)
    xg = jax.nn.sigmoid(zc[:, o0 + 4 * LORA:]).astype(BF16)
    bd = bd_ref[...]
    kk = k * kk_ref[...]
    kk = kk / jnp.maximum(jnp.sqrt(_segsum(kk * kk, bd)), 1e-12)
    ka = ka_ref[...]
    lw_refs = (lw0_ref, lw1_ref)
    k_refs = (k0_ref, k1_ref)
    b_refs = (b0_ref, b1_ref)
    ksum = None
    for d in range(2):
        yw = w0_ref[d:d + 1, :] + _mm(xw, w2_ref[d])
        lw = -DECAY_SCALE * jax.nn.sigmoid(yw)
        a = jax.nn.sigmoid(a0_ref[d:d + 1, :] + _mm(xa, a2_ref[d]))
        kd = k * (1.0 + (a - 1.0) * ka)
        bb = kk * a
        ksum = kd if ksum is None else ksum + kd
        for p in range(N_PAIR):
            ls = slice(p * LANES, (p + 1) * LANES)
            lw_refs[d][p] = lw[:, ls]
            k_refs[d][p] = kd[:, ls].astype(k0_ref.dtype)
            b_refs[d][p] = bb[:, ls].astype(b0_ref.dtype)
    for p in range(N_PAIR):
        ls = slice(p * LANES, (p + 1) * LANES)
        r_ref[p] = r[:, ls].astype(r_ref.dtype)
        v_ref[p] = vv[:, ls].astype(v_ref.dtype)
        kn_ref[p] = kk[:, ls].astype(kn_ref.dtype)
    g_ref[...] = _mm(xg, g2_ref[...])
    bonus_ref[...] = _segsum(r * ksum * rk_ref[...], bd) * vv


def _inproj(x, w):
    bsz, t, _ = x.shape
    tt = min(TILE_IN, t)
    nt = t // tt
    t8 = tt // 8

    def full(a):
        nd = a.ndim
        return pl.BlockSpec(a.shape, lambda b, i: (0,) * nd)

    consts = (w["norm_mix"], w["w_in"], w["a_ln_g"], w["a_ln_b"], w["a_ws"], w["a_bs"], w["b_conv"],
              w["b_w0"], w["b_w2"], w["b_a0"], w["b_a2"], w["b_g2"], w["b_kk"], w["b_ka"], w["b_rk"],
              w["bd"])
    in_specs = [
        pl.BlockSpec((None, tt, D_MODEL), lambda b, i: (b, i, 0)),
        pl.BlockSpec((None, 8, D_MODEL), lambda b, i: (b, jnp.maximum(i * t8 - 1, 0), 0)),
        pl.BlockSpec((None, 8, D_MODEL), lambda b, i: (b, jnp.minimum((i + 1) * t8, t // 8 - 1), 0)),
    ] + [full(a) for a in consts]
    pair = lambda dt: jax.ShapeDtypeStruct((bsz, N_PAIR, t, LANES), dt)
    flat = lambda dt: jax.ShapeDtypeStruct((bsz, t, GB), dt)
    pair_spec = pl.BlockSpec((None, N_PAIR, tt, LANES), lambda b, i: (b, 0, i, 0))
    flat_spec = pl.BlockSpec((None, tt, GB), lambda b, i: (b, i, 0))
    out_shape = (flat(BF16),
                 pair(BF16), pair(BF16), pair(BF16),
                 pair(F32), pair(F32),
                 pair(BF16), pair(BF16), pair(BF16), pair(BF16),
                 flat(F32), flat(F32))
    out_specs = (flat_spec,) + (pair_spec,) * 9 + (flat_spec, flat_spec)
    return pl.pallas_call(
        _inproj_kernel,
        grid=(bsz, nt),
        in_specs=in_specs,
        out_specs=out_specs,
        out_shape=out_shape,
        compiler_params=pltpu.CompilerParams(
            dimension_semantics=("parallel", "parallel"), vmem_limit_bytes=VMEM_LIMIT),
        name="inproj",
    )(x, x, x, *consts)


def _wkv_prep(p, refs, reverse, shared):
    r_ref, v_ref, kn_ref, k_ref, b_ref, lw_ref = refs
    row, col, m64 = shared
    strict = m64 & ((col > row) if reverse else (col < row))
    incl = m64 & ((col >= row) if reverse else (col <= row))

    lw = lw_ref[p]
    l1, l2 = _split2(lw)
    tri = jnp.where(incl, 1.0, 0.0).astype(BF16)
    cum = _mm(tri, l1) + _mm(tri, l2)
    nc = WKV_STEP // WKV_CHUNK
    end = 0 if reverse else WKV_CHUNK - 1
    tot = jnp.concatenate(
        [jnp.broadcast_to(cum[ck * WKV_CHUNK + end:ck * WKV_CHUNK + end + 1], (WKV_CHUNK, LANES))
         for ck in range(nc)], axis=0)
    rr = r_ref[p].astype(F32)
    kn = kn_ref[p].astype(F32)
    kd = k_ref[p].astype(F32)
    bb = b_ref[p].astype(F32)
    winv = jnp.exp(-cum)
    wd = jnp.exp(tot - cum)
    return dict(
        strict=strict, incl=incl, wtot=jnp.exp(tot),
        rt=(rr * jnp.exp(cum)).astype(BF16), at=(-kn * jnp.exp(cum - lw)).astype(BF16),
        bt=(bb * winv).astype(BF16), kt=(kd * winv).astype(BF16),
        bh=(bb * wd).astype(BF16), kh=(kd * wd).astype(BF16), vb=v_ref[p].astype(BF16))


def _wkv_chains(chains, s_scr, masks):
    c8, c16, c32, cb, eye_c, eye_h, blk = masks
    n = WKV_STEP
    nc = n // WKV_CHUNK
    dot_nt = lambda a, b: lax.dot_general(a, b, NT, preferred_element_type=F32)
    dot_tn = lambda a, b: lax.dot_general(a, b, TN, preferred_element_type=F32)
    each = lambda fn: [fn(c) for c in chains]

    for c in chains:
        sl = slice(c["hh"] * HEAD_DIM, (c["hh"] + 1) * HEAD_DIM)
        pr = c["prep"]
        c.update(sl=sl, a=pr["at"][:, sl], r=pr["rt"][:, sl], b=pr["bt"][:, sl], k=pr["kt"][:, sl],
                 v=pr["vb"][:, sl], strict=pr["strict"], incl=pr["incl"])
    lab = each(lambda c: jnp.where(c["strict"], dot_nt(c["a"], c["b"]), 0.0))
    lak = each(lambda c: jnp.where(c["strict"], dot_nt(c["a"], c["k"]), 0.0).astype(BF16))
    mrb = each(lambda c: jnp.where(c["incl"], dot_nt(c["r"], c["b"]), 0.0).astype(BF16))
    mrk = each(lambda c: jnp.where(c["incl"], dot_nt(c["r"], c["k"]), 0.0).astype(BF16))

    lab_c = [sum(jnp.where(cb == ck, l[ck * WKV_CHUNK:(ck + 1) * WKV_CHUNK], 0.0) for ck in range(nc))
             for l in lab]
    bdiag = lambda zb: jnp.concatenate([zb] * nc, axis=0) * blk
    l8 = [jnp.where(c8, l, 0.0) for l in lab_c]
    p1 = [l.astype(BF16) for l in l8]
    p2 = [_mm(q, bdiag(q)).astype(BF16) for q in p1]
    p2d = [bdiag(q) for q in p2]
    x = [eye_c + l for l in l8]
    p4d = [bdiag(_mm(q, qd).astype(BF16)) for q, qd in zip(p2, p2d)]
    x = [xi + _mm(xi.astype(BF16), qd) for xi, qd in zip(x, p2d)]
    x = [xi + _mm(xi.astype(BF16), qd) for xi, qd in zip(x, p4d)]
    prev = c8
    for cur in (c16, c32, None):
        lvl = jnp.logical_not(prev) if cur is None else cur & jnp.logical_not(prev)
        xb = [xi.astype(BF16) for xi in x]
        t = [_mm(jnp.where(lvl, l, 0.0).astype(BF16), bdiag(xi)).astype(BF16) for l, xi in zip(lab_c, xb)]
        x = [xi + _mm(xbi, bdiag(ti)) for xi, xbi, ti in zip(x, xb, t)]
        prev = cur
    tb = [bdiag(xi.astype(BF16)) for xi in x]

    x1 = [_mm(l, c["v"]).astype(BF16) for l, c in zip(lak, chains)]
    wu = [_mm(t, c["a"]).astype(BF16) for t, c in zip(tb, chains)]
    uv = [_mm(t, xi).astype(BF16) for t, xi in zip(tb, x1)]
    q = [(c["r"].astype(F32) + _mm(m, w)).astype(BF16) for c, m, w in zip(chains, mrb, wu)]
    yl = [_mm(m, u) + _mm(mk, c["v"]) for m, u, mk, c in zip(mrb, uv, mrk, chains)]

    g, h = [], []
    for i, c in enumerate(chains):
        gi, hi = [], []
        for ck in range(nc):
            rs = slice(ck * WKV_CHUNK, (ck + 1) * WKV_CHUNK)
            b_c = c["prep"]["bh"][rs, c["sl"]]
            k_c = c["prep"]["kh"][rs, c["sl"]]
            wrow = c["prep"]["wtot"][ck * WKV_CHUNK:ck * WKV_CHUNK + 1, c["sl"]]
            gi.append((dot_tn(b_c, wu[i][rs])
                       + jnp.where(eye_h, jnp.broadcast_to(wrow, (HEAD_DIM, HEAD_DIM)), 0.0)).astype(BF16))
            hi.append(dot_tn(b_c, uv[i][rs]) + dot_tn(k_c, c["v"][rs]))
        g.append(gi)
        h.append(hi)

    s = [s_scr[c["d"], 2 * c["p"] + c["hh"]] for c in chains]
    for step in range(nc):
        for i, c in enumerate(chains):
            ck = nc - 1 - step if c["reverse"] else step
            rs = slice(ck * WKV_CHUNK, (ck + 1) * WKV_CHUNK)
            sb = s[i].astype(BF16)
            c["y_ref"][c["p"], rs, c["sl"]] = _mm(q[i][rs], sb) + yl[i][rs]
            s[i] = _mm(g[i][ck], sb) + h[i][ck]
    for i, c in enumerate(chains):
        s_scr[c["d"], 2 * c["p"] + c["hh"]] = s[i]


def _wkv_kernel(rf_ref, vf_ref, nf_ref, kf_ref, bf_ref, lf_ref,
                rb_ref, vb_ref, nb_ref, kb_ref, bb_ref, lb_ref, yf_ref, yb_ref, s_scr):
    n = WKV_STEP

    @pl.when(pl.program_id(1) == 0)
    def _():
        s_scr[...] = jnp.zeros_like(s_scr)

    def pair(p, carry):
        row = lax.broadcasted_iota(I32, (n, n), 0)
        col = lax.broadcasted_iota(I32, (n, n), 1)
        m64 = (row >> 6) == (col >> 6)
        er = lax.broadcasted_iota(I32, (HEAD_DIM, HEAD_DIM), 0)
        ec = lax.broadcasted_iota(I32, (HEAD_DIM, HEAD_DIM), 1)
        shared = (row, col, m64)
        rc = lax.broadcasted_iota(I32, (WKV_CHUNK, n), 0)
        lc = lax.broadcasted_iota(I32, (WKV_CHUNK, n), 1)
        li = lc & (WKV_CHUNK - 1)
        samec = lambda log2: (rc >> log2) == (li >> log2)
        masks = (samec(3), samec(4), samec(5), lc >> 6, jnp.where(rc == li, 1.0, 0.0).astype(F32), er == ec,
                 jnp.where(m64, 1.0, 0.0).astype(BF16))
        chains = []
        for pp in range(N_PAIR):
            fwd = _wkv_prep(pp, (rf_ref, vf_ref, nf_ref, kf_ref, bf_ref, lf_ref), False, shared)
            bwd = _wkv_prep(pp, (rb_ref, vb_ref, nb_ref, kb_ref, bb_ref, lb_ref), True, shared)
            chains += [dict(prep=pr, hh=hh, d=d, p=pp, reverse=rev, y_ref=y_ref)
                       for pr, d, rev, y_ref in ((fwd, 0, False, yf_ref), (bwd, 1, True, yb_ref))
                       for hh in range(2)]
        _wkv_chains(chains, s_scr, masks)
        return carry

    pair(0, 0)


def _wkv(r, v, kn, k0, b0, lw0, k1, b1, lw1):
    bsz, _, t, _ = r.shape
    assert t % WKV_STEP == 0
    nj = t // WKV_STEP
    fwd = pl.BlockSpec((None, N_PAIR, WKV_STEP, LANES), lambda bi, j: (bi, 0, j, 0))
    bwd = pl.BlockSpec((None, N_PAIR, WKV_STEP, LANES), lambda bi, j: (bi, 0, nj - 1 - j, 0))
    out = jax.ShapeDtypeStruct((bsz, N_PAIR, t, LANES), F32)
    return pl.pallas_call(
        _wkv_kernel,
        grid=(bsz, nj),
        in_specs=[fwd] * 6 + [bwd] * 6,
        out_specs=(fwd, bwd),
        out_shape=(out, out),
        scratch_shapes=[pltpu.VMEM((2, 2 * N_PAIR, HEAD_DIM, HEAD_DIM), F32)],
        compiler_params=pltpu.CompilerParams(
            dimension_semantics=("parallel", "arbitrary"), vmem_limit_bytes=VMEM_LIMIT),
        name="wkv",
    )(r, v, kn, k0, b0, lw0, r, v, kn, k1, b1, lw1)


def _post_kernel(yf_ref, yb_ref, bonus_ref, g_ref, ya_ref, x_ref, gng_ref, gnb_ref, bd_ref, wout_ref,
                 nf_ref, rwh_ref, rwl_ref, rb_ref,
                 h_ref, xn_ref, gate_ref, slot_ref, cnt_ref, before_ref, run_scr):
    tt = x_ref.shape[0]
    step = pl.program_id(0)

    @pl.when(step == 0)
    def _():
        run_scr[...] = jnp.zeros_like(run_scr)

    y = jnp.concatenate([yf_ref[p] + yb_ref[p] for p in range(N_PAIR)], axis=1)
    bd = bd_ref[...]
    inv = 1.0 / HEAD_DIM
    mu = _segsum(y, bd) * inv
    yc = y - mu
    var = _segsum(yc * yc, bd) * inv
    yn = yc * lax.rsqrt(var + GN_EPS) * gng_ref[...] + gnb_ref[...] + bonus_ref[...]
    ybm = (yn * g_ref[...]).astype(BF16)
    h = x_ref[...] + _mm(ya_ref[...], wout_ref[:GA, :]) + _mm(ybm, wout_ref[GA:, :])
    h_ref[...] = h
    xn = _rms(h, nf_ref[...])
    xn_ref[...] = xn

    lane = lax.broadcasted_iota(I32, (tt, LANES), 1)
    xh, xl = _split2(xn)
    logits = _mm(xh, rwh_ref[...]) + _mm(xl, rwh_ref[...]) + _mm(xh, rwl_ref[...])
    logits = jnp.where(lane < N_EXPERTS, logits + rb_ref[...], -jnp.inf)
    vals, ids, sel = [], [], jnp.zeros((tt, LANES), F32)
    cur = logits
    for _ in range(TOP_K):
        m = jnp.max(cur, axis=-1, keepdims=True)
        idx = jnp.min(jnp.where(cur == m, lane, LANES), axis=-1, keepdims=True)
        hit = lane == idx
        vals.append(m)
        ids.append(idx)
        sel = sel + jnp.where(hit, 1.0, 0.0)
        cur = jnp.where(hit, -jnp.inf, cur)
    es = [jnp.exp(vv - vals[0]) for vv in vals]
    den = es[0] + es[1] + es[2] + es[3]

    rt = lax.broadcasted_iota(I32, (tt, tt), 0)
    ct = lax.broadcasted_iota(I32, (tt, tt), 1)
    earlier = jnp.where(ct < rt, 1.0, 0.0).astype(BF16)
    tile_cnt = jnp.sum(sel, axis=0, keepdims=True)
    tile_pad = jnp.floor((tile_cnt + (RUN_ALIGN - 1)) * (1.0 / RUN_ALIGN)) * RUN_ALIGN
    er = lax.broadcasted_iota(I32, (LANES, LANES), 0)
    ec = lax.broadcasted_iota(I32, (LANES, LANES), 1)
    lower_experts = jnp.where(er < ec, 1.0, 0.0).astype(BF16)
    tile_off = _mm(jnp.broadcast_to(tile_pad, (8, LANES)).astype(BF16), lower_experts)[0:1]
    pos = _mm(earlier, sel.astype(BF16)) + tile_off
    slot_o = jnp.full((tt, LANES), -1.0, F32)
    gate_o = jnp.zeros((tt, LANES), F32)
    for kx in range(TOP_K):
        sk = jnp.sum(jnp.where(lane == ids[kx], pos, 0.0), axis=-1, keepdims=True)
        gate_o = jnp.where(lane == kx, es[kx] / den, gate_o)
        slot_o = jnp.where(lane == kx, sk, slot_o)
    gate_ref[...] = gate_o
    slot_ref[...] = slot_o.astype(I32)
    before_ref[...] = run_scr[...]
    run = run_scr[...] + tile_cnt
    run_scr[...] = run
    cnt_ref[...] = run


def _post(yf, yb, bonus, g, ya, x, w):
    bsz, t, _ = x.shape
    n = bsz * t
    tt = min(TILE_POST, t)
    nt = t // tt

    def full(a):
        nd = a.ndim
        return pl.BlockSpec(a.shape, lambda s: (0,) * nd)

    consts = (w["b_gn_g"], w["b_gn_b"], w["bd"], w["w_out"], w["norm_ffn"], w["router_wh"], w["router_wl"],
              w["router_b"])
    pair_spec = pl.BlockSpec((None, N_PAIR, tt, LANES), lambda s: (s // nt, 0, s % nt, 0))
    tok3 = lambda width: pl.BlockSpec((None, tt, width), lambda s: (s // nt, s % nt, 0))
    tok2 = lambda width: pl.BlockSpec((tt, width), lambda s: (s, 0))
    in_specs = [pair_spec, pair_spec, tok3(GB), tok3(GB), tok3(GA), tok3(D_MODEL)] + [full(a) for a in consts]
    out_shape = (jax.ShapeDtypeStruct((n, D_MODEL), F32),
                 jax.ShapeDtypeStruct((n, D_MODEL), F32),
                 jax.ShapeDtypeStruct((n, LANES), F32),
                 jax.ShapeDtypeStruct((n, LANES), I32),
                 jax.ShapeDtypeStruct((1, LANES), F32),
                 jax.ShapeDtypeStruct((bsz * nt, 1, LANES), F32))
    out_specs = (tok2(D_MODEL), tok2(D_MODEL), tok2(LANES), tok2(LANES),
                 pl.BlockSpec((1, LANES), lambda s: (0, 0)),
                 pl.BlockSpec((None, 1, LANES), lambda s: (s, 0, 0)))
    return pl.pallas_call(
        _post_kernel,
        grid=(bsz * nt,),
        in_specs=in_specs,
        out_specs=out_specs,
        out_shape=out_shape,
        scratch_shapes=[pltpu.VMEM((1, LANES), F32)],
        compiler_params=pltpu.CompilerParams(
            dimension_semantics=("arbitrary",), vmem_limit_bytes=VMEM_LIMIT),
        name="post_router",
    )(yf, yb, bonus, g, ya, x, *consts)


RUN_ALIGN = 8
RUN_SIZES = tuple(1 << i for i in range(8, 2, -1))
RUN_SMALL = 64
TILE_SORT = TILE_ROW * TOP_K + N_EXPERTS * RUN_ALIGN
N_RUNS = N_EXPERTS + 1
DUMP_ROWS = 2 * TILE_ROW


def _tile_runs(sc_ref, copy, wait=False):
    def per_run(e, c):
        o0 = sc_ref[0, e]
        d0 = sc_ref[0, N_RUNS + e]
        ln = sc_ref[0, 2 * N_RUNS + e]

        def pieces(sizes, o, d):
            for size in sizes:
                bit = ln & size

                @pl.when(bit != 0)
                def _():
                    cp = copy(e, pl.multiple_of(o, RUN_ALIGN), pl.multiple_of(d, RUN_ALIGN), size)
                    if wait:
                        cp.wait()
                    else:
                        cp.start()

                o = o + bit
                d = d + bit

        big = tuple(sz for sz in RUN_SIZES if sz >= RUN_SMALL)
        small = tuple(sz for sz in RUN_SIZES if sz < RUN_SMALL)

        @pl.when(ln >= RUN_SMALL)
        def _():
            pieces(big, o0, d0)

        skip = ln & ~(RUN_SMALL - 1)
        pieces(small, o0 + skip, d0 + skip)
        return c

    lax.fori_loop(0, N_RUNS, per_run, 0)


def _slot_matrix(slot, vals, width):
    tt = slot.shape[0]
    lane = lax.broadcasted_iota(I32, (tt, width), 1)
    m = jnp.zeros((tt, width), F32)
    for kx in range(TOP_K):
        m = m + jnp.where(lane == slot[:, kx:kx + 1], vals[kx], 0.0)
    return m


def _scatter_kernel(sc_ref, zt_ref, na_ref, slot_ref, xn_ref, xs_ref, sb, sem):
    s = pl.program_id(0)
    ns = pl.num_programs(0)
    cur = s % 2
    n_blocks = (xs_ref.shape[0] - DUMP_ROWS) // MOE_BLOCK

    def runs(tab_ref, sl, wait=False):
        _tile_runs(tab_ref, lambda e, o, d, size: pltpu.make_async_copy(
            sb.at[sl, pl.ds(o, size)], xs_ref.at[pl.ds(d, size)], sem.at[sl]), wait)

    def wait_tile(sl):
        for j in range(TILE_SORT // TILE_ROW):
            pltpu.make_async_copy(sb.at[sl, pl.ds(j * TILE_ROW, TILE_ROW)], xs_ref.at[pl.ds(0, TILE_ROW)],
                                  sem.at[sl]).wait()

    def blocks(wait):
        def body(b, c):
            cp = pltpu.make_async_copy(sb.at[0, pl.ds(0, MOE_BLOCK)],
                                       xs_ref.at[pl.ds(pl.multiple_of(b * MOE_BLOCK, MOE_BLOCK), MOE_BLOCK)],
                                       sem.at[0])
            if wait:
                cp.wait()
            else:
                cp.start()
            return c
        lax.fori_loop(na_ref[0], n_blocks + DUMP_ROWS // MOE_BLOCK, body, 0)

    @pl.when(s == 0)
    def _():
        sb[0] = jnp.zeros(sb.shape[1:], U32)
        runs(zt_ref, 0)
        blocks(False)
        runs(zt_ref, 0, wait=True)
        blocks(True)

    p01 = _slot_matrix(slot_ref[...], (1.0,) * TOP_K, TILE_SORT).astype(BF16)
    sb[cur] = _pack_bf16_pairs(lax.dot_general(p01, xn_ref[...].astype(BF16), TN, preferred_element_type=F32))
    runs(sc_ref, cur)

    @pl.when(s > 0)
    def _():
        wait_tile(1 - cur)

    @pl.when(s == ns - 1)
    def _():
        wait_tile(cur)


def _scatter_rows(xn, slot, runs, ztab, n_active, rows):
    n = xn.shape[0]
    tt = TILE_ROW
    return pl.pallas_call(
        _scatter_kernel,
        grid=(n // tt,),
        in_specs=[pl.BlockSpec((None, 1, 3 * N_RUNS), lambda s: (s, 0, 0), memory_space=pltpu.SMEM),
                  pl.BlockSpec(memory_space=pltpu.SMEM),
                  pl.BlockSpec(memory_space=pltpu.SMEM),
                  pl.BlockSpec((tt, LANES), lambda s: (s, 0)),
                  pl.BlockSpec((tt, D_MODEL), lambda s: (s, 0))],
        out_specs=pl.BlockSpec(memory_space=pl.ANY),
        out_shape=jax.ShapeDtypeStruct((rows + DUMP_ROWS, D_PACK), U32),
        scratch_shapes=[pltpu.VMEM((2, TILE_SORT, D_PACK), U32), pltpu.SemaphoreType.DMA((2,))],
        compiler_params=pltpu.CompilerParams(
            dimension_semantics=("arbitrary",), vmem_limit_bytes=VMEM_LIMIT),
        name="moe_scatter",
    )(runs, ztab, n_active, slot, xn)


def _expert_kernel(be_ref, na_ref, xs_ref, w1_ref, b1_ref, w2_ref, b2_ref, o_ref):
    del be_ref
    s = pl.program_id(0)

    @pl.when(s < na_ref[0])
    def _():
        xl, xr = _unpack_bf16_pairs(xs_ref[...])
        hdn = _mm(xl, w1_ref[:D_PACK, :]) + _mm(xr, w1_ref[D_PACK:, :]) + b1_ref[...]
        glu = jnp.minimum(hdn[:, :D_FF], SWIGLU_LIMIT)
        lin = jnp.clip(hdn[:, D_FF:], -SWIGLU_LIMIT, SWIGLU_LIMIT)
        act = glu * jax.nn.sigmoid(SWIGLU_ALPHA * glu) * (lin + 1.0)
        o_ref[...] = _pack_bf16_pairs(_mm(act.astype(BF16), w2_ref[...]) + b2_ref[...])

    @pl.when(s >= na_ref[0])
    def _():
        o_ref[...] = jnp.zeros_like(o_ref)


def _experts(xs, block_e, n_active, w):
    nb = block_e.shape[0]
    rows = nb * MOE_BLOCK
    grid_spec = pltpu.PrefetchScalarGridSpec(
        num_scalar_prefetch=2,
        grid=(nb,),
        in_specs=[
            pl.BlockSpec((MOE_BLOCK, D_PACK), lambda s, be, na: (s, 0)),
            pl.BlockSpec((None, D_MODEL, 2 * D_FF), lambda s, be, na: (be[s], 0, 0)),
            pl.BlockSpec((None, 1, 2 * D_FF), lambda s, be, na: (be[s], 0, 0)),
            pl.BlockSpec((None, D_FF, D_MODEL), lambda s, be, na: (be[s], 0, 0)),
            pl.BlockSpec((None, 1, D_MODEL), lambda s, be, na: (be[s], 0, 0)),
        ],
        out_specs=pl.BlockSpec((MOE_BLOCK, D_PACK), lambda s, be, na: (s, 0)),
    )
    return pl.pallas_call(
        _expert_kernel,
        grid_spec=grid_spec,
        out_shape=jax.ShapeDtypeStruct((rows, D_PACK), U32),
        compiler_params=pltpu.CompilerParams(
            dimension_semantics=("arbitrary",), vmem_limit_bytes=VMEM_LIMIT),
        name="moe_experts",
    )(block_e, n_active, xs, w["moe_w1"], w["moe_b1"], w["moe_w2"], w["moe_b2"])


def _combine_kernel(scur_ref, snxt_ref, slot_ref, h_ref, gate_ref, p_ref, npl_ref, pg_ref, pp_ref, nfin_ref,
                    os_ref, y_ref, gb, sem):
    s = pl.program_id(0)
    ns = pl.num_programs(0)
    slot = s % 2

    def fetch(sc_ref, sl):
        _tile_runs(sc_ref, lambda e, o, d, size: pltpu.make_async_copy(
            os_ref.at[pl.ds(pl.multiple_of(jnp.where(e == N_EXPERTS, 0, d), RUN_ALIGN), size)],
            gb.at[sl, pl.ds(o, size)], sem.at[sl]))

    @pl.when(s == 0)
    def _():
        fetch(scur_ref, 0)

    @pl.when(s + 1 < ns)
    def _():
        fetch(snxt_ref, 1 - slot)

    for j in range(TILE_SORT // TILE_ROW):
        pltpu.make_async_copy(os_ref.at[pl.ds(0, TILE_ROW)], gb.at[slot, pl.ds(j * TILE_ROW, TILE_ROW)],
                              sem.at[slot]).wait()

    gate = gate_ref[...]
    pm = _slot_matrix(slot_ref[...], [gate[:, kx:kx + 1] for kx in range(TOP_K)], TILE_SORT)
    tt = pm.shape[0]
    pst = jnp.concatenate(_split2(pm), axis=0)
    gl, gr = _unpack_bf16_pairs(gb[slot])
    moe = jnp.concatenate([_mm(pst, gl), _mm(pst, gr)], axis=1)
    h = h_ref[...] + moe[:tt] + moe[tt:]
    gt = jax.nn.sigmoid(_mm(_rms(h, npl_ref[...]).astype(BF16), pg_ref[...]))
    h = h + _mm(p_ref[...].astype(BF16), pp_ref[...]) * gt
    y_ref[...] = _rms(h, nfin_ref[...])


def _combine(h, gates, slot, runs, p, os_rows, w):
    n = h.shape[0]
    tt = TILE_ROW
    ns = n // tt

    def full(a):
        nd = a.ndim
        return pl.BlockSpec(a.shape, lambda s: (0,) * nd)

    consts = (w["norm_ple"], w["ple_gate"], w["ple_proj"], w["norm_final"])
    smem = lambda fn: pl.BlockSpec((None, 1, 3 * N_RUNS), fn, memory_space=pltpu.SMEM)
    return pl.pallas_call(
        _combine_kernel,
        grid=(ns,),
        in_specs=[smem(lambda s: (s, 0, 0)),
                  smem(lambda s: (jnp.minimum(s + 1, ns - 1), 0, 0)),
                  pl.BlockSpec((tt, LANES), lambda s: (s, 0)),
                  pl.BlockSpec((tt, D_MODEL), lambda s: (s, 0)),
                  pl.BlockSpec((tt, LANES), lambda s: (s, 0)),
                  pl.BlockSpec((tt, PLE_DIM), lambda s: (s, 0))]
        + [full(a) for a in consts]
        + [pl.BlockSpec(memory_space=pl.ANY)],
        out_specs=pl.BlockSpec((tt, D_MODEL), lambda s: (s, 0)),
        out_shape=jax.ShapeDtypeStruct((n, D_MODEL), F32),
        scratch_shapes=[pltpu.VMEM((2, TILE_SORT, D_PACK), U32), pltpu.SemaphoreType.DMA((2,))],
        compiler_params=pltpu.CompilerParams(
            dimension_semantics=("arbitrary",), vmem_limit_bytes=VMEM_LIMIT),
        name="moe_combine",
    )(runs, runs, slot, h, gates, p, *consts, os_rows)


def _prep_weights(norm_mix, w_in, a_ln_g, a_ln_b, a_ws, a_bs, b_conv, b_w0, b_w2, b_a0, b_a2, b_g2,
                  b_kk, b_ka, b_rk, b_gn_g, b_gn_b, w_out, norm_ffn, router_w, router_b, moe_w1,
                  moe_b1, moe_w2, moe_b2, norm_ple, ple_proj, ple_gate, norm_final):
    row = lambda a: a.reshape(1, -1).astype(F32)

    def lora_pad(m):
        z = jnp.zeros((2, 2 * LORA, GB), F32)
        z = z.at[0, :LORA].set(m[0]).at[1, LORA:].set(m[1])
        return z.astype(BF16)

    seg = jnp.arange(GB, dtype=I32) // HEAD_DIM
    rw = jnp.pad(router_w[0].astype(F32), ((0, 0), (0, LANES - N_EXPERTS)))
    rw_hi = rw.astype(BF16)
    return {
        "norm_mix": row(norm_mix[0]),
        "w_in": w_in[0].astype(BF16),
        "a_ln_g": row(a_ln_g[0]),
        "a_ln_b": row(a_ln_b[0]),
        "a_ws": a_ws[0].reshape(-1, CHUNK_A).astype(BF16),
        "a_bs": jnp.repeat(a_bs[0].T.astype(F32), HEAD_DIM, axis=1),
        "b_conv": b_conv[0].astype(F32),
        "b_w0": b_w0[0].astype(F32),
        "b_w2": lora_pad(b_w2[0]),
        "b_a0": b_a0[0].astype(F32),
        "b_a2": lora_pad(b_a2[0]),
        "b_g2": b_g2[0].astype(BF16),
        "b_kk": row(b_kk[0]),
        "b_ka": row(b_ka[0]),
        "b_rk": row(b_rk[0]),
        "bd": (seg[:, None] == seg[None, :]).astype(BF16),
        "b_gn_g": row(b_gn_g[0]),
        "b_gn_b": row(b_gn_b[0]),
        "w_out": w_out[0].astype(BF16),
        "norm_ffn": row(norm_ffn[0]),
        "router_wh": rw_hi,
        "router_wl": (rw - rw_hi.astype(F32)).astype(BF16),
        "router_b": jnp.pad(router_b[0].astype(F32), (0, LANES - N_EXPERTS)).reshape(1, LANES),
        "moe_w1": moe_w1[0].astype(BF16),
        "moe_b1": moe_b1[0].astype(F32).reshape(N_EXPERTS, 1, 2 * D_FF),
        "moe_w2": moe_w2[0].astype(BF16),
        "moe_b2": moe_b2[0].astype(F32).reshape(N_EXPERTS, 1, D_MODEL),
        "norm_ple": row(norm_ple[0]),
        "ple_gate": ple_gate[0].astype(BF16),
        "ple_proj": ple_proj[0].astype(BF16),
        "norm_final": row(norm_final),
    }


def _forward(x, p, w):
    bsz, t, _ = x.shape
    n = bsz * t
    ya, r, v, kn, lw0, lw1, k0, k1, b0, b1, g, bonus = _inproj(x, w)
    yf, yb = _wkv(r, v, kn, k0, b0, lw0, k1, b1, lw1)
    h, xn, gates, slot, counts, before = _post(yf, yb, bonus, g, ya, x, w)

    nt = n // TILE_ROW
    counts = counts[0, :N_EXPERTS].astype(I32)
    before = before[:, 0, :N_EXPERTS].astype(I32)
    tile_cnt = jnp.concatenate([before[1:], counts[None]], axis=0) - before
    tile_pad = (tile_cnt + RUN_ALIGN - 1) // RUN_ALIGN * RUN_ALIGN
    tile_off = jnp.cumsum(tile_pad, axis=1) - tile_pad
    before_pad = jnp.cumsum(tile_pad, axis=0) - tile_pad
    padded = (jnp.sum(tile_pad, axis=0) + MOE_BLOCK - 1) // MOE_BLOCK * MOE_BLOCK
    pends = jnp.cumsum(padded)
    pstarts = pends - padded
    n_blocks = -(-(n * TOP_K + nt * N_EXPERTS * (RUN_ALIGN - 1)) // MOE_BLOCK) + N_EXPERTS
    block_start = jnp.arange(n_blocks, dtype=I32) * MOE_BLOCK
    block_e = jnp.minimum(jnp.sum(pends[None, :] <= block_start[:, None], axis=1), N_EXPERTS - 1).astype(I32)
    n_active = (pends[-1:] // MOE_BLOCK).astype(I32)
    used = jnp.sum(tile_pad, axis=1, keepdims=True)
    dump = n_blocks * MOE_BLOCK + (jnp.arange(nt, dtype=I32)[:, None] % 2) * TILE_ROW
    runs = jnp.concatenate([tile_off, used, pstarts[None, :] + before_pad, dump, tile_pad, TILE_SORT - used],
                           axis=1).reshape(nt, 1, 3 * N_RUNS).astype(I32)
    rows_used = jnp.sum(tile_pad, axis=0)
    tails = padded - rows_used
    zero1 = jnp.zeros((1,), I32)
    ztab = jnp.concatenate([jnp.zeros_like(padded), zero1, pstarts + rows_used, zero1, tails, zero1])
    ztab = ztab.reshape(1, 3 * N_RUNS).astype(I32)

    xs = _scatter_rows(xn, slot, runs, ztab, n_active, n_blocks * MOE_BLOCK)
    os_rows = _experts(xs, block_e, n_active, w)
    y = _combine(h, gates, slot, runs, p.reshape(n, PLE_DIM), os_rows, w)
    return y.reshape(bsz, t, D_MODEL)


def kernel(x_prompt, x_sample, p_prompt, p_sample, norm_mix, w_in, a_ln_g, a_ln_b, a_ws, a_bs, b_conv, b_w0, b_w2, b_a0, b_a2, b_g2, b_kk, b_ka, b_rk, b_gn_g, b_gn_b, w_out, norm_ffn, router_w, router_b, moe_w1, moe_b1, moe_w2, moe_b2, norm_ple, ple_proj, ple_gate, norm_final):
    assert norm_mix.shape[0] == 1, "single-layer trunk"
    w = _prep_weights(norm_mix, w_in, a_ln_g, a_ln_b, a_ws, a_bs, b_conv, b_w0, b_w2, b_a0, b_a2, b_g2,
                      b_kk, b_ka, b_rk, b_gn_g, b_gn_b, w_out, norm_ffn, router_w, router_b, moe_w1,
                      moe_b1, moe_w2, moe_b2, norm_ple, ple_proj, ple_gate, norm_final)
    y_prompt = _forward(x_prompt, p_prompt[0], w)
    y_sample = _forward(x_sample, p_sample[0], w)
    return (y_prompt, y_sample)
```

```python
import math

import jax
import jax.numpy as jnp
from jax import lax
from jax.experimental import pallas as pl
from jax.experimental.pallas import tpu as pltpu

F32 = jnp.float32
BF16 = jnp.bfloat16
I32 = jnp.int32
U32 = jnp.uint32

D_MODEL = 1024
D_PACK = D_MODEL // 2
HEAD_DIM = 64
GA = 512
GB = 512
N_PAIR = GB // 128
CHUNK_A = 128
LORA = 64
LORA_G = 128
B_CONV = 3 * GB + 4 * LORA + LORA_G
N_EXPERTS = 32
TOP_K = 4
D_FF = 1024
PLE_DIM = 256
SWIGLU_ALPHA = 1.702
SWIGLU_LIMIT = 7.0
EPS = 1e-6
GN_EPS = 64e-5
DECAY_SCALE = math.exp(-0.5)

LANES = 128
TILE_IN = 512
WKV_STEP = 256
WKV_CHUNK = 64
TILE_POST = 256
TILE_ROW = 256
MOE_BLOCK = 1024
VMEM_LIMIT = 56 * 1024 * 1024

NT = (((1,), (1,)), ((), ()))
TN = (((0,), (0,)), ((), ()))


def _mm(a, b):
    return jnp.dot(a, b, preferred_element_type=F32)


def _split2(q):
    hi = q.astype(BF16)
    lo = (q - hi.astype(F32)).astype(BF16)
    return hi, lo


def _segsum(q, bd):
    hi, lo = _split2(q)
    return _mm(hi, bd) + _mm(lo, bd)


def _gelu(z):
    return 0.5 * z * (1.0 + lax.erf(z * (1.0 / math.sqrt(2.0))))


def _pack_bf16_pairs(a):
    half = a.shape[1] // 2
    u = lax.bitcast_convert_type(a.astype(BF16).astype(F32), U32)
    return (u[:, :half] & jnp.uint32(0xFFFF0000)) | (u[:, half:] >> jnp.uint32(16))


def _unpack_bf16_pairs(u):
    left = lax.bitcast_convert_type(u & jnp.uint32(0xFFFF0000), F32)
    right = lax.bitcast_convert_type(u << jnp.uint32(16), F32)
    return left.astype(BF16), right.astype(BF16)


def _rms(xv, g):
    ms = jnp.mean(xv * xv, axis=-1, keepdims=True)
    return xv * lax.rsqrt(ms + EPS) * g


def _inproj_kernel(x_ref, xp_ref, xn_ref, nm_ref, win_ref, lng_ref, lnb_ref, ws_ref, bs_ref,
                   conv_ref, w0_ref, w2_ref, a0_ref, a2_ref, g2_ref, kk_ref, ka_ref, rk_ref, bd_ref,
                   ya_ref, r_ref, v_ref, kn_ref, lw0_ref, lw1_ref, k0_ref, k1_ref, b0_ref, b1_ref,
                   g_ref, bonus_ref):
    tt = x_ref.shape[0]
    i = pl.program_id(1)
    last = pl.num_programs(1) - 1
    nm = nm_ref[...]

    xe = jnp.concatenate([x_ref[...], xp_ref[...], xn_ref[...]], axis=0)
    xe = _rms(xe, nm).astype(BF16)
    za = _mm(xe[:tt], win_ref[:, :2 * GA])
    ze = _mm(xe, win_ref[:, 2 * GA:])
    zb = ze[:tt]
    row_prev = jnp.where(i > 0, ze[tt + 7:tt + 8], 0.0)
    row_next = jnp.where(i < last, ze[tt + 8:tt + 9], 0.0)

    u = _gelu(za[:, :GA])
    v = _gelu(za[:, GA:])
    mu = jnp.mean(v, axis=-1, keepdims=True)
    var = jnp.mean(jnp.square(v - mu), axis=-1, keepdims=True)
    v = ((v - mu) * lax.rsqrt(var + EPS) * lng_ref[...] + lnb_ref[...]).astype(BF16)
    lane_head = lax.broadcasted_iota(I32, (CHUNK_A, GA), 1) // HEAD_DIM
    for c in range(tt // CHUNK_A):
        rows = slice(c * CHUNK_A, (c + 1) * CHUNK_A)
        o = _mm(ws_ref[...], v[rows])
        s = bs_ref[...]
        for h in range(GA // HEAD_DIM):
            s = s + jnp.where(lane_head == h, o[h * CHUNK_A:(h + 1) * CHUNK_A], 0.0)
        ya_ref[rows, :] = (u[rows] * s).astype(ya_ref.dtype)

    conv = conv_ref[...]
    trow = lax.broadcasted_iota(I32, (tt, 1), 0)
    z_prev = jnp.where(trow == 0, row_prev, pltpu.roll(zb, 1, axis=0))
    z_next = jnp.where(trow == tt - 1, row_next, pltpu.roll(zb, tt - 1, axis=0))
    zc = z_prev * conv[0:1] + zb * conv[1:2] + z_next * conv[2:3]
    r = zc[:, :GB]
    k = zc[:, GB:2 * GB]
    vv = zc[:, 2 * GB:3 * GB]
    o0 = 3 * GB
    xw = jnp.tanh(zc[:, o0:o0 + 2 * LORA]).astype(BF16)
    xa = zc[:, o0 + 2 * LORA:o0 + 4 * LORA].astype(BF16)
    xg = jax.nn.sigmoid(zc[:, o0 + 4 * LORA:]).astype(BF16)
    bd = bd_ref[...]
    kk = k * kk_ref[...]
    kk = kk / jnp.maximum(jnp.sqrt(_segsum(kk * kk, bd)), 1e-12)
    ka = ka_ref[...]
    lw_refs = (lw0_ref, lw1_ref)
    k_refs = (k0_ref, k1_ref)
    b_refs = (b0_ref, b1_ref)
    ksum = None
    for d in range(2):
        yw = w0_ref[d:d + 1, :] + _mm(xw, w2_ref[d])
        lw = -DECAY_SCALE * jax.nn.sigmoid(yw)
        a = jax.nn.sigmoid(a0_ref[d:d + 1, :] + _mm(xa, a2_ref[d]))
        kd = k * (1.0 + (a - 1.0) * ka)
        bb = kk * a
        ksum = kd if ksum is None else ksum + kd
        for p in range(N_PAIR):
            ls = slice(p * LANES, (p + 1) * LANES)
            lw_refs[d][p] = lw[:, ls]
            k_refs[d][p] = kd[:, ls].astype(k0_ref.dtype)
            b_refs[d][p] = bb[:, ls].astype(b0_ref.dtype)
    for p in range(N_PAIR):
        ls = slice(p * LANES, (p + 1) * LANES)
        r_ref[p] = r[:, ls].astype(r_ref.dtype)
        v_ref[p] = vv[:, ls].astype(v_ref.dtype)
        kn_ref[p] = kk[:, ls].astype(kn_ref.dtype)
    g_ref[...] = _mm(xg, g2_ref[...])
    bonus_ref[...] = _segsum(r * ksum * rk_ref[...], bd) * vv


def _inproj(x, w):
    bsz, t, _ = x.shape
    tt = min(TILE_IN, t)
    nt = t // tt
    t8 = tt // 8

    def full(a):
        nd = a.ndim
        return pl.BlockSpec(a.shape, lambda b, i: (0,) * nd)

    consts = (w["norm_mix"], w["w_in"], w["a_ln_g"], w["a_ln_b"], w["a_ws"], w["a_bs"], w["b_conv"],
              w["b_w0"], w["b_w2"], w["b_a0"], w["b_a2"], w["b_g2"], w["b_kk"], w["b_ka"], w["b_rk"],
              w["bd"])
    in_specs = [
        pl.BlockSpec((None, tt, D_MODEL), lambda b, i: (b, i, 0)),
        pl.BlockSpec((None, 8, D_MODEL), lambda b, i: (b, jnp.maximum(i * t8 - 1, 0), 0)),
        pl.BlockSpec((None, 8, D_MODEL), lambda b, i: (b, jnp.minimum((i + 1) * t8, t // 8 - 1), 0)),
    ] + [full(a) for a in consts]
    pair = lambda dt: jax.ShapeDtypeStruct((bsz, N_PAIR, t, LANES), dt)
    flat = lambda dt: jax.ShapeDtypeStruct((bsz, t, GB), dt)
    pair_spec = pl.BlockSpec((None, N_PAIR, tt, LANES), lambda b, i: (b, 0, i, 0))
    flat_spec = pl.BlockSpec((None, tt, GB), lambda b, i: (b, i, 0))
    out_shape = (flat(BF16),
                 pair(BF16), pair(BF16), pair(BF16),
                 pair(F32), pair(F32),
                 pair(BF16), pair(BF16), pair(BF16), pair(BF16),
                 flat(F32), flat(F32))
    out_specs = (flat_spec,) + (pair_spec,) * 9 + (flat_spec, flat_spec)
    return pl.pallas_call(
        _inproj_kernel,
        grid=(bsz, nt),
        in_specs=in_specs,
        out_specs=out_specs,
        out_shape=out_shape,
        compiler_params=pltpu.CompilerParams(
            dimension_semantics=("parallel", "parallel"), vmem_limit_bytes=VMEM_LIMIT),
        name="inproj",
    )(x, x, x, *consts)


def _wkv_prep(p, refs, reverse, shared):
    r_ref, v_ref, kn_ref, k_ref, b_ref, lw_ref = refs
    row, col, m64 = shared
    strict = m64 & ((col > row) if reverse else (col < row))
    incl = m64 & ((col >= row) if reverse else (col <= row))

    lw = lw_ref[p]
    l1, l2 = _split2(lw)
    tri = jnp.where(incl, 1.0, 0.0).astype(BF16)
    cum = _mm(tri, l1) + _mm(tri, l2)
    nc = WKV_STEP // WKV_CHUNK
    end = 0 if reverse else WKV_CHUNK - 1
    tot = jnp.concatenate(
        [jnp.broadcast_to(cum[ck * WKV_CHUNK + end:ck * WKV_CHUNK + end + 1], (WKV_CHUNK, LANES))
         for ck in range(nc)], axis=0)
    rr = r_ref[p].astype(F32)
    kn = kn_ref[p].astype(F32)
    kd = k_ref[p].astype(F32)
    bb = b_ref[p].astype(F32)
    winv = jnp.exp(-cum)
    wd = jnp.exp(tot - cum)
    return dict(
        strict=strict, incl=incl, wtot=jnp.exp(tot),
        rt=(rr * jnp.exp(cum)).astype(BF16), at=(-kn * jnp.exp(cum - lw)).astype(BF16),
        bt=(bb * winv).astype(BF16), kt=(kd * winv).astype(BF16),
        bh=(bb * wd).astype(BF16), kh=(kd * wd).astype(BF16), vb=v_ref[p].astype(BF16))


def _wkv_chains(chains, s_scr, masks):
    c8, c16, c32, cb, eye_c, eye_h, blk = masks
    n = WKV_STEP
    nc = n // WKV_CHUNK
    dot_nt = lambda a, b: lax.dot_general(a, b, NT, preferred_element_type=F32)
    dot_tn = lambda a, b: lax.dot_general(a, b, TN, preferred_element_type=F32)
    each = lambda fn: [fn(c) for c in chains]

    for c in chains:
        sl = slice(c["hh"] * HEAD_DIM, (c["hh"] + 1) * HEAD_DIM)
        pr = c["prep"]
        c.update(sl=sl, a=pr["at"][:, sl], r=pr["rt"][:, sl], b=pr["bt"][:, sl], k=pr["kt"][:, sl],
                 v=pr["vb"][:, sl], strict=pr["strict"], incl=pr["incl"])
    lab = each(lambda c: jnp.where(c["strict"], dot_nt(c["a"], c["b"]), 0.0))
    lak = each(lambda c: jnp.where(c["strict"], dot_nt(c["a"], c["k"]), 0.0).astype(BF16))
    mrb = each(lambda c: jnp.where(c["incl"], dot_nt(c["r"], c["b"]), 0.0).astype(BF16))
    mrk = each(lambda c: jnp.where(c["incl"], dot_nt(c["r"], c["k"]), 0.0).astype(BF16))

    lab_c = [sum(jnp.where(cb == ck, l[ck * WKV_CHUNK:(ck + 1) * WKV_CHUNK], 0.0) for ck in range(nc))
             for l in lab]
    bdiag = lambda zb: jnp.concatenate([zb] * nc, axis=0) * blk
    l8 = [jnp.where(c8, l, 0.0) for l in lab_c]
    p1 = [l.astype(BF16) for l in l8]
    p2 = [_mm(q, bdiag(q)).astype(BF16) for q in p1]
    p2d = [bdiag(q) for q in p2]
    x = [eye_c + l for l in l8]
    p4d = [bdiag(_mm(q, qd).astype(BF16)) for q, qd in zip(p2, p2d)]
    x = [xi + _mm(xi.astype(BF16), qd) for xi, qd in zip(x, p2d)]
    x = [xi + _mm(xi.astype(BF16), qd) for xi, qd in zip(x, p4d)]
    prev = c8
    for cur in (c16, c32, None):
        lvl = jnp.logical_not(prev) if cur is None else cur & jnp.logical_not(prev)
        xb = [xi.astype(BF16) for xi in x]
        t = [_mm(jnp.where(lvl, l, 0.0).astype(BF16), bdiag(xi)).astype(BF16) for l, xi in zip(lab_c, xb)]
        x = [xi + _mm(xbi, bdiag(ti)) for xi, xbi, ti in zip(x, xb, t)]
        prev = cur
    tb = [bdiag(xi.astype(BF16)) for xi in x]

    x1 = [_mm(l, c["v"]).astype(BF16) for l, c in zip(lak, chains)]
    wu = [_mm(t, c["a"]).astype(BF16) for t, c in zip(tb, chains)]
    uv = [_mm(t, xi).astype(BF16) for t, xi in zip(tb, x1)]
    q = [(c["r"].astype(F32) + _mm(m, w)).astype(BF16) for c, m, w in zip(chains, mrb, wu)]
    yl = [_mm(m, u) + _mm(mk, c["v"]) for m, u, mk, c in zip(mrb, uv, mrk, chains)]

    g, h = [], []
    for i, c in enumerate(chains):
        gi, hi = [], []
        for ck in range(nc):
            rs = slice(ck * WKV_CHUNK, (ck + 1) * WKV_CHUNK)
            b_c = c["prep"]["bh"][rs, c["sl"]]
            k_c = c["prep"]["kh"][rs, c["sl"]]
            wrow = c["prep"]["wtot"][ck * WKV_CHUNK:ck * WKV_CHUNK + 1, c["sl"]]
            gi.append((dot_tn(b_c, wu[i][rs])
                       + jnp.where(eye_h, jnp.broadcast_to(wrow, (HEAD_DIM, HEAD_DIM)), 0.0)).astype(BF16))
            hi.append(dot_tn(b_c, uv[i][rs]) + dot_tn(k_c, c["v"][rs]))
        g.append(gi)
        h.append(hi)

    s = [s_scr[c["d"], 2 * c["p"] + c["hh"]] for c in chains]
    for step in range(nc):
        for i, c in enumerate(chains):
            ck = nc - 1 - step if c["reverse"] else step
            rs = slice(ck * WKV_CHUNK, (ck + 1) * WKV_CHUNK)
            sb = s[i].astype(BF16)
            c["y_ref"][c["p"], rs, c["sl"]] = _mm(q[i][rs], sb) + yl[i][rs]
            s[i] = _mm(g[i][ck], sb) + h[i][ck]
    for i, c in enumerate(chains):
        s_scr[c["d"], 2 * c["p"] + c["hh"]] = s[i]


def _wkv_kernel(rf_ref, vf_ref, nf_ref, kf_ref, bf_ref, lf_ref,
                rb_ref, vb_ref, nb_ref, kb_ref, bb_ref, lb_ref, yf_ref, yb_ref, s_scr):
    n = WKV_STEP

    @pl.when(pl.program_id(1) == 0)
    def _():
        s_scr[...] = jnp.zeros_like(s_scr)

    def pair(p, carry):
        row = lax.broadcasted_iota(I32, (n, n), 0)
        col = lax.broadcasted_iota(I32, (n, n), 1)
        m64 = (row >> 6) == (col >> 6)
        er = lax.broadcasted_iota(I32, (HEAD_DIM, HEAD_DIM), 0)
        ec = lax.broadcasted_iota(I32, (HEAD_DIM, HEAD_DIM), 1)
        shared = (row, col, m64)
        rc = lax.broadcasted_iota(I32, (WKV_CHUNK, n), 0)
        lc = lax.broadcasted_iota(I32, (WKV_CHUNK, n), 1)
        li = lc & (WKV_CHUNK - 1)
        samec = lambda log2: (rc >> log2) == (li >> log2)
        masks = (samec(3), samec(4), samec(5), lc >> 6, jnp.where(rc == li, 1.0, 0.0).astype(F32), er == ec,
                 jnp.where(m64, 1.0, 0.0).astype(BF16))
        chains = []
        for pp in range(N_PAIR):
            fwd = _wkv_prep(pp, (rf_ref, vf_ref, nf_ref, kf_ref, bf_ref, lf_ref), False, shared)
            bwd = _wkv_prep(pp, (rb_ref, vb_ref, nb_ref, kb_ref, bb_ref, lb_ref), True, shared)
            chains += [dict(prep=pr, hh=hh, d=d, p=pp, reverse=rev, y_ref=y_ref)
                       for pr, d, rev, y_ref in ((fwd, 0, False, yf_ref), (bwd, 1, True, yb_ref))
                       for hh in range(2)]
        _wkv_chains(chains, s_scr, masks)
        return carry

    pair(0, 0)


def _wkv(r, v, kn, k0, b0, lw0, k1, b1, lw1):
    bsz, _, t, _ = r.shape
    assert t % WKV_STEP == 0
    nj = t // WKV_STEP
    fwd = pl.BlockSpec((None, N_PAIR, WKV_STEP, LANES), lambda bi, j: (bi, 0, j, 0))
    bwd = pl.BlockSpec((None, N_PAIR, WKV_STEP, LANES), lambda bi, j: (bi, 0, nj - 1 - j, 0))
    out = jax.ShapeDtypeStruct((bsz, N_PAIR, t, LANES), F32)
    return pl.pallas_call(
        _wkv_kernel,
        grid=(bsz, nj),
        in_specs=[fwd] * 6 + [bwd] * 6,
        out_specs=(fwd, bwd),
        out_shape=(out, out),
        scratch_shapes=[pltpu.VMEM((2, 2 * N_PAIR, HEAD_DIM, HEAD_DIM), F32)],
        compiler_params=pltpu.CompilerParams(
            dimension_semantics=("parallel", "arbitrary"), vmem_limit_bytes=VMEM_LIMIT),
        name="wkv",
    )(r, v, kn, k0, b0, lw0, r, v, kn, k1, b1, lw1)


def _post_kernel(yf_ref, yb_ref, bonus_ref, g_ref, ya_ref, x_ref, gng_ref, gnb_ref, bd_ref, wout_ref,
                 nf_ref, rwh_ref, rwl_ref, rb_ref,
                 h_ref, xn_ref, gate_ref, slot_ref, cnt_ref, before_ref, run_scr):
    tt = x_ref.shape[0]
    step = pl.program_id(0)

    @pl.when(step == 0)
    def _():
        run_scr[...] = jnp.zeros_like(run_scr)

    y = jnp.concatenate([yf_ref[p] + yb_ref[p] for p in range(N_PAIR)], axis=1)
    bd = bd_ref[...]
    inv = 1.0 / HEAD_DIM
    mu = _segsum(y, bd) * inv
    yc = y - mu
    var = _segsum(yc * yc, bd) * inv
    yn = yc * lax.rsqrt(var + GN_EPS) * gng_ref[...] + gnb_ref[...] + bonus_ref[...]
    ybm = (yn * g_ref[...]).astype(BF16)
    h = x_ref[...] + _mm(ya_ref[...], wout_ref[:GA, :]) + _mm(ybm, wout_ref[GA:, :])
    h_ref[...] = h
    xn = _rms(h, nf_ref[...])
    xn_ref[...] = xn

    lane = lax.broadcasted_iota(I32, (tt, LANES), 1)
    xh, xl = _split2(xn)
    logits = _mm(xh, rwh_ref[...]) + _mm(xl, rwh_ref[...]) + _mm(xh, rwl_ref[...])
    logits = jnp.where(lane < N_EXPERTS, logits + rb_ref[...], -jnp.inf)
    vals, ids, sel = [], [], jnp.zeros((tt, LANES), F32)
    cur = logits
    for _ in range(TOP_K):
        m = jnp.max(cur, axis=-1, keepdims=True)
        idx = jnp.min(jnp.where(cur == m, lane, LANES), axis=-1, keepdims=True)
        hit = lane == idx
        vals.append(m)
        ids.append(idx)
        sel = sel + jnp.where(hit, 1.0, 0.0)
        cur = jnp.where(hit, -jnp.inf, cur)
    es = [jnp.exp(vv - vals[0]) for vv in vals]
    den = es[0] + es[1] + es[2] + es[3]

    rt = lax.broadcasted_iota(I32, (tt, tt), 0)
    ct = lax.broadcasted_iota(I32, (tt, tt), 1)
    earlier = jnp.where(ct < rt, 1.0, 0.0).astype(BF16)
    tile_cnt = jnp.sum(sel, axis=0, keepdims=True)
    tile_pad = jnp.floor((tile_cnt + (RUN_ALIGN - 1)) * (1.0 / RUN_ALIGN)) * RUN_ALIGN
    er = lax.broadcasted_iota(I32, (LANES, LANES), 0)
    ec = lax.broadcasted_iota(I32, (LANES, LANES), 1)
    lower_experts = jnp.where(er < ec, 1.0, 0.0).astype(BF16)
    tile_off = _mm(jnp.broadcast_to(tile_pad, (8, LANES)).astype(BF16), lower_experts)[0:1]
    pos = _mm(earlier, sel.astype(BF16)) + tile_off
    slot_o = jnp.full((tt, LANES), -1.0, F32)
    gate_o = jnp.zeros((tt, LANES), F32)
    for kx in range(TOP_K):
        sk = jnp.sum(jnp.where(lane == ids[kx], pos, 0.0), axis=-1, keepdims=True)
        gate_o = jnp.where(lane == kx, es[kx] / den, gate_o)
        slot_o = jnp.where(lane == kx, sk, slot_o)
    gate_ref[...] = gate_o
    slot_ref[...] = slot_o.astype(I32)
    before_ref[...] = run_scr[...]
    run = run_scr[...] + tile_cnt
    run_scr[...] = run
    cnt_ref[...] = run


def _post(yf, yb, bonus, g, ya, x, w):
    bsz, t, _ = x.shape
    n = bsz * t
    tt = min(TILE_POST, t)
    nt = t // tt

    def full(a):
        nd = a.ndim
        return pl.BlockSpec(a.shape, lambda s: (0,) * nd)

    consts = (w["b_gn_g"], w["b_gn_b"], w["bd"], w["w_out"], w["norm_ffn"], w["router_wh"], w["router_wl"],
              w["router_b"])
    pair_spec = pl.BlockSpec((None, N_PAIR, tt, LANES), lambda s: (s // nt, 0, s % nt, 0))
    tok3 = lambda width: pl.BlockSpec((None, tt, width), lambda s: (s // nt, s % nt, 0))
    tok2 = lambda width: pl.BlockSpec((tt, width), lambda s: (s, 0))
    in_specs = [pair_spec, pair_spec, tok3(GB), tok3(GB), tok3(GA), tok3(D_MODEL)] + [full(a) for a in consts]
    out_shape = (jax.ShapeDtypeStruct((n, D_MODEL), F32),
                 jax.ShapeDtypeStruct((n, D_MODEL), F32),
                 jax.ShapeDtypeStruct((n, LANES), F32),
                 jax.ShapeDtypeStruct((n, LANES), I32),
                 jax.ShapeDtypeStruct((1, LANES), F32),
                 jax.ShapeDtypeStruct((bsz * nt, 1, LANES), F32))
    out_specs = (tok2(D_MODEL), tok2(D_MODEL), tok2(LANES), tok2(LANES),
                 pl.BlockSpec((1, LANES), lambda s: (0, 0)),
                 pl.BlockSpec((None, 1, LANES), lambda s: (s, 0, 0)))
    return pl.pallas_call(
        _post_kernel,
        grid=(bsz * nt,),
        in_specs=in_specs,
        out_specs=out_specs,
        out_shape=out_shape,
        scratch_shapes=[pltpu.VMEM((1, LANES), F32)],
        compiler_params=pltpu.CompilerParams(
            dimension_semantics=("arbitrary",), vmem_limit_bytes=VMEM_LIMIT),
        name="post_router",
    )(yf, yb, bonus, g, ya, x, *consts)


RUN_ALIGN = 8
RUN_SIZES = tuple(1 << i for i in range(9, 2, -1))
RUN_SMALL = 64
TILE_SORT = TILE_ROW * TOP_K + N_EXPERTS * RUN_ALIGN
N_RUNS = N_EXPERTS + 1
DUMP_ROWS = MOE_BLOCK


def _tile_runs(sc_ref, copy, wait=False):
    def per_run(e, c):
        o0 = sc_ref[0, e]
        d0 = sc_ref[0, N_RUNS + e]
        ln = sc_ref[0, 2 * N_RUNS + e]

        def pieces(sizes, o, d):
            for size in sizes:
                bit = ln & size

                @pl.when(bit != 0)
                def _():
                    cp = copy(e, pl.multiple_of(o, RUN_ALIGN), pl.multiple_of(d, RUN_ALIGN), size)
                    if wait:
                        cp.wait()
                    else:
                        cp.start()

                o = o + bit
                d = d + bit

        big = tuple(sz for sz in RUN_SIZES if sz >= RUN_SMALL)
        small = tuple(sz for sz in RUN_SIZES if sz < RUN_SMALL)

        @pl.when(ln >= RUN_SMALL)
        def _():
            pieces(big, o0, d0)

        skip = ln & ~(RUN_SMALL - 1)
        pieces(small, o0 + skip, d0 + skip)
        return c

    lax.fori_loop(0, N_RUNS, per_run, 0)


def _slot_matrix(slot, vals, width):
    tt = slot.shape[0]
    lane = lax.broadcasted_iota(I32, (tt, width), 1)
    m = jnp.zeros((tt, width), F32)
    for kx in range(TOP_K):
        m = m + jnp.where(lane == slot[:, kx:kx + 1], vals[kx], 0.0)
    return m


def _scatter_kernel(sc_ref, zt_ref, na_ref, slot_ref, xn_ref, xs_ref, sb, sem):
    s = pl.program_id(0)
    ns = pl.num_programs(0)
    cur = s % 2
    n_blocks = (xs_ref.shape[0] - DUMP_ROWS) // MOE_BLOCK

    def runs(tab_ref, sl, wait=False):
        _tile_runs(tab_ref, lambda e, o, d, size: pltpu.make_async_copy(
            sb.at[sl, pl.ds(o, size)], xs_ref.at[pl.ds(d, size)], sem.at[sl]), wait)

    def wait_tile(sl):
        for j in range(TILE_SORT // TILE_ROW):
            pltpu.make_async_copy(sb.at[sl, pl.ds(j * TILE_ROW, TILE_ROW)], xs_ref.at[pl.ds(0, TILE_ROW)],
                                  sem.at[sl]).wait()

    def blocks(wait):
        def body(b, c):
            cp = pltpu.make_async_copy(sb.at[0, pl.ds(0, MOE_BLOCK)],
                                       xs_ref.at[pl.ds(pl.multiple_of(b * MOE_BLOCK, MOE_BLOCK), MOE_BLOCK)],
                                       sem.at[0])
            if wait:
                cp.wait()
            else:
                cp.start()
            return c
        lax.fori_loop(na_ref[0], n_blocks + DUMP_ROWS // MOE_BLOCK, body, 0)

    @pl.when(s == 0)
    def _():
        sb[0] = jnp.zeros(sb.shape[1:], U32)
        runs(zt_ref, 0)
        blocks(False)
        runs(zt_ref, 0, wait=True)
        blocks(True)

    p01 = _slot_matrix(slot_ref[...], (1.0,) * TOP_K, TILE_SORT).astype(BF16)
    sb[cur] = _pack_bf16_pairs(lax.dot_general(p01, xn_ref[...].astype(BF16), TN, preferred_element_type=F32))
    runs(sc_ref, cur)

    @pl.when(s > 0)
    def _():
        wait_tile(1 - cur)

    @pl.when(s == ns - 1)
    def _():
        wait_tile(cur)


def _scatter_rows(xn, slot, runs, ztab, n_active, rows):
    n = xn.shape[0]
    tt = TILE_ROW
    return pl.pallas_call(
        _scatter_kernel,
        grid=(n // tt,),
        in_specs=[pl.BlockSpec((None, 1, 3 * N_RUNS), lambda s: (s, 0, 0), memory_space=pltpu.SMEM),
                  pl.BlockSpec(memory_space=pltpu.SMEM),
                  pl.BlockSpec(memory_space=pltpu.SMEM),
                  pl.BlockSpec((tt, LANES), lambda s: (s, 0)),
                  pl.BlockSpec((tt, D_MODEL), lambda s: (s, 0))],
        out_specs=pl.BlockSpec(memory_space=pl.ANY),
        out_shape=jax.ShapeDtypeStruct((rows + DUMP_ROWS, D_PACK), U32),
        scratch_shapes=[pltpu.VMEM((2, TILE_SORT, D_PACK), U32), pltpu.SemaphoreType.DMA((2,))],
        compiler_params=pltpu.CompilerParams(
            dimension_semantics=("arbitrary",), vmem_limit_bytes=VMEM_LIMIT),
        name="moe_scatter",
    )(runs, ztab, n_active, slot, xn)


def _expert_kernel(be_ref, na_ref, xs_ref, w1_ref, b1_ref, w2_ref, b2_ref, o_ref):
    del be_ref
    s = pl.program_id(0)

    @pl.when(s < na_ref[0])
    def _():
        xl, xr = _unpack_bf16_pairs(xs_ref[...])
        hdn = _mm(xl, w1_ref[:D_PACK, :]) + _mm(xr, w1_ref[D_PACK:, :]) + b1_ref[...]
        glu = jnp.minimum(hdn[:, :D_FF], SWIGLU_LIMIT)
        lin = jnp.clip(hdn[:, D_FF:], -SWIGLU_LIMIT, SWIGLU_LIMIT)
        act = glu * jax.nn.sigmoid(SWIGLU_ALPHA * glu) * (lin + 1.0)
        o_ref[...] = _pack_bf16_pairs(_mm(act.astype(BF16), w2_ref[...]) + b2_ref[...])

    @pl.when(s >= na_ref[0])
    def _():
        o_ref[...] = jnp.zeros_like(o_ref)


def _experts(xs, block_e, n_active, w):
    nb = block_e.shape[0]
    rows = nb * MOE_BLOCK
    grid_spec = pltpu.PrefetchScalarGridSpec(
        num_scalar_prefetch=2,
        grid=(nb,),
        in_specs=[
            pl.BlockSpec((MOE_BLOCK, D_PACK), lambda s, be, na: (s, 0)),
            pl.BlockSpec((None, D_MODEL, 2 * D_FF), lambda s, be, na: (be[s], 0, 0)),
            pl.BlockSpec((None, 1, 2 * D_FF), lambda s, be, na: (be[s], 0, 0)),
            pl.BlockSpec((None, D_FF, D_MODEL), lambda s, be, na: (be[s], 0, 0)),
            pl.BlockSpec((None, 1, D_MODEL), lambda s, be, na: (be[s], 0, 0)),
        ],
        out_specs=pl.BlockSpec((MOE_BLOCK, D_PACK), lambda s, be, na: (s, 0)),
    )
    return pl.pallas_call(
        _expert_kernel,
        grid_spec=grid_spec,
        out_shape=jax.ShapeDtypeStruct((rows, D_PACK), U32),
        compiler_params=pltpu.CompilerParams(
            dimension_semantics=("arbitrary",), vmem_limit_bytes=VMEM_LIMIT),
        name="moe_experts",
    )(block_e, n_active, xs, w["moe_w1"], w["moe_b1"], w["moe_w2"], w["moe_b2"])


def _combine_kernel(scur_ref, snxt_ref, slot_ref, h_ref, gate_ref, p_ref, npl_ref, pg_ref, pp_ref, nfin_ref,
                    os_ref, y_ref, gb, sem):
    s = pl.program_id(0)
    ns = pl.num_programs(0)
    slot = s % 2

    def fetch(sc_ref, sl):
        _tile_runs(sc_ref, lambda e, o, d, size: pltpu.make_async_copy(
            os_ref.at[pl.ds(pl.multiple_of(jnp.where(e == N_EXPERTS, 0, d), RUN_ALIGN), size)],
            gb.at[sl, pl.ds(o, size)], sem.at[sl]))

    @pl.when(s == 0)
    def _():
        fetch(scur_ref, 0)

    @pl.when(s + 1 < ns)
    def _():
        fetch(snxt_ref, 1 - slot)

    for j in range(TILE_SORT // TILE_ROW):
        pltpu.make_async_copy(os_ref.at[pl.ds(0, TILE_ROW)], gb.at[slot, pl.ds(j * TILE_ROW, TILE_ROW)],
                              sem.at[slot]).wait()

    gate = gate_ref[...]
    pm = _slot_matrix(slot_ref[...], [gate[:, kx:kx + 1] for kx in range(TOP_K)], TILE_SORT)
    tt = pm.shape[0]
    pst = jnp.concatenate(_split2(pm), axis=0)
    gl, gr = _unpack_bf16_pairs(gb[slot])
    moe = jnp.concatenate([_mm(pst, gl), _mm(pst, gr)], axis=1)
    h = h_ref[...] + moe[:tt] + moe[tt:]
    gt = jax.nn.sigmoid(_mm(_rms(h, npl_ref[...]).astype(BF16), pg_ref[...]))
    h = h + _mm(p_ref[...].astype(BF16), pp_ref[...]) * gt
    y_ref[...] = _rms(h, nfin_ref[...])


def _combine(h, gates, slot, runs, p, os_rows, w):
    n = h.shape[0]
    tt = TILE_ROW
    ns = n // tt

    def full(a):
        nd = a.ndim
        return pl.BlockSpec(a.shape, lambda s: (0,) * nd)

    consts = (w["norm_ple"], w["ple_gate"], w["ple_proj"], w["norm_final"])
    smem = lambda fn: pl.BlockSpec((None, 1, 3 * N_RUNS), fn, memory_space=pltpu.SMEM)
    return pl.pallas_call(
        _combine_kernel,
        grid=(ns,),
        in_specs=[smem(lambda s: (s, 0, 0)),
                  smem(lambda s: (jnp.minimum(s + 1, ns - 1), 0, 0)),
                  pl.BlockSpec((tt, LANES), lambda s: (s, 0)),
                  pl.BlockSpec((tt, D_MODEL), lambda s: (s, 0)),
                  pl.BlockSpec((tt, LANES), lambda s: (s, 0)),
                  pl.BlockSpec((tt, PLE_DIM), lambda s: (s, 0))]
        + [full(a) for a in consts]
        + [pl.BlockSpec(memory_space=pl.ANY)],
        out_specs=pl.BlockSpec((tt, D_MODEL), lambda s: (s, 0)),
        out_shape=jax.ShapeDtypeStruct((n, D_MODEL), F32),
        scratch_shapes=[pltpu.VMEM((2, TILE_SORT, D_PACK), U32), pltpu.SemaphoreType.DMA((2,))],
        compiler_params=pltpu.CompilerParams(
            dimension_semantics=("arbitrary",), vmem_limit_bytes=VMEM_LIMIT),
        name="moe_combine",
    )(runs, runs, slot, h, gates, p, *consts, os_rows)


def _prep_weights(norm_mix, w_in, a_ln_g, a_ln_b, a_ws, a_bs, b_conv, b_w0, b_w2, b_a0, b_a2, b_g2,
                  b_kk, b_ka, b_rk, b_gn_g, b_gn_b, w_out, norm_ffn, router_w, router_b, moe_w1,
                  moe_b1, moe_w2, moe_b2, norm_ple, ple_proj, ple_gate, norm_final):
    row = lambda a: a.reshape(1, -1).astype(F32)

    def lora_pad(m):
        z = jnp.zeros((2, 2 * LORA, GB), F32)
        z = z.at[0, :LORA].set(m[0]).at[1, LORA:].set(m[1])
        return z.astype(BF16)

    seg = jnp.arange(GB, dtype=I32) // HEAD_DIM
    rw = jnp.pad(router_w[0].astype(F32), ((0, 0), (0, LANES - N_EXPERTS)))
    rw_hi = rw.astype(BF16)
    return {
        "norm_mix": row(norm_mix[0]),
        "w_in": w_in[0].astype(BF16),
        "a_ln_g": row(a_ln_g[0]),
        "a_ln_b": row(a_ln_b[0]),
        "a_ws": a_ws[0].reshape(-1, CHUNK_A).astype(BF16),
        "a_bs": jnp.repeat(a_bs[0].T.astype(F32), HEAD_DIM, axis=1),
        "b_conv": b_conv[0].astype(F32),
        "b_w0": b_w0[0].astype(F32),
        "b_w2": lora_pad(b_w2[0]),
        "b_a0": b_a0[0].astype(F32),
        "b_a2": lora_pad(b_a2[0]),
        "b_g2": b_g2[0].astype(BF16),
        "b_kk": row(b_kk[0]),
        "b_ka": row(b_ka[0]),
        "b_rk": row(b_rk[0]),
        "bd": (seg[:, None] == seg[None, :]).astype(BF16),
        "b_gn_g": row(b_gn_g[0]),
        "b_gn_b": row(b_gn_b[0]),
        "w_out": w_out[0].astype(BF16),
        "norm_ffn": row(norm_ffn[0]),
        "router_wh": rw_hi,
        "router_wl": (rw - rw_hi.astype(F32)).astype(BF16),
        "router_b": jnp.pad(router_b[0].astype(F32), (0, LANES - N_EXPERTS)).reshape(1, LANES),
        "moe_w1": moe_w1[0].astype(BF16),
        "moe_b1": moe_b1[0].astype(F32).reshape(N_EXPERTS, 1, 2 * D_FF),
        "moe_w2": moe_w2[0].astype(BF16),
        "moe_b2": moe_b2[0].astype(F32).reshape(N_EXPERTS, 1, D_MODEL),
        "norm_ple": row(norm_ple[0]),
        "ple_gate": ple_gate[0].astype(BF16),
        "ple_proj": ple_proj[0].astype(BF16),
        "norm_final": row(norm_final),
    }


def _forward(x, p, w):
    bsz, t, _ = x.shape
    n = bsz * t
    ya, r, v, kn, lw0, lw1, k0, k1, b0, b1, g, bonus = _inproj(x, w)
    yf, yb = _wkv(r, v, kn, k0, b0, lw0, k1, b1, lw1)
    h, xn, gates, slot, counts, before = _post(yf, yb, bonus, g, ya, x, w)

    nt = n // TILE_ROW
    counts = counts[0, :N_EXPERTS].astype(I32)
    before = before[:, 0, :N_EXPERTS].astype(I32)
    tile_cnt = jnp.concatenate([before[1:], counts[None]], axis=0) - before
    tile_pad = (tile_cnt + RUN_ALIGN - 1) // RUN_ALIGN * RUN_ALIGN
    tile_off = jnp.cumsum(tile_pad, axis=1) - tile_pad
    before_pad = jnp.cumsum(tile_pad, axis=0) - tile_pad
    padded = (jnp.sum(tile_pad, axis=0) + MOE_BLOCK - 1) // MOE_BLOCK * MOE_BLOCK
    pends = jnp.cumsum(padded)
    pstarts = pends - padded
    n_blocks = -(-(n * TOP_K + nt * N_EXPERTS * (RUN_ALIGN - 1)) // MOE_BLOCK) + N_EXPERTS
    block_start = jnp.arange(n_blocks, dtype=I32) * MOE_BLOCK
    block_e = jnp.minimum(jnp.sum(pends[None, :] <= block_start[:, None], axis=1), N_EXPERTS - 1).astype(I32)
    n_active = (pends[-1:] // MOE_BLOCK).astype(I32)
    used = jnp.sum(tile_pad, axis=1, keepdims=True)
    dump = n_blocks * MOE_BLOCK + (jnp.arange(nt, dtype=I32)[:, None] % 2) * TILE_ROW
    runs = jnp.concatenate([tile_off, used, pstarts[None, :] + before_pad, dump, tile_pad, TILE_SORT - used],
                           axis=1).reshape(nt, 1, 3 * N_RUNS).astype(I32)
    rows_used = jnp.sum(tile_pad, axis=0)
    tails = padded - rows_used
    zero1 = jnp.zeros((1,), I32)
    ztab = jnp.concatenate([jnp.zeros_like(padded), zero1, pstarts + rows_used, zero1, tails, zero1])
    ztab = ztab.reshape(1, 3 * N_RUNS).astype(I32)

    xs = _scatter_rows(xn, slot, runs, ztab, n_active, n_blocks * MOE_BLOCK)
    os_rows = _experts(xs, block_e, n_active, w)
    y = _combine(h, gates, slot, runs, p.reshape(n, PLE_DIM), os_rows, w)
    return y.reshape(bsz, t, D_MODEL)


def kernel(x_prompt, x_sample, p_prompt, p_sample, norm_mix, w_in, a_ln_g, a_ln_b, a_ws, a_bs, b_conv, b_w0, b_w2, b_a0, b_a2, b_g2, b_kk, b_ka, b_rk, b_gn_g, b_gn_b, w_out, norm_ffn, router_w, router_b, moe_w1, moe_b1, moe_w2, moe_b2, norm_ple, ple_proj, ple_gate, norm_final):
    assert norm_mix.shape[0] == 1, "single-layer trunk"
    w = _prep_weights(norm_mix, w_in, a_ln_g, a_ln_b, a_ws, a_bs, b_conv, b_w0, b_w2, b_a0, b_a2, b_g2,
                      b_kk, b_ka, b_rk, b_gn_g, b_gn_b, w_out, norm_ffn, router_w, router_b, moe_w1,
                      moe_b1, moe_w2, moe_b2, norm_ple, ple_proj, ple_gate, norm_final)
    y_prompt = _forward(x_prompt, p_prompt[0], w)
    y_sample = _forward(x_sample, p_sample[0], w)
    return (y_prompt, y_sample)
```
